```python
import jax, jax.numpy as jnp
from jax import lax
import numpy as np

D_MODEL = 1024
BATCH = 8
SEQ = 4096
DEPTH = 1

CHUNK = 64
HEAD_DIM = 64
MIX_WIDTH = D_MODEL
A_HEADS = (MIX_WIDTH // 2) // HEAD_DIM
A_KV_HEADS = 2
A_GROUP = A_HEADS // A_KV_HEADS
A_WINDOW = 128
A_BAND_CHUNKS = -(-A_WINDOW // CHUNK) + 1
B_HEADS = (MIX_WIDTH - A_HEADS * HEAD_DIM) // HEAD_DIM
B_LEFT_CHUNKS = 8
B_BAND_CHUNKS = B_LEFT_CHUNKS + 1
B_MAX_REL = 128
D_FF = 4 * D_MODEL
EPS = 1e-6
NEG_INF = -1e30

QA_W = A_HEADS * HEAD_DIM
KA_W = A_KV_HEADS * HEAD_DIM
QB_W = B_HEADS * HEAD_DIM
PROJ_W = QA_W + 2 * KA_W + 3 * QB_W

kernel_name = "hybrid_swa_sink_chunk_relpos_block"


def rms_norm(x, g):
    xf = x.astype(jnp.float32)
    y = xf * lax.rsqrt(jnp.mean(xf * xf, axis=-1, keepdims=True) + EPS)
    return (y * g.astype(jnp.float32)).astype(x.dtype)


def alibi_slopes(n_heads):
    return jnp.exp2(-8.0 * (jnp.arange(n_heads, dtype=jnp.float32) + 1.0) / n_heads)


def chunk_band(t, n_band):
    b, s, h, d = t.shape
    nc = s // CHUNK
    tc = t.reshape(b, nc, CHUNK, h, d)
    tc = jnp.pad(tc, ((0, 0), (n_band - 1, 0), (0, 0), (0, 0), (0, 0)))
    idx = jnp.arange(nc)[:, None] + jnp.arange(n_band)[None, :]
    band = tc[:, idx]
    return band.reshape(b, nc, n_band * CHUNK, h, d)


def band_valid(nc, n_band):
    kc = jnp.arange(nc)[:, None] - (n_band - 1) + jnp.arange(n_band)[None, :]
    return jnp.repeat(kc >= 0, CHUNK, axis=1)


def band_distance(n_band):
    i = jnp.arange(CHUNK)[:, None]
    j = jnp.arange(n_band * CHUNK)[None, :]
    return (n_band - 1) * CHUNK + i - j


def mixer_a(q, k, v, sinks):
    b, s = q.shape[:2]
    nc = s // CHUNK
    qc = q.reshape(b, nc, CHUNK, A_KV_HEADS, A_GROUP, HEAD_DIM)
    kb = chunk_band(k, A_BAND_CHUNKS)
    vb = chunk_band(v, A_BAND_CHUNKS)
    scores = jnp.einsum('bnqhgd,bnshd->bnhgqs', qc, kb).astype(jnp.float32) * (HEAD_DIM ** -0.5)
    dist = jnp.abs(band_distance(A_BAND_CHUNKS)).astype(jnp.float32)
    slopes = alibi_slopes(A_HEADS).reshape(A_KV_HEADS, A_GROUP)
    scores = scores - slopes[:, :, None, None] * dist
    valid = band_valid(nc, A_BAND_CHUNKS)
    scores = jnp.where(valid[None, :, None, None, None, :], scores, NEG_INF)
    sink = sinks.astype(jnp.float32).reshape(A_KV_HEADS, A_GROUP)[None, None, :, :, None, None]
    m = jnp.maximum(jnp.max(scores, axis=-1, keepdims=True), sink)
    p = jnp.exp(scores - m)
    probs = (p / (jnp.sum(p, axis=-1, keepdims=True) + jnp.exp(sink - m))).astype(v.dtype)
    out = jnp.einsum('bnhgqs,bnshd->bnqhgd', probs, vb)
    return out.reshape(b, s, QA_W)


def mixer_b(q, k, v, rel_bias):
    b, s = q.shape[:2]
    nc = s // CHUNK
    qc = q.reshape(b, nc, CHUNK, B_HEADS, HEAD_DIM)
    kb = chunk_band(k, B_BAND_CHUNKS)
    vb = chunk_band(v, B_BAND_CHUNKS)
    scores = jnp.einsum('bnqhd,bnshd->bnhqs', qc, kb).astype(jnp.float32) * (HEAD_DIM ** -0.5)
    rel = jnp.clip(band_distance(B_BAND_CHUNKS), -B_MAX_REL, B_MAX_REL) + B_MAX_REL
    bias = rel_bias.astype(jnp.float32)[:, rel]
    valid = band_valid(nc, B_BAND_CHUNKS)
    scores = jnp.where(valid[None, :, None, None, :], scores + bias, NEG_INF)
    probs = jax.nn.softmax(scores, axis=-1).astype(v.dtype)
    out = jnp.einsum('bnhqs,bnshd->bnqhd', probs, vb)
    return out.reshape(b, s, QB_W)


def setup_inputs(seed: int = 0) -> dict:
    key = jax.random.key(seed)
    ks = jax.random.split(key, 13)
    f32 = jnp.float32

    def nrm(k, shape, scale):
        return jax.random.normal(k, shape, f32) * scale

    return {
        "x": nrm(ks[0], (BATCH, SEQ, D_MODEL), 1.0),
        "norm1_g": 1.0 + nrm(ks[1], (DEPTH, D_MODEL), 0.02),
        "w_in": nrm(ks[2], (DEPTH, D_MODEL, PROJ_W), D_MODEL ** -0.5),
        "sinks_a": nrm(ks[3], (DEPTH, A_HEADS), 0.5),
        "rel_bias_b": nrm(ks[4], (DEPTH, B_HEADS, 2 * B_MAX_REL + 1), 0.1),
        "out_norm_a_g": 1.0 + nrm(ks[5], (DEPTH, QA_W), 0.02),
        "out_norm_b_g": 1.0 + nrm(ks[6], (DEPTH, QB_W), 0.02),
        "w_out": nrm(ks[7], (DEPTH, MIX_WIDTH, D_MODEL), MIX_WIDTH ** -0.5),
        "norm2_g": 1.0 + nrm(ks[8], (DEPTH, D_MODEL), 0.02),
        "w_ff1": nrm(ks[9], (DEPTH, D_MODEL, D_FF), D_MODEL ** -0.5),
        "w_ff2": nrm(ks[10], (DEPTH, D_FF, D_MODEL), D_FF ** -0.5),
        "final_norm_g": 1.0 + nrm(ks[11], (D_MODEL,), 0.02),
    }


def reference(x, norm1_g, w_in, sinks_a, rel_bias_b, out_norm_a_g, out_norm_b_g,
              w_out, norm2_g, w_ff1, w_ff2, final_norm_g):
    b, s, _ = x.shape
    split_at = np.cumsum([QA_W, KA_W, KA_W, QB_W, QB_W])
    h = x
    for layer in range(DEPTH):
        n = rms_norm(h, norm1_g[layer])
        proj = jnp.einsum('bsd,dp->bsp', n, w_in[layer])
        qa, ka, va, qb, kb, vb = jnp.split(proj, split_at, axis=-1)
        ya = mixer_a(qa.reshape(b, s, A_HEADS, HEAD_DIM),
                     ka.reshape(b, s, A_KV_HEADS, HEAD_DIM),
                     va.reshape(b, s, A_KV_HEADS, HEAD_DIM),
                     sinks_a[layer])
        yb = mixer_b(qb.reshape(b, s, B_HEADS, HEAD_DIM),
                     kb.reshape(b, s, B_HEADS, HEAD_DIM),
                     vb.reshape(b, s, B_HEADS, HEAD_DIM),
                     rel_bias_b[layer])
        y = jnp.concatenate([rms_norm(ya, out_norm_a_g[layer]),
                             rms_norm(yb, out_norm_b_g[layer])], axis=-1)
        h = h + jnp.einsum('bsm,md->bsd', y, w_out[layer])
        n2 = rms_norm(h, norm2_g[layer])
        u = jnp.square(jax.nn.relu(jnp.einsum('bsd,df->bsf', n2, w_ff1[layer])))
        h = h + jnp.einsum('bsf,fd->bsd', u, w_ff2[layer])
    return rms_norm(h, final_norm_g)
```

```python
import functools

import jax
import jax.numpy as jnp
import numpy as np
from jax import lax
from jax.experimental import pallas as pl
from jax.experimental.pallas import tpu as pltpu

D_MODEL = 1024
CHUNK = 64
HEAD_DIM = 64
A_HEADS = 8
A_KV_HEADS = 2
A_GROUP = A_HEADS // A_KV_HEADS
A_BAND_CHUNKS = 3
B_HEADS = 8
B_BAND_CHUNKS = 9
B_MAX_REL = 128
D_FF = 4 * D_MODEL
EPS = 1e-6
NEG_INF = -1e30

QA_W = A_HEADS * HEAD_DIM
KA_W = A_KV_HEADS * HEAD_DIM
QB_W = B_HEADS * HEAD_DIM
PROJ_W = QA_W + 2 * KA_W + 3 * QB_W

LANES = 128
TQ = 256
TQ_CHUNKS = TQ // CHUNK
A_PAIR = 2 * CHUNK
A_WIN = 4 * CHUNK
B_WIN = 3 * TQ
TM_PROJ = 512
TM_FFN = 512
FF_CHUNK = 512
VMEM_LIMIT = 56 * 1024 * 1024

F32 = jnp.float32
BF16 = jnp.bfloat16


def _rms(x, g):
    ms = jnp.mean(x * x, axis=-1, keepdims=True)
    return x * lax.rsqrt(ms + EPS) * g


def _dot(a, b):
    return jnp.dot(a, b, preferred_element_type=F32)


def _dot_nt(a, b):
    return lax.dot_general(a, b, (((1,), (1,)), ((), ())), preferred_element_type=F32)


def _proj_kernel(x_ref, g_ref, w_ref, o_ref):
    n = _rms(x_ref[...], g_ref[...]).astype(BF16)
    o_ref[...] = _dot(n, w_ref[...]).astype(BF16)


def _const_spec(shape):
    nd = len(shape)
    return pl.BlockSpec(shape, lambda *_: (0,) * nd, pipeline_mode=pl.Buffered(1))


def _proj_call(x2, g, w):
    t = x2.shape[0]
    return pl.pallas_call(
        _proj_kernel,
        grid=(t // TM_PROJ,),
        in_specs=[
            pl.BlockSpec((TM_PROJ, D_MODEL), lambda i: (i, 0)),
            _const_spec((1, D_MODEL)),
            _const_spec((D_MODEL, PROJ_W)),
        ],
        out_specs=pl.BlockSpec((TM_PROJ, PROJ_W), lambda i: (i, 0)),
        out_shape=jax.ShapeDtypeStruct((t, PROJ_W), BF16),
        compiler_params=pltpu.CompilerParams(
            dimension_semantics=("arbitrary",), vmem_limit_bytes=VMEM_LIMIT),
        name="norm_in_proj",
    )(x2, g, w)


def _attn_kernel(qa_ref, qb_ref, kb0_ref, kb1_ref, kb2_ref, vb0_ref, vb1_ref, vb2_ref,
                 kap_ref, kac_ref, vap_ref, vac_ref,
                 bias_a_ref, sink_ref, bias_b_ref, ga_ref, gb_ref,
                 y_ref, yb_scr):
    j = pl.program_id(1)
    lo_q = lax.broadcasted_iota(jnp.int32, (TQ, LANES), 1) < HEAD_DIM
    lo_p = lax.broadcasted_iota(jnp.int32, (A_PAIR, LANES), 1) < HEAD_DIM

    def lane_half(mask, half):
        return mask if half == 0 else jnp.logical_not(mask)

    kb_refs = (kb0_ref, kb1_ref, kb2_ref)
    vb_refs = (vb0_ref, vb1_ref, vb2_ref)
    pen_b = [jnp.where(j - 2 + p >= 0, 0.0, NEG_INF).astype(F32) for p in range(2)]
    for hp in range(B_HEADS // 2):
        cols = slice(hp * LANES, (hp + 1) * LANES)
        qpair = qb_ref[0, :, cols]
        kps = [r[0, :, cols] for r in kb_refs]
        vps = [r[0, :, cols] for r in vb_refs]
        outs = []
        for half in range(2):
            h = 2 * hp + half
            msk = lane_half(lo_q, half)
            qm = jnp.where(msk, qpair, jnp.zeros_like(qpair))
            s = []
            for p in range(3):
                sp = _dot_nt(qm, kps[p]) + bias_b_ref[h, :, p * TQ:(p + 1) * TQ]
                if p < 2:
                    sp = sp + pen_b[p]
                s.append(sp)
            m = jnp.max(jnp.maximum(jnp.maximum(s[0], s[1]), s[2]), axis=-1, keepdims=True)
            o = None
            for p in range(3):
                pr = jnp.exp(s[p] - m).astype(BF16)
                vm = jnp.where(msk, vps[p], jnp.ones_like(vps[p]))
                op = _dot(pr, vm)
                o = op if o is None else o + op
            outs.append(o)
        num = jnp.where(lo_q, outs[0], outs[1])
        den = pltpu.roll(jnp.where(lo_q, outs[1], outs[0]), HEAD_DIM, 1)
        yb_scr[:, cols] = num / den
    yb = yb_scr[...]
    y_ref[0, :, QA_W:] = _rms(yb, gb_ref[...]).astype(BF16)

    pen_a = jnp.where(j >= 1, 0.0, NEG_INF).astype(F32)
    for r in range(TQ // A_PAIR):
        rows = slice(r * A_PAIR, (r + 1) * A_PAIR)
        if r == 0:
            kwin = jnp.concatenate([kap_ref[0], kac_ref[0, :A_PAIR, :]], axis=0)
            vwin = jnp.concatenate([vap_ref[0], vac_ref[0, :A_PAIR, :]], axis=0)
        else:
            kwin = kac_ref[0]
            vwin = vac_ref[0]
        o_kv = []
        e_kv = []
        for kvh in range(A_KV_HEADS):
            msk = lane_half(lo_p, kvh)
            qst = jnp.concatenate(
                [jnp.where(msk, qa_ref[0, rows, g * LANES:(g + 1) * LANES],
                           jnp.zeros((A_PAIR, LANES), BF16)) for g in range(A_GROUP)], axis=0)
            s = _dot_nt(qst, kwin) + bias_a_ref[kvh]
            s0 = s[:, :A_PAIR]
            s1 = s[:, A_PAIR:]
            if r == 0:
                s0 = s0 + pen_a
            sink = sink_ref[kvh]
            m = jnp.maximum(jnp.max(jnp.maximum(s0, s1), axis=-1, keepdims=True), sink)
            pr = jnp.concatenate([jnp.exp(s0 - m), jnp.exp(s1 - m)], axis=1).astype(BF16)
            mskv = jnp.concatenate([msk, msk], axis=0)
            vm = jnp.where(mskv, vwin, jnp.ones_like(vwin))
            o_kv.append(_dot(pr, vm))
            e_kv.append(jnp.exp(sink - m))
        tiles = []
        for g in range(A_GROUP):
            gr = slice(g * A_PAIR, (g + 1) * A_PAIR)
            o0, o1 = o_kv[0][gr], o_kv[1][gr]
            e0, e1 = e_kv[0][gr], e_kv[1][gr]
            num = jnp.where(lo_p, o0, o1)
            den = pltpu.roll(jnp.where(lo_p, o1, o0), HEAD_DIM, 1) + jnp.where(lo_p, e0, e1)
            tiles.append(num / den)
        ya = jnp.concatenate(tiles, axis=1)
        y_ref[0, rows, :QA_W] = _rms(ya, ga_ref[...]).astype(BF16)


def _attn_call(proj3, bias_a, sink_a, bias_b, ga, gb):
    b, s, _ = proj3.shape
    nq = s // TQ
    ka_col = (QA_W + 3 * QB_W) // KA_W
    va_col = ka_col + 1

    def blk(col):
        return pl.BlockSpec((1, TQ, QB_W), lambda bi, j: (bi, j, col))

    def prev_blk(col, back):
        return pl.BlockSpec((1, TQ, QB_W), lambda bi, j: (bi, jnp.maximum(j - back, 0), col))

    in_specs = [
        blk(0), blk(1),
        prev_blk(2, 2), prev_blk(2, 1), blk(2),
        prev_blk(3, 2), prev_blk(3, 1), blk(3),
        pl.BlockSpec((1, A_PAIR, KA_W), lambda bi, j: (bi, jnp.maximum(2 * j - 1, 0), ka_col)),
        pl.BlockSpec((1, TQ, KA_W), lambda bi, j: (bi, j, ka_col)),
        pl.BlockSpec((1, A_PAIR, KA_W), lambda bi, j: (bi, jnp.maximum(2 * j - 1, 0), va_col)),
        pl.BlockSpec((1, TQ, KA_W), lambda bi, j: (bi, j, va_col)),
        _const_spec(bias_a.shape), _const_spec(sink_a.shape), _const_spec(bias_b.shape),
        _const_spec(ga.shape), _const_spec(gb.shape),
    ]
    return pl.pallas_call(
        _attn_kernel,
        grid=(b, nq),
        in_specs=in_specs,
        out_specs=pl.BlockSpec((1, TQ, QA_W + QB_W), lambda bi, j: (bi, j, 0)),
        out_shape=jax.ShapeDtypeStruct((b, s, QA_W + QB_W), BF16),
        scratch_shapes=[pltpu.VMEM((TQ, QB_W), F32)],
        compiler_params=pltpu.CompilerParams(
            dimension_semantics=("arbitrary", "arbitrary"), vmem_limit_bytes=VMEM_LIMIT),
        name="attention",
    )(proj3, proj3, proj3, proj3, proj3, proj3, proj3, proj3, proj3, proj3, proj3, proj3,
      bias_a, sink_a, bias_b, ga, gb)


def _ffn_kernel(y_ref, x_ref, wo_ref, g2_ref, w1_ref, w2_ref, gf_ref, o_ref, h_scr, n2_scr):
    h = x_ref[...] + _dot(y_ref[...], wo_ref[...])
    h_scr[...] = h
    n2_scr[...] = _rms(h, g2_ref[...]).astype(BF16)

    def body(c, carry):
        u = _dot(n2_scr[...], w1_ref[c])
        u = jnp.square(jnp.maximum(u, 0.0)).astype(BF16)
        h_scr[...] += _dot(u, w2_ref[c])
        return carry

    lax.fori_loop(0, D_FF // FF_CHUNK, body, 0)
    o_ref[...] = _rms(h_scr[...], gf_ref[...])


def _ffn_call(y2, x2, wo, g2, w1, w2, gf):
    t = x2.shape[0]
    return pl.pallas_call(
        _ffn_kernel,
        grid=(t // TM_FFN,),
        in_specs=[
            pl.BlockSpec((TM_FFN, D_MODEL), lambda i: (i, 0)),
            pl.BlockSpec((TM_FFN, D_MODEL), lambda i: (i, 0)),
            _const_spec(wo.shape), _const_spec(g2.shape),
            _const_spec(w1.shape), _const_spec(w2.shape), _const_spec(gf.shape),
        ],
        out_specs=pl.BlockSpec((TM_FFN, D_MODEL), lambda i: (i, 0)),
        out_shape=jax.ShapeDtypeStruct((t, D_MODEL), F32),
        scratch_shapes=[pltpu.VMEM((TM_FFN, D_MODEL), F32), pltpu.VMEM((TM_FFN, D_MODEL), BF16)],
        compiler_params=pltpu.CompilerParams(
            dimension_semantics=("arbitrary",), vmem_limit_bytes=VMEM_LIMIT),
        name="out_proj_mlp",
    )(y2, x2, wo, g2, w1, w2, gf)


def _perm_a():
    cols = []
    for g in range(A_GROUP):
        for kvh in range(A_KV_HEADS):
            h = kvh * A_GROUP + g
            cols.extend(range(h * HEAD_DIM, (h + 1) * HEAD_DIM))
    return np.asarray(cols, np.int32)


def _bias_a_table(sinks):
    i = np.arange(A_PAIR)[:, None]
    k = np.arange(A_WIN)[None, :]
    dist = np.abs(A_PAIR + i - k).astype(np.float32)
    qc = i // CHUNK
    kc = k // CHUNK
    allowed = (kc >= qc) & (kc <= qc + A_BAND_CHUNKS - 1)
    slopes = jnp.exp2(-8.0 * (jnp.arange(A_HEADS, dtype=F32) + 1.0) / A_HEADS)
    bias = -slopes[:, None, None] * jnp.asarray(dist)[None]
    bias = jnp.where(jnp.asarray(allowed)[None], bias, NEG_INF)
    bias = bias.reshape(A_KV_HEADS, A_GROUP * A_PAIR, A_WIN)
    sink = jnp.broadcast_to(sinks.astype(F32).reshape(A_KV_HEADS, A_GROUP, 1, 1),
                            (A_KV_HEADS, A_GROUP, A_PAIR, 1)).reshape(A_KV_HEADS, A_GROUP * A_PAIR, 1)
    return bias, sink


def _bias_b_table(rel_bias):
    q = np.arange(TQ)[:, None]
    k = np.arange(B_WIN)[None, :]
    dist = (B_BAND_CHUNKS - 1) * CHUNK + q - k
    rel = np.clip(dist, -B_MAX_REL, B_MAX_REL) + B_MAX_REL
    qc = q // CHUNK
    kc = k // CHUNK
    allowed = (kc >= qc) & (kc <= qc + B_BAND_CHUNKS - 1)
    bias = rel_bias.astype(F32)[:, jnp.asarray(rel)]
    return jnp.where(jnp.asarray(allowed)[None], bias, NEG_INF)


def kernel(x, norm1_g, w_in, sinks_a, rel_bias_b, out_norm_a_g, out_norm_b_g, w_out, norm2_g,
           w_ff1, w_ff2, final_norm_g):
    b, s, d = x.shape
    assert d == D_MODEL and s % TQ == 0 and (b * s) % TM_PROJ == 0 and (b * s) % TM_FFN == 0
    assert norm1_g.shape[0] == 1, "single-layer block"
    perm = _perm_a()
    scale = HEAD_DIM ** -0.5

    w = w_in[0]
    o_ka, o_va, o_qb, o_kb, o_vb = QA_W, QA_W + KA_W, QA_W + 2 * KA_W, QA_W + 2 * KA_W + QB_W, QA_W + 2 * KA_W + 2 * QB_W
    w_perm = jnp.concatenate([
        w[:, :QA_W][:, perm] * scale,
        w[:, o_qb:o_kb] * scale,
        w[:, o_kb:o_vb],
        w[:, o_vb:],
        w[:, o_ka:o_va],
        w[:, o_va:o_qb],
    ], axis=1).astype(BF16)

    x2 = x.reshape(b * s, d)
    proj = _proj_call(x2, norm1_g[0].reshape(1, d), w_perm)

    bias_a, sink_a = _bias_a_table(sinks_a[0])
    bias_b = _bias_b_table(rel_bias_b[0])
    ga = out_norm_a_g[0][perm].reshape(1, QA_W).astype(F32)
    gb = out_norm_b_g[0].reshape(1, QB_W).astype(F32)
    y = _attn_call(proj.reshape(b, s, PROJ_W), bias_a, sink_a, bias_b, ga, gb)

    wo = jnp.concatenate([w_out[0][:QA_W][perm], w_out[0][QA_W:]], axis=0).astype(BF16)
    nc = D_FF // FF_CHUNK
    w1 = w_ff1[0].reshape(d, nc, FF_CHUNK).transpose(1, 0, 2).astype(BF16)
    w2 = w_ff2[0].reshape(nc, FF_CHUNK, d).astype(BF16)
    out = _ffn_call(y.reshape(b * s, d), x2, wo, norm2_g[0].reshape(1, d), w1, w2,
                    final_norm_g.reshape(1, d))
    return out.reshape(b, s, d)
```

```python
import functools

import jax
import jax.numpy as jnp
import numpy as np
from jax import lax
from jax.experimental import pallas as pl
from jax.experimental.pallas import tpu as pltpu

D_MODEL = 1024
CHUNK = 64
HEAD_DIM = 64
A_HEADS = 8
A_KV_HEADS = 2
A_GROUP = A_HEADS // A_KV_HEADS
A_BAND_CHUNKS = 3
B_HEADS = 8
B_BAND_CHUNKS = 9
B_MAX_REL = 128
D_FF = 4 * D_MODEL
EPS = 1e-6
NEG_INF = -1e30

QA_W = A_HEADS * HEAD_DIM
KA_W = A_KV_HEADS * HEAD_DIM
QB_W = B_HEADS * HEAD_DIM
PROJ_W = QA_W + 2 * KA_W + 3 * QB_W

LANES = 128
TQ = 256
TQ_CHUNKS = TQ // CHUNK
A_PAIR = 2 * CHUNK
A_WIN = 4 * CHUNK
B_WIN = 3 * TQ
TM_PROJ = 512
TM_FFN = 512
FF_CHUNK = 512
VMEM_LIMIT = 56 * 1024 * 1024

F32 = jnp.float32
BF16 = jnp.bfloat16


def _rms(x, g):
    ms = jnp.mean(x * x, axis=-1, keepdims=True)
    return x * lax.rsqrt(ms + EPS) * g


def _dot(a, b):
    return jnp.dot(a, b, preferred_element_type=F32)


def _dot_nt(a, b):
    return lax.dot_general(a, b, (((1,), (1,)), ((), ())), preferred_element_type=F32)


def _proj_kernel(x_ref, g_ref, w_ref, o_ref):
    n = _rms(x_ref[...], g_ref[...]).astype(BF16)
    o_ref[...] = _dot(n, w_ref[...]).astype(BF16)


def _const_spec(shape):
    nd = len(shape)
    return pl.BlockSpec(shape, lambda *_: (0,) * nd, pipeline_mode=pl.Buffered(1))


def _proj_call(x2, g, w):
    t = x2.shape[0]
    return pl.pallas_call(
        _proj_kernel,
        grid=(t // TM_PROJ,),
        in_specs=[
            pl.BlockSpec((TM_PROJ, D_MODEL), lambda i: (i, 0)),
            _const_spec((1, D_MODEL)),
            _const_spec((D_MODEL, PROJ_W)),
        ],
        out_specs=pl.BlockSpec((TM_PROJ, PROJ_W), lambda i: (i, 0)),
        out_shape=jax.ShapeDtypeStruct((t, PROJ_W), BF16),
        compiler_params=pltpu.CompilerParams(
            dimension_semantics=("arbitrary",), vmem_limit_bytes=VMEM_LIMIT),
        name="norm_in_proj",
    )(x2, g, w)


def _attn_kernel(qa_ref, qb_ref, kb0_ref, kb1_ref, kb2_ref, vb0_ref, vb1_ref, vb2_ref,
                 kap_ref, kac_ref, vap_ref, vac_ref,
                 bias_a_ref, sink_ref, bias_b_ref, ga_ref, gb_ref,
                 y_ref, yb_scr):
    j = pl.program_id(1)
    lo_q = lax.broadcasted_iota(jnp.int32, (TQ, LANES), 1) < HEAD_DIM
    lo_p = lax.broadcasted_iota(jnp.int32, (A_PAIR, LANES), 1) < HEAD_DIM

    def lane_half(mask, half):
        return mask if half == 0 else jnp.logical_not(mask)

    kb_refs = (kb0_ref, kb1_ref, kb2_ref)
    vb_refs = (vb0_ref, vb1_ref, vb2_ref)
    pen_b = [jnp.where(j - 2 + p >= 0, 0.0, NEG_INF).astype(F32) for p in range(2)]
    for hp in range(B_HEADS // 2):
        cols = slice(hp * LANES, (hp + 1) * LANES)
        qpair = qb_ref[0, :, cols]
        kps = [r[0, :, cols] for r in kb_refs]
        vps = [r[0, :, cols] for r in vb_refs]
        outs = []
        for half in range(2):
            h = 2 * hp + half
            msk = lane_half(lo_q, half)
            qm = jnp.where(msk, qpair, jnp.zeros_like(qpair))
            s = []
            for p in range(3):
                sp = _dot_nt(qm, kps[p]) + bias_b_ref[h, :, p * TQ:(p + 1) * TQ]
                if p < 2:
                    sp = sp + pen_b[p]
                s.append(sp)
            m = jnp.max(jnp.maximum(jnp.maximum(s[0], s[1]), s[2]), axis=-1, keepdims=True)
            o = None
            for p in range(3):
                pr = jnp.exp(s[p] - m).astype(BF16)
                vm = jnp.where(msk, vps[p], jnp.ones_like(vps[p]))
                op = _dot(pr, vm)
                o = op if o is None else o + op
            outs.append(o)
        num = jnp.where(lo_q, outs[0], outs[1])
        den = pltpu.roll(jnp.where(lo_q, outs[1], outs[0]), HEAD_DIM, 1)
        yb_scr[:, cols] = num / den
    yb = yb_scr[...]
    y_ref[0, :, QA_W:] = _rms(yb, gb_ref[...]).astype(BF16)

    pen_a = jnp.where(j >= 1, 0.0, NEG_INF).astype(F32)
    for r in range(TQ // A_PAIR):
        rows = slice(r * A_PAIR, (r + 1) * A_PAIR)
        if r == 0:
            kwin = jnp.concatenate([kap_ref[0], kac_ref[0, :A_PAIR, :]], axis=0)
            vwin = jnp.concatenate([vap_ref[0], vac_ref[0, :A_PAIR, :]], axis=0)
        else:
            kwin = kac_ref[0]
            vwin = vac_ref[0]
        o_kv = []
        e_kv = []
        for kvh in range(A_KV_HEADS):
            msk = lane_half(lo_p, kvh)
            qst = jnp.concatenate(
                [jnp.where(msk, qa_ref[0, rows, g * LANES:(g + 1) * LANES],
                           jnp.zeros((A_PAIR, LANES), BF16)) for g in range(A_GROUP)], axis=0)
            s = _dot_nt(qst, kwin) + bias_a_ref[kvh]
            s0 = s[:, :A_PAIR]
            s1 = s[:, A_PAIR:]
            if r == 0:
                s0 = s0 + pen_a
            sink = sink_ref[kvh]
            m = jnp.maximum(jnp.max(jnp.maximum(s0, s1), axis=-1, keepdims=True), sink)
            pr = jnp.concatenate([jnp.exp(s0 - m), jnp.exp(s1 - m)], axis=1).astype(BF16)
            mskv = jnp.concatenate([msk, msk], axis=0)
            vm = jnp.where(mskv, vwin, jnp.ones_like(vwin))
            o_kv.append(_dot(pr, vm))
            e_kv.append(jnp.exp(sink - m))
        tiles = []
        for g in range(A_GROUP):
            gr = slice(g * A_PAIR, (g + 1) * A_PAIR)
            o0, o1 = o_kv[0][gr], o_kv[1][gr]
            e0, e1 = e_kv[0][gr], e_kv[1][gr]
            num = jnp.where(lo_p, o0, o1)
            den = pltpu.roll(jnp.where(lo_p, o1, o0), HEAD_DIM, 1) + jnp.where(lo_p, e0, e1)
            tiles.append(num / den)
        ya = jnp.concatenate(tiles, axis=1)
        y_ref[0, rows, :QA_W] = _rms(ya, ga_ref[...]).astype(BF16)


def _attn_call(proj3, bias_a, sink_a, bias_b, ga, gb):
    b, s, _ = proj3.shape
    nq = s // TQ
    ka_col = (QA_W + 3 * QB_W) // KA_W
    va_col = ka_col + 1

    def blk(col):
        return pl.BlockSpec((1, TQ, QB_W), lambda bi, j: (bi, j, col))

    def prev_blk(col, back):
        return pl.BlockSpec((1, TQ, QB_W), lambda bi, j: (bi, jnp.maximum(j - back, 0), col))

    in_specs = [
        blk(0), blk(1),
        prev_blk(2, 2), prev_blk(2, 1), blk(2),
        prev_blk(3, 2), prev_blk(3, 1), blk(3),
        pl.BlockSpec((1, A_PAIR, KA_W), lambda bi, j: (bi, jnp.maximum(2 * j - 1, 0), ka_col)),
        pl.BlockSpec((1, TQ, KA_W), lambda bi, j: (bi, j, ka_col)),
        pl.BlockSpec((1, A_PAIR, KA_W), lambda bi, j: (bi, jnp.maximum(2 * j - 1, 0), va_col)),
        pl.BlockSpec((1, TQ, KA_W), lambda bi, j: (bi, j, va_col)),
        _const_spec(bias_a.shape), _const_spec(sink_a.shape), _const_spec(bias_b.shape),
        _const_spec(ga.shape), _const_spec(gb.shape),
    ]
    return pl.pallas_call(
        _attn_kernel,
        grid=(b, nq),
        in_specs=in_specs,
        out_specs=pl.BlockSpec((1, TQ, QA_W + QB_W), lambda bi, j: (bi, j, 0)),
        out_shape=jax.ShapeDtypeStruct((b, s, QA_W + QB_W), BF16),
        scratch_shapes=[pltpu.VMEM((TQ, QB_W), F32)],
        compiler_params=pltpu.CompilerParams(
            dimension_semantics=("arbitrary", "arbitrary"), vmem_limit_bytes=VMEM_LIMIT),
        name="attention",
    )(proj3, proj3, proj3, proj3, proj3, proj3, proj3, proj3, proj3, proj3, proj3, proj3,
      bias_a, sink_a, bias_b, ga, gb)


def _ffn_kernel(y_ref, x_ref, wo_ref, g2_ref, w1_ref, w2_ref, gf_ref, o_ref, h_scr, n2_scr):
    h = x_ref[...] + _dot(y_ref[...], wo_ref[...])
    h_scr[...] = h
    n2_scr[...] = _rms(h, g2_ref[...]).astype(BF16)

    def body(c, carry):
        u = _dot(n2_scr[...], w1_ref[c])
        u = jnp.square(jnp.maximum(u, 0.0)).astype(BF16)
        h_scr[...] += _dot(u, w2_ref[c])
        return carry

    lax.fori_loop(0, D_FF // FF_CHUNK, body, 0)
    o_ref[...] = _rms(h_scr[...], gf_ref[...])


def _ffn_call(y2, x2, wo, g2, w1, w2, gf):
    t = x2.shape[0]
    return pl.pallas_call(
        _ffn_kernel,
        grid=(t // TM_FFN,),
        in_specs=[
            pl.BlockSpec((TM_FFN, D_MODEL), lambda i: (i, 0)),
            pl.BlockSpec((TM_FFN, D_MODEL), lambda i: (i, 0)),
            _const_spec(wo.shape), _const_spec(g2.shape),
            _const_spec(w1.shape), _const_spec(w2.shape), _const_spec(gf.shape),
        ],
        out_specs=pl.BlockSpec((TM_FFN, D_MODEL), lambda i: (i, 0)),
        out_shape=jax.ShapeDtypeStruct((t, D_MODEL), F32),
        scratch_shapes=[pltpu.VMEM((TM_FFN, D_MODEL), F32), pltpu.VMEM((TM_FFN, D_MODEL), BF16)],
        compiler_params=pltpu.CompilerParams(
            dimension_semantics=("arbitrary",), vmem_limit_bytes=VMEM_LIMIT),
        name="out_proj_mlp",
    )(y2, x2, wo, g2, w1, w2, gf)


def _perm_a(t, axis):
    shape = t.shape
    t = t.reshape(shape[:axis] + (A_KV_HEADS, A_GROUP, HEAD_DIM) + shape[axis + 1:])
    t = jnp.swapaxes(t, axis, axis + 1)
    return t.reshape(shape)


def _bias_a_table(sinks):
    i = np.arange(A_PAIR)[:, None]
    k = np.arange(A_WIN)[None, :]
    dist = np.abs(A_PAIR + i - k).astype(np.float32)
    qc = i // CHUNK
    kc = k // CHUNK
    allowed = (kc >= qc) & (kc <= qc + A_BAND_CHUNKS - 1)
    slopes = jnp.exp2(-8.0 * (jnp.arange(A_HEADS, dtype=F32) + 1.0) / A_HEADS)
    bias = -slopes[:, None, None] * jnp.asarray(dist)[None]
    bias = jnp.where(jnp.asarray(allowed)[None], bias, NEG_INF)
    bias = bias.reshape(A_KV_HEADS, A_GROUP * A_PAIR, A_WIN)
    sink = jnp.broadcast_to(sinks.astype(F32).reshape(A_KV_HEADS, A_GROUP, 1, 1),
                            (A_KV_HEADS, A_GROUP, A_PAIR, 1)).reshape(A_KV_HEADS, A_GROUP * A_PAIR, 1)
    return bias, sink


def _bias_b_table(rel_bias):
    period = TQ + B_WIN
    d = np.arange(period)
    d = np.where(d >= B_WIN, d - period, d)
    dist = (B_BAND_CHUNKS - 1) * CHUNK - d
    rel = np.clip(dist, -B_MAX_REL, B_MAX_REL) + B_MAX_REL
    row = rel_bias.astype(F32)[:, jnp.asarray(rel)]
    flat = jnp.tile(row, (1, TQ))[:, :TQ * (period - 1)]
    bias = flat.reshape(B_HEADS, TQ, period - 1)[:, :, :B_WIN]
    q = np.arange(TQ)[:, None]
    k = np.arange(B_WIN)[None, :]
    qc = q // CHUNK
    kc = k // CHUNK
    allowed = (kc >= qc) & (kc <= qc + B_BAND_CHUNKS - 1)
    return jnp.where(jnp.asarray(allowed)[None], bias, NEG_INF)


def kernel(x, norm1_g, w_in, sinks_a, rel_bias_b, out_norm_a_g, out_norm_b_g, w_out, norm2_g,
           w_ff1, w_ff2, final_norm_g):
    b, s, d = x.shape
    assert d == D_MODEL and s % TQ == 0 and (b * s) % TM_PROJ == 0 and (b * s) % TM_FFN == 0
    assert norm1_g.shape[0] == 1, "single-layer block"
    scale = HEAD_DIM ** -0.5

    w = w_in[0]
    o_ka, o_va, o_qb, o_kb, o_vb = QA_W, QA_W + KA_W, QA_W + 2 * KA_W, QA_W + 2 * KA_W + QB_W, QA_W + 2 * KA_W + 2 * QB_W
    w_perm = jnp.concatenate([
        _perm_a(w[:, :QA_W], 1) * scale,
        w[:, o_qb:o_kb] * scale,
        w[:, o_kb:o_vb],
        w[:, o_vb:],
        w[:, o_ka:o_va],
        w[:, o_va:o_qb],
    ], axis=1).astype(BF16)

    x2 = x.reshape(b * s, d)
    proj = _proj_call(x2, norm1_g[0].reshape(1, d), w_perm)

    bias_a, sink_a = _bias_a_table(sinks_a[0])
    bias_b = _bias_b_table(rel_bias_b[0])
    ga = _perm_a(out_norm_a_g[0], 0).reshape(1, QA_W).astype(F32)
    gb = out_norm_b_g[0].reshape(1, QB_W).astype(F32)
    y = _attn_call(proj.reshape(b, s, PROJ_W), bias_a, sink_a, bias_b, ga, gb)

    wo = jnp.concatenate([_perm_a(w_out[0][:QA_W], 0), w_out[0][QA_W:]], axis=0).astype(BF16)
    nc = D_FF // FF_CHUNK
    w1 = w_ff1[0].reshape(d, nc, FF_CHUNK).transpose(1, 0, 2).astype(BF16)
    w2 = w_ff2[0].reshape(nc, FF_CHUNK, d).astype(BF16)
    out = _ffn_call(y.reshape(b * s, d), x2, wo, norm2_g[0].reshape(1, d), w1, w2,
                    final_norm_g.reshape(1, d))
    return out.reshape(b, s, d)
```

```python
import functools

import jax
import jax.numpy as jnp
import numpy as np
from jax import lax
from jax.experimental import pallas as pl
from jax.experimental.pallas import tpu as pltpu

D_MODEL = 1024
CHUNK = 64
HEAD_DIM = 64
A_HEADS = 8
A_KV_HEADS = 2
A_GROUP = A_HEADS // A_KV_HEADS
A_BAND_CHUNKS = 3
B_HEADS = 8
B_BAND_CHUNKS = 9
B_MAX_REL = 128
D_FF = 4 * D_MODEL
EPS = 1e-6
NEG_INF = -1e30

QA_W = A_HEADS * HEAD_DIM
KA_W = A_KV_HEADS * HEAD_DIM
QB_W = B_HEADS * HEAD_DIM
PROJ_W = QA_W + 2 * KA_W + 3 * QB_W

LANES = 128
TQ = 256
TQ_CHUNKS = TQ // CHUNK
A_PAIR = 2 * CHUNK
A_WIN = 4 * CHUNK
B_WIN = 3 * TQ
TM_PROJ = 512
TM_FFN = 512
FF_CHUNK = 512
VMEM_LIMIT = 56 * 1024 * 1024

F32 = jnp.float32
BF16 = jnp.bfloat16


def _rms(x, g):
    ms = jnp.mean(x * x, axis=-1, keepdims=True)
    return x * lax.rsqrt(ms + EPS) * g


def _dot(a, b):
    return jnp.dot(a, b, preferred_element_type=F32)


def _dot_nt(a, b):
    return lax.dot_general(a, b, (((1,), (1,)), ((), ())), preferred_element_type=F32)


def _proj_kernel(x_ref, g_ref, w_ref, o_ref):
    n = _rms(x_ref[...], g_ref[...]).astype(BF16)
    o_ref[...] = _dot(n, w_ref[...]).astype(BF16)


def _const_spec(shape):
    nd = len(shape)
    return pl.BlockSpec(shape, lambda *_: (0,) * nd, pipeline_mode=pl.Buffered(1))


def _proj_call(x2, g, w):
    t = x2.shape[0]
    return pl.pallas_call(
        _proj_kernel,
        grid=(t // TM_PROJ,),
        in_specs=[
            pl.BlockSpec((TM_PROJ, D_MODEL), lambda i: (i, 0)),
            _const_spec((1, D_MODEL)),
            _const_spec((D_MODEL, PROJ_W)),
        ],
        out_specs=pl.BlockSpec((TM_PROJ, PROJ_W), lambda i: (i, 0)),
        out_shape=jax.ShapeDtypeStruct((t, PROJ_W), BF16),
        compiler_params=pltpu.CompilerParams(
            dimension_semantics=("arbitrary",), vmem_limit_bytes=VMEM_LIMIT),
        name="norm_in_proj",
    )(x2, g, w)


def _attn_kernel(qa_ref, qb_ref, kb0_ref, kb1_ref, kb2_ref, vb0_ref, vb1_ref, vb2_ref,
                 kap_ref, kac_ref, vap_ref, vac_ref,
                 bias_a_ref, sink_ref, bias_b_ref, ga_ref, gb_ref,
                 y_ref, yb_scr):
    j = pl.program_id(1)
    lo_q = lax.broadcasted_iota(jnp.int32, (TQ, LANES), 1) < HEAD_DIM
    lo_p = lax.broadcasted_iota(jnp.int32, (A_PAIR, LANES), 1) < HEAD_DIM

    def lane_half(mask, half):
        return mask if half == 0 else jnp.logical_not(mask)

    kb_refs = (kb0_ref, kb1_ref, kb2_ref)
    vb_refs = (vb0_ref, vb1_ref, vb2_ref)
    pen_b = [jnp.where(j - 2 + p >= 0, 0.0, NEG_INF).astype(F32) for p in range(2)]
    for hp in range(B_HEADS // 2):
        cols = slice(hp * LANES, (hp + 1) * LANES)
        qpair = qb_ref[0, :, cols]
        kps = [r[0, :, cols] for r in kb_refs]
        vps = [r[0, :, cols] for r in vb_refs]
        outs = []
        for half in range(2):
            h = 2 * hp + half
            msk = lane_half(lo_q, half)
            qm = jnp.where(msk, qpair, jnp.zeros_like(qpair))
            s = []
            for p in range(3):
                sp = _dot_nt(qm, kps[p]) + bias_b_ref[h, :, p * TQ:(p + 1) * TQ]
                if p < 2:
                    sp = sp + pen_b[p]
                s.append(sp)
            m = jnp.max(jnp.maximum(jnp.maximum(s[0], s[1]), s[2]), axis=-1, keepdims=True)
            o = None
            for p in range(3):
                pr = jnp.exp(s[p] - m).astype(BF16)
                vm = jnp.where(msk, vps[p], jnp.ones_like(vps[p]))
                op = _dot(pr, vm)
                o = op if o is None else o + op
            outs.append(o)
        num = jnp.where(lo_q, outs[0], outs[1])
        den = pltpu.roll(jnp.where(lo_q, outs[1], outs[0]), HEAD_DIM, 1)
        yb_scr[:, cols] = num / den
    yb = yb_scr[...]
    y_ref[0, :, QA_W:] = _rms(yb, gb_ref[...]).astype(BF16)

    pen_a = jnp.where(j >= 1, 0.0, NEG_INF).astype(F32)
    for r in range(TQ // A_PAIR):
        rows = slice(r * A_PAIR, (r + 1) * A_PAIR)
        if r == 0:
            kwin = jnp.concatenate([kap_ref[0], kac_ref[0, :A_PAIR, :]], axis=0)
            vwin = jnp.concatenate([vap_ref[0], vac_ref[0, :A_PAIR, :]], axis=0)
        else:
            kwin = kac_ref[0]
            vwin = vac_ref[0]
        o_kv = []
        e_kv = []
        for kvh in range(A_KV_HEADS):
            msk = lane_half(lo_p, kvh)
            qst = jnp.concatenate(
                [jnp.where(msk, qa_ref[0, rows, g * LANES:(g + 1) * LANES],
                           jnp.zeros((A_PAIR, LANES), BF16)) for g in range(A_GROUP)], axis=0)
            s = _dot_nt(qst, kwin) + bias_a_ref[kvh]
            s0 = s[:, :A_PAIR]
            s1 = s[:, A_PAIR:]
            if r == 0:
                s0 = s0 + pen_a
            sink = sink_ref[kvh]
            row_max = jnp.max(jnp.maximum(s0, s1), axis=-1, keepdims=True)
            m = jnp.maximum(jnp.broadcast_to(row_max, sink.shape), sink)
            pr = jnp.concatenate([jnp.exp(s0 - m), jnp.exp(s1 - m)], axis=1).astype(BF16)
            mskv = jnp.concatenate([msk, msk], axis=0)
            vm = jnp.where(mskv, vwin, jnp.ones_like(vwin))
            o_kv.append(_dot(pr, vm))
            e_kv.append(jnp.exp(sink - m))
        tiles = []
        for g in range(A_GROUP):
            gr = slice(g * A_PAIR, (g + 1) * A_PAIR)
            o0, o1 = o_kv[0][gr], o_kv[1][gr]
            e0, e1 = e_kv[0][gr], e_kv[1][gr]
            num = jnp.where(lo_p, o0, o1)
            den = pltpu.roll(jnp.where(lo_p, o1, o0), HEAD_DIM, 1) + jnp.where(lo_p, e0, e1)
            tiles.append(num / den)
        ya = jnp.concatenate(tiles, axis=1)
        y_ref[0, rows, :QA_W] = _rms(ya, ga_ref[...]).astype(BF16)


def _attn_call(proj3, bias_a, sink_a, bias_b, ga, gb):
    b, s, _ = proj3.shape
    nq = s // TQ
    ka_col = (QA_W + 3 * QB_W) // KA_W
    va_col = ka_col + 1

    def blk(col):
        return pl.BlockSpec((1, TQ, QB_W), lambda bi, j: (bi, j, col))

    def prev_blk(col, back):
        return pl.BlockSpec((1, TQ, QB_W), lambda bi, j: (bi, jnp.maximum(j - back, 0), col))

    in_specs = [
        blk(0), blk(1),
        prev_blk(2, 2), prev_blk(2, 1), blk(2),
        prev_blk(3, 2), prev_blk(3, 1), blk(3),
        pl.BlockSpec((1, A_PAIR, KA_W), lambda bi, j: (bi, jnp.maximum(2 * j - 1, 0), ka_col)),
        pl.BlockSpec((1, TQ, KA_W), lambda bi, j: (bi, j, ka_col)),
        pl.BlockSpec((1, A_PAIR, KA_W), lambda bi, j: (bi, jnp.maximum(2 * j - 1, 0), va_col)),
        pl.BlockSpec((1, TQ, KA_W), lambda bi, j: (bi, j, va_col)),
        _const_spec(bias_a.shape), _const_spec(sink_a.shape), _const_spec(bias_b.shape),
        _const_spec(ga.shape), _const_spec(gb.shape),
    ]
    return pl.pallas_call(
        _attn_kernel,
        grid=(b, nq),
        in_specs=in_specs,
        out_specs=pl.BlockSpec((1, TQ, QA_W + QB_W), lambda bi, j: (bi, j, 0)),
        out_shape=jax.ShapeDtypeStruct((b, s, QA_W + QB_W), BF16),
        scratch_shapes=[pltpu.VMEM((TQ, QB_W), F32)],
        compiler_params=pltpu.CompilerParams(
            dimension_semantics=("arbitrary", "arbitrary"), vmem_limit_bytes=VMEM_LIMIT),
        name="attention",
    )(proj3, proj3, proj3, proj3, proj3, proj3, proj3, proj3, proj3, proj3, proj3, proj3,
      bias_a, sink_a, bias_b, ga, gb)


def _ffn_kernel(y_ref, x_ref, wo_ref, g2_ref, w1_ref, w2_ref, gf_ref, o_ref, h_scr, n2_scr):
    h = x_ref[...] + _dot(y_ref[...], wo_ref[...])
    h_scr[...] = h
    n2_scr[...] = _rms(h, g2_ref[...]).astype(BF16)

    def body(c, carry):
        u = _dot(n2_scr[...], w1_ref[c])
        u = jnp.square(jnp.maximum(u, 0.0)).astype(BF16)
        h_scr[...] += _dot(u, w2_ref[c])
        return carry

    lax.fori_loop(0, D_FF // FF_CHUNK, body, 0)
    o_ref[...] = _rms(h_scr[...], gf_ref[...])


def _ffn_call(y2, x2, wo, g2, w1, w2, gf):
    t = x2.shape[0]
    return pl.pallas_call(
        _ffn_kernel,
        grid=(t // TM_FFN,),
        in_specs=[
            pl.BlockSpec((TM_FFN, D_MODEL), lambda i: (i, 0)),
            pl.BlockSpec((TM_FFN, D_MODEL), lambda i: (i, 0)),
            _const_spec(wo.shape), _const_spec(g2.shape),
            _const_spec(w1.shape), _const_spec(w2.shape), _const_spec(gf.shape),
        ],
        out_specs=pl.BlockSpec((TM_FFN, D_MODEL), lambda i: (i, 0)),
        out_shape=jax.ShapeDtypeStruct((t, D_MODEL), F32),
        scratch_shapes=[pltpu.VMEM((TM_FFN, D_MODEL), F32), pltpu.VMEM((TM_FFN, D_MODEL), BF16)],
        compiler_params=pltpu.CompilerParams(
            dimension_semantics=("arbitrary",), vmem_limit_bytes=VMEM_LIMIT),
        name="out_proj_mlp",
    )(y2, x2, wo, g2, w1, w2, gf)


def _perm_a(t, axis):
    shape = t.shape
    t = t.reshape(shape[:axis] + (A_KV_HEADS, A_GROUP, HEAD_DIM) + shape[axis + 1:])
    t = jnp.swapaxes(t, axis, axis + 1)
    return t.reshape(shape)


def _bias_a_table(sinks):
    i = np.arange(A_PAIR)[:, None]
    k = np.arange(A_WIN)[None, :]
    dist = np.abs(A_PAIR + i - k).astype(np.float32)
    qc = i // CHUNK
    kc = k // CHUNK
    allowed = (kc >= qc) & (kc <= qc + A_BAND_CHUNKS - 1)
    slopes = jnp.exp2(-8.0 * (jnp.arange(A_HEADS, dtype=F32) + 1.0) / A_HEADS)
    bias = -slopes[:, None, None] * jnp.asarray(dist)[None]
    bias = jnp.where(jnp.asarray(allowed)[None], bias, NEG_INF)
    bias = bias.reshape(A_KV_HEADS, A_GROUP * A_PAIR, A_WIN)
    sink = jnp.broadcast_to(sinks.astype(F32).reshape(A_KV_HEADS, A_GROUP, 1, 1),
                            (A_KV_HEADS, A_GROUP, A_PAIR, LANES)).reshape(A_KV_HEADS, A_GROUP * A_PAIR, LANES)
    return bias, sink


def _bias_b_table(rel_bias):
    period = TQ + B_WIN
    d = np.arange(period)
    d = np.where(d >= B_WIN, d - period, d)
    dist = (B_BAND_CHUNKS - 1) * CHUNK - d
    rel = np.clip(dist, -B_MAX_REL, B_MAX_REL) + B_MAX_REL
    row = rel_bias.astype(F32)[:, jnp.asarray(rel)]
    flat = jnp.tile(row, (1, TQ))[:, :TQ * (period - 1)]
    bias = flat.reshape(B_HEADS, TQ, period - 1)[:, :, :B_WIN]
    q = np.arange(TQ)[:, None]
    k = np.arange(B_WIN)[None, :]
    qc = q // CHUNK
    kc = k // CHUNK
    allowed = (kc >= qc) & (kc <= qc + B_BAND_CHUNKS - 1)
    return jnp.where(jnp.asarray(allowed)[None], bias, NEG_INF)


def kernel(x, norm1_g, w_in, sinks_a, rel_bias_b, out_norm_a_g, out_norm_b_g, w_out, norm2_g,
           w_ff1, w_ff2, final_norm_g):
    b, s, d = x.shape
    assert d == D_MODEL and s % TQ == 0 and (b * s) % TM_PROJ == 0 and (b * s) % TM_FFN == 0
    assert norm1_g.shape[0] == 1, "single-layer block"
    scale = HEAD_DIM ** -0.5

    w = w_in[0]
    o_ka, o_va, o_qb, o_kb, o_vb = QA_W, QA_W + KA_W, QA_W + 2 * KA_W, QA_W + 2 * KA_W + QB_W, QA_W + 2 * KA_W + 2 * QB_W
    w_perm = jnp.concatenate([
        _perm_a(w[:, :QA_W], 1) * scale,
        w[:, o_qb:o_kb] * scale,
        w[:, o_kb:o_vb],
        w[:, o_vb:],
        w[:, o_ka:o_va],
        w[:, o_va:o_qb],
    ], axis=1).astype(BF16)

    x2 = x.reshape(b * s, d)
    proj = _proj_call(x2, norm1_g[0].reshape(1, d), w_perm)

    bias_a, sink_a = _bias_a_table(sinks_a[0])
    bias_b = _bias_b_table(rel_bias_b[0])
    ga = _perm_a(out_norm_a_g[0], 0).reshape(1, QA_W).astype(F32)
    gb = out_norm_b_g[0].reshape(1, QB_W).astype(F32)
    y = _attn_call(proj.reshape(b, s, PROJ_W), bias_a, sink_a, bias_b, ga, gb)

    wo = jnp.concatenate([_perm_a(w_out[0][:QA_W], 0), w_out[0][QA_W:]], axis=0).astype(BF16)
    nc = D_FF // FF_CHUNK
    w1 = w_ff1[0].reshape(d, nc, FF_CHUNK).transpose(1, 0, 2).astype(BF16)
    w2 = w_ff2[0].reshape(nc, FF_CHUNK, d).astype(BF16)
    out = _ffn_call(y.reshape(b * s, d), x2, wo, norm2_g[0].reshape(1, d), w1, w2,
                    final_norm_g.reshape(1, d))
    return out.reshape(b, s, d)
```

```python
import jax
import jax.numpy as jnp
import numpy as np
from jax import lax
from jax.experimental import pallas as pl
from jax.experimental.pallas import tpu as pltpu

D_MODEL = 1024
CHUNK = 64
HEAD_DIM = 64
A_HEADS = 8
A_KV_HEADS = 2
A_GROUP = A_HEADS // A_KV_HEADS
A_BAND_CHUNKS = 3
B_HEADS = 8
B_BAND_CHUNKS = 9
B_MAX_REL = 128
D_FF = 4 * D_MODEL
EPS = 1e-6
NEG_INF = -1e30

QA_W = A_HEADS * HEAD_DIM
KA_W = A_KV_HEADS * HEAD_DIM
QB_W = B_HEADS * HEAD_DIM
MIX_W = QA_W + QB_W
K_W = QB_W + KA_W
T_W = QA_W + 2 * QB_W + KA_W

LANES = 128
TQ = 256
A_PAIR = 2 * CHUNK
A_WIN = 4 * CHUNK
B_BLOCKS = 3
B_WIN = B_BLOCKS * TQ
TM_PROJ = 512
TM_FFN = 512
FF_CHUNK = 512
VMEM_LIMIT = 56 * 1024 * 1024

F32 = jnp.float32
BF16 = jnp.bfloat16


def _rms(x, g):
    ms = jnp.mean(x * x, axis=-1, keepdims=True)
    return x * lax.rsqrt(ms + EPS) * g


def _rms_rows(xt, g):
    ms = jnp.mean(xt * xt, axis=0, keepdims=True)
    return xt * lax.rsqrt(ms + EPS) * g


def _dot(a, b):
    return jnp.dot(a, b, preferred_element_type=F32)


def _dot_nt(a, b):
    return lax.dot_general(a, b, (((1,), (1,)), ((), ())), preferred_element_type=F32)


def _dot_tn(a, b):
    return lax.dot_general(a, b, (((0,), (0,)), ((), ())), preferred_element_type=F32)


def _const_spec(shape):
    nd = len(shape)
    return pl.BlockSpec(shape, lambda *_: (0,) * nd, pipeline_mode=pl.Buffered(1))


def _proj_kernel(x_ref, g_ref, wk_ref, wt_ref, kb_ref, ka_ref, qat_ref, qbt_ref, vbt_ref, vat_ref):
    n = _rms(x_ref[0], g_ref[...]).astype(BF16)
    k = _dot(n, wk_ref[...]).astype(BF16)
    kb_ref[0] = k[:, :QB_W]
    ka_ref[0] = k[:, QB_W:]
    t = _dot_nt(wt_ref[...], n).astype(BF16)
    qat_ref[0] = t[:QA_W]
    qbt_ref[0] = t[QA_W:QA_W + QB_W]
    vbt_ref[0] = t[QA_W + QB_W:QA_W + 2 * QB_W]
    vat_ref[0] = t[QA_W + 2 * QB_W:]


def _proj_call(x, g, wk, wt):
    b, s, _ = x.shape

    def tok(width):
        return pl.BlockSpec((1, TM_PROJ, width), lambda bi, i: (bi, i, 0))

    def feat(width):
        return pl.BlockSpec((1, width, TM_PROJ), lambda bi, i: (bi, 0, i))

    return pl.pallas_call(
        _proj_kernel,
        grid=(b, s // TM_PROJ),
        in_specs=[tok(D_MODEL), _const_spec(g.shape), _const_spec(wk.shape), _const_spec(wt.shape)],
        out_specs=[tok(QB_W), tok(KA_W), feat(QA_W), feat(QB_W), feat(QB_W), feat(KA_W)],
        out_shape=[
            jax.ShapeDtypeStruct((b, s, QB_W), BF16), jax.ShapeDtypeStruct((b, s, KA_W), BF16),
            jax.ShapeDtypeStruct((b, QA_W, s), BF16), jax.ShapeDtypeStruct((b, QB_W, s), BF16),
            jax.ShapeDtypeStruct((b, QB_W, s), BF16), jax.ShapeDtypeStruct((b, KA_W, s), BF16),
        ],
        compiler_params=pltpu.CompilerParams(
            dimension_semantics=("arbitrary", "arbitrary"), vmem_limit_bytes=VMEM_LIMIT),
        name="norm_in_proj",
    )(x, g, wk, wt)


def _attn_kernel(qat_ref, qbt_ref, kb0_ref, kb1_ref, kb2_ref, vbt0_ref, vbt1_ref, vbt2_ref,
                 kap_ref, kac_ref, vatp_ref, vatc_ref,
                 bias_a_ref, sink_ref, bias_b_ref, ga_ref, gb_ref,
                 y_ref, ya_scr, yb_scr):
    j = pl.program_id(1)
    zeros_q = jnp.zeros((HEAD_DIM, TQ), BF16)
    ones_k = jnp.ones((HEAD_DIM, TQ), BF16)
    zeros_p = jnp.zeros((HEAD_DIM, A_PAIR), BF16)
    kb_refs = (kb0_ref, kb1_ref, kb2_ref)
    vbt_refs = (vbt0_ref, vbt1_ref, vbt2_ref)
    pen_a = jnp.where(j >= 1, 0.0, NEG_INF).astype(F32)


    def b_scores(h):
        hp, half = divmod(h, 2)
        qt = qbt_ref[0, h * HEAD_DIM:(h + 1) * HEAD_DIM, :]
        qm = jnp.concatenate([qt, zeros_q] if half == 0 else [zeros_q, qt], axis=0)
        st = [_dot(kb_refs[p][0, :, hp * LANES:(hp + 1) * LANES], qm)
              + bias_b_ref[0, h, p * TQ:(p + 1) * TQ, :] for p in range(B_BLOCKS)]
        m = jnp.max(jnp.maximum(jnp.maximum(st[0], st[1]), st[2]), axis=0, keepdims=True)
        return st, m

    def b_output(h, st, m):
        rows = slice(h * HEAD_DIM, (h + 1) * HEAD_DIM)
        ot = None
        for p in range(B_BLOCKS):
            pt = jnp.exp(st[p] - m).astype(BF16)
            vt = jnp.concatenate([vbt_refs[p][0, rows, :], ones_k], axis=0)
            op = _dot(vt, pt)
            ot = op if ot is None else ot + op
        yb_scr[rows, :] = ot[:HEAD_DIM] * (1.0 / ot[HEAD_DIM:HEAD_DIM + 1])

    def a_windows(r):
        if r == 0:
            kwin = jnp.concatenate([kap_ref[0], kac_ref[0, :A_PAIR, :]], axis=0)
            vwin = jnp.concatenate([vatp_ref[0], vatc_ref[0, :, :A_PAIR]], axis=1)
            return kwin, vwin
        return kac_ref[0], vatc_ref[0]

    def a_scores(r, kvh):
        tok = slice(r * A_PAIR, (r + 1) * A_PAIR)
        kwin, _ = a_windows(r)
        blocks = []
        for g in range(A_GROUP):
            h = kvh * A_GROUP + g
            qt = qat_ref[0, h * HEAD_DIM:(h + 1) * HEAD_DIM, tok]
            blocks.append(jnp.concatenate([qt, zeros_p] if kvh == 0 else [zeros_p, qt], axis=0))
        qst = jnp.concatenate(blocks, axis=1)
        st = _dot(kwin, qst) + bias_a_ref[kvh]
        s0 = st[:A_PAIR]
        s1 = st[A_PAIR:]
        if r == 0:
            s0 = s0 + pen_a
        m = jnp.maximum(jnp.max(jnp.maximum(s0, s1), axis=0, keepdims=True), sink_ref[kvh])
        return (s0, s1), m

    def a_output(r, kvh, st, m):
        tok = slice(r * A_PAIR, (r + 1) * A_PAIR)
        _, vwin = a_windows(r)
        pt = jnp.concatenate([jnp.exp(st[0] - m), jnp.exp(st[1] - m)], axis=0).astype(BF16)
        vt = jnp.concatenate([vwin[kvh * HEAD_DIM:(kvh + 1) * HEAD_DIM, :], ones_k], axis=0)
        ot = _dot(vt, pt)
        den = ot[HEAD_DIM:HEAD_DIM + 1] + jnp.exp(sink_ref[kvh] - m)
        yt = ot[:HEAD_DIM] * (1.0 / den)
        for g in range(A_GROUP):
            h = kvh * A_GROUP + g
            ya_scr[h * HEAD_DIM:(h + 1) * HEAD_DIM, tok] = yt[:, g * A_PAIR:(g + 1) * A_PAIR]

    units = [(b_scores, b_output, (h,)) for h in range(B_HEADS)]
    units += [(a_scores, a_output, (r, kvh)) for r in range(TQ // A_PAIR) for kvh in range(A_KV_HEADS)]
    pending = None
    for scores, output, args in units:
        staged = scores(*args)
        if pending is not None:
            pending[0](*pending[1], *pending[2])
        pending = (output, args, staged)
    pending[0](*pending[1], *pending[2])

    y_ref[0, QA_W:, :] = _rms_rows(yb_scr[...], gb_ref[...]).astype(BF16)
    y_ref[0, :QA_W, :] = _rms_rows(ya_scr[...], ga_ref[...]).astype(BF16)


def _attn_call(kb, ka, qat, qbt, vbt, vat, bias_a, sink_a, bias_b, ga, gb):
    b, s, _ = kb.shape
    nq = s // TQ

    def feat_cur(width):
        return pl.BlockSpec((1, width, TQ), lambda bi, j: (bi, 0, j))

    def tok_back(back):
        return pl.BlockSpec((1, TQ, QB_W), lambda bi, j: (bi, jnp.maximum(j - back, 0), 0))

    def feat_back(back):
        return pl.BlockSpec((1, QB_W, TQ), lambda bi, j: (bi, 0, jnp.maximum(j - back, 0)))

    in_specs = [
        feat_cur(QA_W), feat_cur(QB_W),
        tok_back(2), tok_back(1), tok_back(0),
        feat_back(2), feat_back(1), feat_back(0),
        pl.BlockSpec((1, A_PAIR, KA_W), lambda bi, j: (bi, jnp.maximum(2 * j - 1, 0), 0)),
        pl.BlockSpec((1, TQ, KA_W), lambda bi, j: (bi, j, 0)),
        pl.BlockSpec((1, KA_W, A_PAIR), lambda bi, j: (bi, 0, jnp.maximum(2 * j - 1, 0))),
        pl.BlockSpec((1, KA_W, TQ), lambda bi, j: (bi, 0, j)),
        _const_spec(bias_a.shape), _const_spec(sink_a.shape),
        pl.BlockSpec((1,) + bias_b.shape[1:], lambda bi, j: (jnp.minimum(j, B_BLOCKS - 1), 0, 0, 0)),
        _const_spec(ga.shape), _const_spec(gb.shape),
    ]
    return pl.pallas_call(
        _attn_kernel,
        grid=(b, nq),
        in_specs=in_specs,
        out_specs=pl.BlockSpec((1, MIX_W, TQ), lambda bi, j: (bi, 0, j)),
        out_shape=jax.ShapeDtypeStruct((b, MIX_W, s), BF16),
        scratch_shapes=[pltpu.VMEM((QA_W, TQ), F32), pltpu.VMEM((QB_W, TQ), F32)],
        compiler_params=pltpu.CompilerParams(
            dimension_semantics=("arbitrary", "arbitrary"), vmem_limit_bytes=VMEM_LIMIT),
        name="attention",
    )(qat, qbt, kb, kb, kb, vbt, vbt, vbt, ka, ka, vat, vat, bias_a, sink_a, bias_b, ga, gb)


def _ffn_kernel(yt_ref, x_ref, wo_ref, g2_ref, w1_ref, w2_ref, gf_ref, o_ref, h_scr, n2_scr):
    h = x_ref[0] + _dot_tn(yt_ref[0], wo_ref[...])
    h_scr[...] = h
    n2_scr[...] = _rms(h, g2_ref[...]).astype(BF16)

    def body(c, carry):
        u = _dot(n2_scr[...], w1_ref[c])
        u = jnp.square(jnp.maximum(u, 0.0)).astype(BF16)
        h_scr[...] += _dot(u, w2_ref[c])
        return carry

    lax.fori_loop(0, D_FF // FF_CHUNK, body, 0)
    o_ref[0] = _rms(h_scr[...], gf_ref[...])


def _ffn_call(yt, x, wo, g2, w1, w2, gf):
    b, s, _ = x.shape
    return pl.pallas_call(
        _ffn_kernel,
        grid=(b, s // TM_FFN),
        in_specs=[
            pl.BlockSpec((1, MIX_W, TM_FFN), lambda bi, i: (bi, 0, i)),
            pl.BlockSpec((1, TM_FFN, D_MODEL), lambda bi, i: (bi, i, 0)),
            _const_spec(wo.shape), _const_spec(g2.shape),
            _const_spec(w1.shape), _const_spec(w2.shape), _const_spec(gf.shape),
        ],
        out_specs=pl.BlockSpec((1, TM_FFN, D_MODEL), lambda bi, i: (bi, i, 0)),
        out_shape=jax.ShapeDtypeStruct((b, s, D_MODEL), F32),
        scratch_shapes=[pltpu.VMEM((TM_FFN, D_MODEL), F32), pltpu.VMEM((TM_FFN, D_MODEL), BF16)],
        compiler_params=pltpu.CompilerParams(
            dimension_semantics=("arbitrary", "arbitrary"), vmem_limit_bytes=VMEM_LIMIT),
        name="out_proj_mlp",
    )(yt, x, wo, g2, w1, w2, gf)


def _bias_a_table(sinks):
    k = np.arange(A_WIN)[:, None]
    i = np.arange(A_PAIR)[None, :]
    dist = np.abs(A_PAIR + i - k).astype(np.float32)
    qc = i // CHUNK
    kc = k // CHUNK
    allowed = (kc >= qc) & (kc <= qc + A_BAND_CHUNKS - 1)
    slopes = jnp.exp2(-8.0 * (jnp.arange(A_HEADS, dtype=F32) + 1.0) / A_HEADS)
    bias = -slopes[:, None, None] * jnp.asarray(dist)[None]
    bias = jnp.where(jnp.asarray(allowed)[None], bias, NEG_INF)
    bias = bias.reshape(A_KV_HEADS, A_GROUP, A_WIN, A_PAIR).transpose(0, 2, 1, 3)
    bias = bias.reshape(A_KV_HEADS, A_WIN, A_GROUP * A_PAIR)
    sink = jnp.broadcast_to(sinks.astype(F32).reshape(A_KV_HEADS, 1, A_GROUP, 1),
                            (A_KV_HEADS, 1, A_GROUP, A_PAIR)).reshape(A_KV_HEADS, 1, A_GROUP * A_PAIR)
    return bias, sink


def _bias_b_table(rel_bias):
    period = TQ + B_WIN
    d = np.arange(period)
    d = np.where(d >= TQ, d - period, d)
    dist = (B_BAND_CHUNKS - 1) * CHUNK + d
    rel = np.clip(dist, -B_MAX_REL, B_MAX_REL) + B_MAX_REL
    row = rel_bias.astype(F32)[:, jnp.asarray(rel)]
    flat = jnp.tile(row, (1, B_WIN))[:, :B_WIN * (period - 1)]
    bias = flat.reshape(B_HEADS, B_WIN, period - 1)[:, :, :TQ]
    k = np.arange(B_WIN)[:, None]
    q = np.arange(TQ)[None, :]
    qc = q // CHUNK
    kc = k // CHUNK
    allowed = (kc >= qc) & (kc <= qc + B_BAND_CHUNKS - 1)
    variants = []
    for v in range(B_BLOCKS):
        ok = allowed & (k // TQ >= B_BLOCKS - 1 - v)
        variants.append(jnp.where(jnp.asarray(ok)[None], bias, NEG_INF))
    return jnp.stack(variants)


def kernel(x, norm1_g, w_in, sinks_a, rel_bias_b, out_norm_a_g, out_norm_b_g, w_out, norm2_g,
           w_ff1, w_ff2, final_norm_g):
    b, s, d = x.shape
    assert d == D_MODEL and s % TM_PROJ == 0 and s % TM_FFN == 0 and s % TQ == 0
    assert norm1_g.shape[0] == 1, "single-layer block"
    scale = HEAD_DIM ** -0.5

    w = w_in[0]
    o_ka, o_va, o_qb, o_kb, o_vb = QA_W, QA_W + KA_W, QA_W + 2 * KA_W, QA_W + 2 * KA_W + QB_W, QA_W + 2 * KA_W + 2 * QB_W
    wk = jnp.concatenate([w[:, o_kb:o_vb], w[:, o_ka:o_va]], axis=1).astype(BF16)
    wt = jnp.concatenate([w[:, :QA_W] * scale, w[:, o_qb:o_kb] * scale, w[:, o_vb:], w[:, o_va:o_qb]],
                         axis=1).T.astype(BF16)
    kb, ka, qat, qbt, vbt, vat = _proj_call(x, norm1_g[0].reshape(1, d), wk, wt)

    bias_a, sink_a = _bias_a_table(sinks_a[0])
    bias_b = _bias_b_table(rel_bias_b[0])
    ga = jnp.broadcast_to(out_norm_a_g[0].astype(F32)[:, None], (QA_W, TQ))
    gb = jnp.broadcast_to(out_norm_b_g[0].astype(F32)[:, None], (QB_W, TQ))
    yt = _attn_call(kb, ka, qat, qbt, vbt, vat, bias_a, sink_a, bias_b, ga, gb)

    nc = D_FF // FF_CHUNK
    w1 = w_ff1[0].reshape(d, nc, FF_CHUNK).transpose(1, 0, 2).astype(BF16)
    w2 = w_ff2[0].reshape(nc, FF_CHUNK, d).astype(BF16)
    return _ffn_call(yt, x, w_out[0].astype(BF16), norm2_g[0].reshape(1, d), w1, w2,
                     final_norm_g.reshape(1, d))
```

```python
import jax
import jax.numpy as jnp
import numpy as np
from jax import lax
from jax.experimental import pallas as pl
from jax.experimental.pallas import tpu as pltpu

D_MODEL = 1024
CHUNK = 64
HEAD_DIM = 64
A_HEADS = 8
A_KV_HEADS = 2
A_GROUP = A_HEADS // A_KV_HEADS
A_BAND_CHUNKS = 3
B_HEADS = 8
B_BAND_CHUNKS = 9
B_MAX_REL = 128
D_FF = 4 * D_MODEL
EPS = 1e-6
NEG_INF = -1e30
LOG2E = 1.4426950408889634

QA_W = A_HEADS * HEAD_DIM
KA_W = A_KV_HEADS * HEAD_DIM
QB_W = B_HEADS * HEAD_DIM
MIX_W = QA_W + QB_W
K_W = QB_W + KA_W
T_W = QA_W + 2 * QB_W + KA_W

LANES = 128
TQ = 256
A_PAIR = 2 * CHUNK
A_WIN = 4 * CHUNK
B_BLOCKS = 3
B_WIN = B_BLOCKS * TQ
TM_PROJ = 512
TM_FFN = 512
FF_CHUNK = 1024
VMEM_LIMIT = 56 * 1024 * 1024

F32 = jnp.float32
BF16 = jnp.bfloat16


def _rms(x, g):
    ms = jnp.mean(x * x, axis=-1, keepdims=True)
    return x * lax.rsqrt(ms + EPS) * g


def _rms_rows(xt, g):
    ms = jnp.mean(xt * xt, axis=0, keepdims=True)
    return xt * lax.rsqrt(ms + EPS) * g


def _dot(a, b):
    return jnp.dot(a, b, preferred_element_type=F32)


def _dot_nt(a, b):
    return lax.dot_general(a, b, (((1,), (1,)), ((), ())), preferred_element_type=F32)


def _dot_tn(a, b):
    return lax.dot_general(a, b, (((0,), (0,)), ((), ())), preferred_element_type=F32)


def _const_spec(shape):
    nd = len(shape)
    return pl.BlockSpec(shape, lambda *_: (0,) * nd, pipeline_mode=pl.Buffered(1))


def _proj_kernel(x_ref, g_ref, wk_ref, wt_ref, kb_ref, ka_ref, qat_ref, qbt_ref, vbt_ref, vat_ref):
    n = _rms(x_ref[0], g_ref[...]).astype(BF16)
    k = _dot(n, wk_ref[...]).astype(BF16)
    kb_ref[0] = k[:, :QB_W]
    ka_ref[0] = k[:, QB_W:]
    t = _dot_nt(wt_ref[...], n).astype(BF16)
    qat_ref[0] = t[:QA_W]
    qbt_ref[0] = t[QA_W:QA_W + QB_W]
    vbt_ref[0] = t[QA_W + QB_W:QA_W + 2 * QB_W]
    vat_ref[0] = t[QA_W + 2 * QB_W:]


def _proj_call(x, g, wk, wt):
    b, s, _ = x.shape

    def tok(width):
        return pl.BlockSpec((1, TM_PROJ, width), lambda bi, i: (bi, i, 0))

    def feat(width):
        return pl.BlockSpec((1, width, TM_PROJ), lambda bi, i: (bi, 0, i))

    return pl.pallas_call(
        _proj_kernel,
        grid=(b, s // TM_PROJ),
        in_specs=[tok(D_MODEL), _const_spec(g.shape), _const_spec(wk.shape), _const_spec(wt.shape)],
        out_specs=[tok(QB_W), tok(KA_W), feat(QA_W), feat(QB_W), feat(QB_W), feat(KA_W)],
        out_shape=[
            jax.ShapeDtypeStruct((b, s, QB_W), BF16), jax.ShapeDtypeStruct((b, s, KA_W), BF16),
            jax.ShapeDtypeStruct((b, QA_W, s), BF16), jax.ShapeDtypeStruct((b, QB_W, s), BF16),
            jax.ShapeDtypeStruct((b, QB_W, s), BF16), jax.ShapeDtypeStruct((b, KA_W, s), BF16),
        ],
        compiler_params=pltpu.CompilerParams(
            dimension_semantics=("arbitrary", "arbitrary"), vmem_limit_bytes=VMEM_LIMIT),
        name="norm_in_proj",
    )(x, g, wk, wt)


def _attn_kernel(qat_ref, qbt_ref, kb0_ref, kb1_ref, kb2_ref, vbt0_ref, vbt1_ref, vbt2_ref,
                 kap_ref, kac_ref, vatp_ref, vatc_ref,
                 bias_a_ref, sink_ref, bias_b_ref, ga_ref, gb_ref,
                 y_ref, ya_scr, yb_scr):
    j = pl.program_id(1)
    zeros_q = jnp.zeros((HEAD_DIM, TQ), BF16)
    ones_k = jnp.ones((HEAD_DIM, TQ), BF16)
    zeros_p = jnp.zeros((HEAD_DIM, A_PAIR), BF16)
    kb_refs = (kb0_ref, kb1_ref, kb2_ref)
    vbt_refs = (vbt0_ref, vbt1_ref, vbt2_ref)
    pen_a = jnp.where(j >= 1, 0.0, NEG_INF).astype(F32)


    def b_scores(h):
        hp, half = divmod(h, 2)
        qt = qbt_ref[0, h * HEAD_DIM:(h + 1) * HEAD_DIM, :]
        qm = jnp.concatenate([qt, zeros_q] if half == 0 else [zeros_q, qt], axis=0)
        st = [_dot(kb_refs[p][0, :, hp * LANES:(hp + 1) * LANES], qm)
              + bias_b_ref[0, h, p * TQ:(p + 1) * TQ, :] for p in range(B_BLOCKS)]
        m = jnp.max(jnp.maximum(jnp.maximum(st[0], st[1]), st[2]), axis=0, keepdims=True)
        return st, m

    def b_output(h, st, m):
        rows = slice(h * HEAD_DIM, (h + 1) * HEAD_DIM)
        ot = None
        for p in range(B_BLOCKS):
            pt = jnp.exp2(st[p] - m).astype(BF16)
            vt = jnp.concatenate([vbt_refs[p][0, rows, :], ones_k], axis=0)
            op = _dot(vt, pt)
            ot = op if ot is None else ot + op
        yb_scr[rows, :] = ot[:HEAD_DIM] * (1.0 / ot[HEAD_DIM:HEAD_DIM + 1])

    def a_windows(r):
        if r == 0:
            kwin = jnp.concatenate([kap_ref[0], kac_ref[0, :A_PAIR, :]], axis=0)
            vwin = jnp.concatenate([vatp_ref[0], vatc_ref[0, :, :A_PAIR]], axis=1)
            return kwin, vwin
        return kac_ref[0], vatc_ref[0]

    def a_scores(r, kvh):
        tok = slice(r * A_PAIR, (r + 1) * A_PAIR)
        kwin, _ = a_windows(r)
        blocks = []
        for g in range(A_GROUP):
            h = kvh * A_GROUP + g
            qt = qat_ref[0, h * HEAD_DIM:(h + 1) * HEAD_DIM, tok]
            blocks.append(jnp.concatenate([qt, zeros_p] if kvh == 0 else [zeros_p, qt], axis=0))
        qst = jnp.concatenate(blocks, axis=1)
        st = _dot(kwin, qst) + bias_a_ref[kvh]
        s0 = st[:A_PAIR]
        s1 = st[A_PAIR:]
        if r == 0:
            s0 = s0 + pen_a
        m = jnp.maximum(jnp.max(jnp.maximum(s0, s1), axis=0, keepdims=True), sink_ref[kvh])
        return (s0, s1), m

    def a_output(r, kvh, st, m):
        tok = slice(r * A_PAIR, (r + 1) * A_PAIR)
        _, vwin = a_windows(r)
        pt = jnp.concatenate([jnp.exp2(st[0] - m), jnp.exp2(st[1] - m)], axis=0).astype(BF16)
        vt = jnp.concatenate([vwin[kvh * HEAD_DIM:(kvh + 1) * HEAD_DIM, :], ones_k], axis=0)
        ot = _dot(vt, pt)
        den = ot[HEAD_DIM:HEAD_DIM + 1] + jnp.exp2(sink_ref[kvh] - m)
        yt = ot[:HEAD_DIM] * (1.0 / den)
        for g in range(A_GROUP):
            h = kvh * A_GROUP + g
            ya_scr[h * HEAD_DIM:(h + 1) * HEAD_DIM, tok] = yt[:, g * A_PAIR:(g + 1) * A_PAIR]

    units = [(b_scores, b_output, (h,)) for h in range(B_HEADS)]
    units += [(a_scores, a_output, (r, kvh)) for r in range(TQ // A_PAIR) for kvh in range(A_KV_HEADS)]
    pending = None
    for scores, output, args in units:
        staged = scores(*args)
        if pending is not None:
            pending[0](*pending[1], *pending[2])
        pending = (output, args, staged)
    pending[0](*pending[1], *pending[2])

    y_ref[0, QA_W:, :] = _rms_rows(yb_scr[...], gb_ref[...]).astype(BF16)
    y_ref[0, :QA_W, :] = _rms_rows(ya_scr[...], ga_ref[...]).astype(BF16)


def _attn_call(kb, ka, qat, qbt, vbt, vat, bias_a, sink_a, bias_b, ga, gb):
    b, s, _ = kb.shape
    nq = s // TQ

    def feat_cur(width):
        return pl.BlockSpec((1, width, TQ), lambda bi, j: (bi, 0, j))

    def tok_back(back):
        return pl.BlockSpec((1, TQ, QB_W), lambda bi, j: (bi, jnp.maximum(j - back, 0), 0))

    def feat_back(back):
        return pl.BlockSpec((1, QB_W, TQ), lambda bi, j: (bi, 0, jnp.maximum(j - back, 0)))

    in_specs = [
        feat_cur(QA_W), feat_cur(QB_W),
        tok_back(2), tok_back(1), tok_back(0),
        feat_back(2), feat_back(1), feat_back(0),
        pl.BlockSpec((1, A_PAIR, KA_W), lambda bi, j: (bi, jnp.maximum(2 * j - 1, 0), 0)),
        pl.BlockSpec((1, TQ, KA_W), lambda bi, j: (bi, j, 0)),
        pl.BlockSpec((1, KA_W, A_PAIR), lambda bi, j: (bi, 0, jnp.maximum(2 * j - 1, 0))),
        pl.BlockSpec((1, KA_W, TQ), lambda bi, j: (bi, 0, j)),
        _const_spec(bias_a.shape), _const_spec(sink_a.shape),
        pl.BlockSpec((1,) + bias_b.shape[1:], lambda bi, j: (jnp.minimum(j, B_BLOCKS - 1), 0, 0, 0)),
        _const_spec(ga.shape), _const_spec(gb.shape),
    ]
    return pl.pallas_call(
        _attn_kernel,
        grid=(b, nq),
        in_specs=in_specs,
        out_specs=pl.BlockSpec((1, MIX_W, TQ), lambda bi, j: (bi, 0, j)),
        out_shape=jax.ShapeDtypeStruct((b, MIX_W, s), BF16),
        scratch_shapes=[pltpu.VMEM((QA_W, TQ), F32), pltpu.VMEM((QB_W, TQ), F32)],
        compiler_params=pltpu.CompilerParams(
            dimension_semantics=("arbitrary", "arbitrary"), vmem_limit_bytes=VMEM_LIMIT),
        name="attention",
    )(qat, qbt, kb, kb, kb, vbt, vbt, vbt, ka, ka, vat, vat, bias_a, sink_a, bias_b, ga, gb)


def _ffn_kernel(yt_ref, x_ref, wo_ref, g2_ref, w1_ref, w2_ref, gf_ref, o_ref, h_scr, n2_scr):
    h = x_ref[0] + _dot_tn(yt_ref[0], wo_ref[...])
    h_scr[...] = h
    n2_scr[...] = _rms(h, g2_ref[...]).astype(BF16)

    def body(c, carry):
        u = _dot(n2_scr[...], w1_ref[c])
        u = jnp.square(jnp.maximum(u, 0.0)).astype(BF16)
        h_scr[...] += _dot(u, w2_ref[c])
        return carry

    lax.fori_loop(0, D_FF // FF_CHUNK, body, 0, unroll=True)
    o_ref[0] = _rms(h_scr[...], gf_ref[...])


def _ffn_call(yt, x, wo, g2, w1, w2, gf):
    b, s, _ = x.shape
    return pl.pallas_call(
        _ffn_kernel,
        grid=(b, s // TM_FFN),
        in_specs=[
            pl.BlockSpec((1, MIX_W, TM_FFN), lambda bi, i: (bi, 0, i)),
            pl.BlockSpec((1, TM_FFN, D_MODEL), lambda bi, i: (bi, i, 0)),
            _const_spec(wo.shape), _const_spec(g2.shape),
            _const_spec(w1.shape), _const_spec(w2.shape), _const_spec(gf.shape),
        ],
        out_specs=pl.BlockSpec((1, TM_FFN, D_MODEL), lambda bi, i: (bi, i, 0)),
        out_shape=jax.ShapeDtypeStruct((b, s, D_MODEL), F32),
        scratch_shapes=[pltpu.VMEM((TM_FFN, D_MODEL), F32), pltpu.VMEM((TM_FFN, D_MODEL), BF16)],
        compiler_params=pltpu.CompilerParams(
            dimension_semantics=("arbitrary", "arbitrary"), vmem_limit_bytes=VMEM_LIMIT),
        name="out_proj_mlp",
    )(yt, x, wo, g2, w1, w2, gf)


def _bias_a_table(sinks):
    k = np.arange(A_WIN)[:, None]
    i = np.arange(A_PAIR)[None, :]
    dist = np.abs(A_PAIR + i - k).astype(np.float32)
    qc = i // CHUNK
    kc = k // CHUNK
    allowed = (kc >= qc) & (kc <= qc + A_BAND_CHUNKS - 1)
    slopes = jnp.exp2(-8.0 * (jnp.arange(A_HEADS, dtype=F32) + 1.0) / A_HEADS)
    bias = -slopes[:, None, None] * jnp.asarray(dist)[None] * LOG2E
    bias = jnp.where(jnp.asarray(allowed)[None], bias, NEG_INF)
    bias = bias.reshape(A_KV_HEADS, A_GROUP, A_WIN, A_PAIR).transpose(0, 2, 1, 3)
    bias = bias.reshape(A_KV_HEADS, A_WIN, A_GROUP * A_PAIR)
    sink = jnp.broadcast_to((sinks.astype(F32) * LOG2E).reshape(A_KV_HEADS, 1, A_GROUP, 1),
                            (A_KV_HEADS, 1, A_GROUP, A_PAIR)).reshape(A_KV_HEADS, 1, A_GROUP * A_PAIR)
    return bias, sink


def _bias_b_table(rel_bias):
    period = TQ + B_WIN
    d = np.arange(period)
    d = np.where(d >= B_WIN, d - period, d)
    dist = (B_BAND_CHUNKS - 1) * CHUNK - d
    rel = np.clip(dist, -B_MAX_REL, B_MAX_REL) + B_MAX_REL
    row = (rel_bias.astype(F32) * LOG2E)[:, jnp.asarray(rel)]
    flat = jnp.tile(row, (1, TQ))[:, :TQ * (period - 1)]
    bias = flat.reshape(B_HEADS, TQ, period - 1)[:, :, :B_WIN]
    bias = jnp.swapaxes(bias, 1, 2)
    k = np.arange(B_WIN)[:, None]
    q = np.arange(TQ)[None, :]
    qc = q // CHUNK
    kc = k // CHUNK
    allowed = (kc >= qc) & (kc <= qc + B_BAND_CHUNKS - 1)
    variants = []
    for v in range(B_BLOCKS):
        ok = allowed & (k // TQ >= B_BLOCKS - 1 - v)
        variants.append(jnp.where(jnp.asarray(ok)[None], bias, NEG_INF))
    return jnp.stack(variants)


def kernel(x, norm1_g, w_in, sinks_a, rel_bias_b, out_norm_a_g, out_norm_b_g, w_out, norm2_g,
           w_ff1, w_ff2, final_norm_g):
    b, s, d = x.shape
    assert d == D_MODEL and s % TM_PROJ == 0 and s % TM_FFN == 0 and s % TQ == 0
    assert norm1_g.shape[0] == 1, "single-layer block"
    scale = HEAD_DIM ** -0.5 * LOG2E

    w = w_in[0]
    o_ka, o_va, o_qb, o_kb, o_vb = QA_W, QA_W + KA_W, QA_W + 2 * KA_W, QA_W + 2 * KA_W + QB_W, QA_W + 2 * KA_W + 2 * QB_W
    wk = jnp.concatenate([w[:, o_kb:o_vb], w[:, o_ka:o_va]], axis=1).astype(BF16)
    wt = jnp.concatenate([w[:, :QA_W] * scale, w[:, o_qb:o_kb] * scale, w[:, o_vb:], w[:, o_va:o_qb]],
                         axis=1).T.astype(BF16)
    kb, ka, qat, qbt, vbt, vat = _proj_call(x, norm1_g[0].reshape(1, d), wk, wt)

    bias_a, sink_a = _bias_a_table(sinks_a[0])
    bias_b = _bias_b_table(rel_bias_b[0])
    ga = jnp.broadcast_to(out_norm_a_g[0].astype(F32)[:, None], (QA_W, TQ))
    gb = jnp.broadcast_to(out_norm_b_g[0].astype(F32)[:, None], (QB_W, TQ))
    yt = _attn_call(kb, ka, qat, qbt, vbt, vat, bias_a, sink_a, bias_b, ga, gb)

    nc = D_FF // FF_CHUNK
    w1 = w_ff1[0].reshape(d, nc, FF_CHUNK).transpose(1, 0, 2).astype(BF16)
    w2 = w_ff2[0].reshape(nc, FF_CHUNK, d).astype(BF16)
    return _ffn_call(yt, x, w_out[0].astype(BF16), norm2_g[0].reshape(1, d), w1, w2,
                     final_norm_g.reshape(1, d))
```

```python
import functools

import jax
import jax.numpy as jnp
import numpy as np
from jax import lax
from jax.experimental import pallas as pl
from jax.experimental.pallas import tpu as pltpu

D_MODEL = 1024
CHUNK = 64
HEAD_DIM = 64
A_HEADS = 8
A_KV_HEADS = 2
A_GROUP = A_HEADS // A_KV_HEADS
A_BAND_CHUNKS = 3
B_HEADS = 8
B_BAND_CHUNKS = 9
B_MAX_REL = 128
D_FF = 4 * D_MODEL
EPS = 1e-6
NEG_INF = -1e30
LOG2E = 1.4426950408889634

QA_W = A_HEADS * HEAD_DIM
KA_W = A_KV_HEADS * HEAD_DIM
QB_W = B_HEADS * HEAD_DIM
MIX_W = QA_W + QB_W
K_W = QB_W + KA_W
T_W = QA_W + 2 * QB_W + KA_W

LANES = 128
TQ = 256
A_PAIR = 2 * CHUNK
A_WIN = 4 * CHUNK
B_BLOCKS = 3
B_WIN = B_BLOCKS * TQ
PIPE_DEPTH = 5
TM_PROJ = 512
TM_FFN = 512
FF_CHUNK = 1024
VMEM_LIMIT = 56 * 1024 * 1024

F32 = jnp.float32
BF16 = jnp.bfloat16


def _rms(x, g):
    ms = jnp.mean(x * x, axis=-1, keepdims=True)
    return x * lax.rsqrt(ms + EPS) * g


def _rms_rows(xt, g):
    ms = jnp.mean(xt * xt, axis=0, keepdims=True)
    return xt * lax.rsqrt(ms + EPS) * g


def _dot(a, b):
    return jnp.dot(a, b, preferred_element_type=F32)


def _dot_nt(a, b):
    return lax.dot_general(a, b, (((1,), (1,)), ((), ())), preferred_element_type=F32)


def _dot_tn(a, b):
    return lax.dot_general(a, b, (((0,), (0,)), ((), ())), preferred_element_type=F32)


def _const_spec(shape):
    nd = len(shape)
    return pl.BlockSpec(shape, lambda *_: (0,) * nd, pipeline_mode=pl.Buffered(1))


def _proj_kernel(x_ref, g_ref, wk_ref, wt_ref, kb_ref, ka_ref, qat_ref, qbt_ref, vbt_ref, vat_ref):
    n = _rms(x_ref[0], g_ref[...]).astype(BF16)
    k = _dot(n, wk_ref[...]).astype(BF16)
    kb_ref[0] = k[:, :QB_W]
    ka_ref[0] = k[:, QB_W:]
    t = _dot_nt(wt_ref[...], n).astype(BF16)
    qat_ref[0] = t[:QA_W]
    qbt_ref[0] = t[QA_W:QA_W + QB_W]
    vbt_ref[0] = t[QA_W + QB_W:QA_W + 2 * QB_W]
    vat_ref[0] = t[QA_W + 2 * QB_W:]


def _proj_call(x, g, wk, wt):
    b, s, _ = x.shape

    def tok(width):
        return pl.BlockSpec((1, TM_PROJ, width), lambda bi, i: (bi, i, 0))

    def feat(width):
        return pl.BlockSpec((1, width, TM_PROJ), lambda bi, i: (bi, 0, i))

    return pl.pallas_call(
        _proj_kernel,
        grid=(b, s // TM_PROJ),
        in_specs=[tok(D_MODEL), _const_spec(g.shape), _const_spec(wk.shape), _const_spec(wt.shape)],
        out_specs=[tok(QB_W), tok(KA_W), feat(QA_W), feat(QB_W), feat(QB_W), feat(KA_W)],
        out_shape=[
            jax.ShapeDtypeStruct((b, s, QB_W), BF16), jax.ShapeDtypeStruct((b, s, KA_W), BF16),
            jax.ShapeDtypeStruct((b, QA_W, s), BF16), jax.ShapeDtypeStruct((b, QB_W, s), BF16),
            jax.ShapeDtypeStruct((b, QB_W, s), BF16), jax.ShapeDtypeStruct((b, KA_W, s), BF16),
        ],
        compiler_params=pltpu.CompilerParams(
            dimension_semantics=("arbitrary", "arbitrary"), vmem_limit_bytes=VMEM_LIMIT),
        name="norm_in_proj",
    )(x, g, wk, wt)


def _attn_kernel(qat_ref, qbt_ref, kb0_ref, kb1_ref, kb2_ref, vbt0_ref, vbt1_ref, vbt2_ref,
                 kap_ref, kac_ref, vatp_ref, vatc_ref,
                 bias_a_ref, sink_ref, bias_b_ref, ga_ref, gb_ref,
                 y_ref, ya_scr, yb_scr):
    j = pl.program_id(1)
    zeros_q = jnp.zeros((HEAD_DIM, TQ), BF16)
    ones_k = jnp.ones((HEAD_DIM, TQ), BF16)
    zeros_p = jnp.zeros((HEAD_DIM, A_PAIR), BF16)
    kb_refs = (kb0_ref, kb1_ref, kb2_ref)
    vbt_refs = (vbt0_ref, vbt1_ref, vbt2_ref)
    pen_a = jnp.where(j >= 1, 0.0, NEG_INF).astype(F32)


    b_parts = {}

    def b_scores(h, p):
        hp, half = divmod(h, 2)
        qt = qbt_ref[0, h * HEAD_DIM:(h + 1) * HEAD_DIM, :]
        qm = jnp.concatenate([qt, zeros_q] if half == 0 else [zeros_q, qt], axis=0)
        st = _dot(kb_refs[p][0, :, hp * LANES:(hp + 1) * LANES], qm) + bias_b_ref[0, h, p * TQ:(p + 1) * TQ, :]
        return st, jnp.max(st, axis=0, keepdims=True)

    def b_output(h, p, st, mp):
        rows = slice(h * HEAD_DIM, (h + 1) * HEAD_DIM)
        pt = jnp.exp2(st - mp).astype(BF16)
        vt = jnp.concatenate([vbt_refs[p][0, rows, :], ones_k], axis=0)
        b_parts.setdefault(h, []).append((_dot(vt, pt)[:HEAD_DIM + 8], mp))
        if p == B_BLOCKS - 1:
            parts = b_parts.pop(h)
            m = functools.reduce(jnp.maximum, [mq for _, mq in parts])
            ot = sum(op * jnp.exp2(mq - m) for op, mq in parts)
            yb_scr[rows, :] = ot[:HEAD_DIM] * (1.0 / ot[HEAD_DIM:HEAD_DIM + 1])

    def a_windows(r):
        if r == 0:
            kwin = jnp.concatenate([kap_ref[0], kac_ref[0, :A_PAIR, :]], axis=0)
            vwin = jnp.concatenate([vatp_ref[0], vatc_ref[0, :, :A_PAIR]], axis=1)
            return kwin, vwin
        return kac_ref[0], vatc_ref[0]

    def a_scores(r, kvh):
        tok = slice(r * A_PAIR, (r + 1) * A_PAIR)
        kwin, _ = a_windows(r)
        blocks = []
        for g in range(A_GROUP):
            h = kvh * A_GROUP + g
            qt = qat_ref[0, h * HEAD_DIM:(h + 1) * HEAD_DIM, tok]
            blocks.append(jnp.concatenate([qt, zeros_p] if kvh == 0 else [zeros_p, qt], axis=0))
        qst = jnp.concatenate(blocks, axis=1)
        st = _dot(kwin, qst) + bias_a_ref[kvh]
        s0 = st[:A_PAIR]
        s1 = st[A_PAIR:]
        if r == 0:
            s0 = s0 + pen_a
        m = jnp.maximum(jnp.max(jnp.maximum(s0, s1), axis=0, keepdims=True), sink_ref[kvh])
        return (s0, s1), m

    def a_output(r, kvh, st, m):
        tok = slice(r * A_PAIR, (r + 1) * A_PAIR)
        _, vwin = a_windows(r)
        pt = jnp.concatenate([jnp.exp2(st[0] - m), jnp.exp2(st[1] - m)], axis=0).astype(BF16)
        vt = jnp.concatenate([vwin[kvh * HEAD_DIM:(kvh + 1) * HEAD_DIM, :], ones_k], axis=0)
        ot = _dot(vt, pt)
        den = ot[HEAD_DIM:HEAD_DIM + 1] + jnp.exp2(sink_ref[kvh] - m)
        yt = ot[:HEAD_DIM] * (1.0 / den)
        for g in range(A_GROUP):
            h = kvh * A_GROUP + g
            ya_scr[h * HEAD_DIM:(h + 1) * HEAD_DIM, tok] = yt[:, g * A_PAIR:(g + 1) * A_PAIR]

    units = [(b_scores, b_output, (h, p)) for h in range(B_HEADS) for p in range(B_BLOCKS)]
    units += [(a_scores, a_output, (r, kvh)) for r in range(TQ // A_PAIR) for kvh in range(A_KV_HEADS)]
    pending = []
    for scores, output, args in units:
        pending.append((output, args, scores(*args)))
        if len(pending) > PIPE_DEPTH:
            output, args, staged = pending.pop(0)
            output(*args, *staged)
    for output, args, staged in pending:
        output(*args, *staged)

    y_ref[0, QA_W:, :] = _rms_rows(yb_scr[...], gb_ref[...]).astype(BF16)
    y_ref[0, :QA_W, :] = _rms_rows(ya_scr[...], ga_ref[...]).astype(BF16)


def _attn_call(kb, ka, qat, qbt, vbt, vat, bias_a, sink_a, bias_b, ga, gb):
    b, s, _ = kb.shape
    nq = s // TQ

    def feat_cur(width):
        return pl.BlockSpec((1, width, TQ), lambda bi, j: (bi, 0, j))

    def tok_back(back):
        return pl.BlockSpec((1, TQ, QB_W), lambda bi, j: (bi, jnp.maximum(j - back, 0), 0))

    def feat_back(back):
        return pl.BlockSpec((1, QB_W, TQ), lambda bi, j: (bi, 0, jnp.maximum(j - back, 0)))

    in_specs = [
        feat_cur(QA_W), feat_cur(QB_W),
        tok_back(2), tok_back(1), tok_back(0),
        feat_back(2), feat_back(1), feat_back(0),
        pl.BlockSpec((1, A_PAIR, KA_W), lambda bi, j: (bi, jnp.maximum(2 * j - 1, 0), 0)),
        pl.BlockSpec((1, TQ, KA_W), lambda bi, j: (bi, j, 0)),
        pl.BlockSpec((1, KA_W, A_PAIR), lambda bi, j: (bi, 0, jnp.maximum(2 * j - 1, 0))),
        pl.BlockSpec((1, KA_W, TQ), lambda bi, j: (bi, 0, j)),
        _const_spec(bias_a.shape), _const_spec(sink_a.shape),
        pl.BlockSpec((1,) + bias_b.shape[1:], lambda bi, j: (jnp.minimum(j, B_BLOCKS - 1), 0, 0, 0)),
        _const_spec(ga.shape), _const_spec(gb.shape),
    ]
    return pl.pallas_call(
        _attn_kernel,
        grid=(b, nq),
        in_specs=in_specs,
        out_specs=pl.BlockSpec((1, MIX_W, TQ), lambda bi, j: (bi, 0, j)),
        out_shape=jax.ShapeDtypeStruct((b, MIX_W, s), BF16),
        scratch_shapes=[pltpu.VMEM((QA_W, TQ), F32), pltpu.VMEM((QB_W, TQ), F32)],
        compiler_params=pltpu.CompilerParams(
            dimension_semantics=("arbitrary", "arbitrary"), vmem_limit_bytes=VMEM_LIMIT),
        name="attention",
    )(qat, qbt, kb, kb, kb, vbt, vbt, vbt, ka, ka, vat, vat, bias_a, sink_a, bias_b, ga, gb)


def _ffn_kernel(yt_ref, x_ref, wo_ref, g2_ref, w1_ref, w2_ref, gf_ref, o_ref, h_scr, n2_scr):
    h = x_ref[0] + _dot_tn(yt_ref[0], wo_ref[...])
    h_scr[...] = h
    n2_scr[...] = _rms(h, g2_ref[...]).astype(BF16)

    for c in range(D_FF // FF_CHUNK):
        cols = slice(c * FF_CHUNK, (c + 1) * FF_CHUNK)
        u = _dot(n2_scr[...], w1_ref[:, cols])
        u = jnp.square(jnp.maximum(u, 0.0)).astype(BF16)
        h_scr[...] += _dot(u, w2_ref[cols, :])
    o_ref[0] = _rms(h_scr[...], gf_ref[...])


def _ffn_call(yt, x, wo, g2, w1, w2, gf):
    b, s, _ = x.shape
    return pl.pallas_call(
        _ffn_kernel,
        grid=(b, s // TM_FFN),
        in_specs=[
            pl.BlockSpec((1, MIX_W, TM_FFN), lambda bi, i: (bi, 0, i)),
            pl.BlockSpec((1, TM_FFN, D_MODEL), lambda bi, i: (bi, i, 0)),
            _const_spec(wo.shape), _const_spec(g2.shape),
            _const_spec(w1.shape), _const_spec(w2.shape), _const_spec(gf.shape),
        ],
        out_specs=pl.BlockSpec((1, TM_FFN, D_MODEL), lambda bi, i: (bi, i, 0)),
        out_shape=jax.ShapeDtypeStruct((b, s, D_MODEL), F32),
        scratch_shapes=[pltpu.VMEM((TM_FFN, D_MODEL), F32), pltpu.VMEM((TM_FFN, D_MODEL), BF16)],
        compiler_params=pltpu.CompilerParams(
            dimension_semantics=("arbitrary", "arbitrary"), vmem_limit_bytes=VMEM_LIMIT),
        name="out_proj_mlp",
    )(yt, x, wo, g2, w1, w2, gf)


def _bias_a_table(sinks):
    k = np.arange(A_WIN)[:, None]
    i = np.arange(A_PAIR)[None, :]
    dist = np.abs(A_PAIR + i - k).astype(np.float32)
    qc = i // CHUNK
    kc = k // CHUNK
    allowed = (kc >= qc) & (kc <= qc + A_BAND_CHUNKS - 1)
    slopes = jnp.exp2(-8.0 * (jnp.arange(A_HEADS, dtype=F32) + 1.0) / A_HEADS)
    bias = -slopes[:, None, None] * jnp.asarray(dist)[None] * LOG2E
    bias = jnp.where(jnp.asarray(allowed)[None], bias, NEG_INF)
    bias = bias.reshape(A_KV_HEADS, A_GROUP, A_WIN, A_PAIR).transpose(0, 2, 1, 3)
    bias = bias.reshape(A_KV_HEADS, A_WIN, A_GROUP * A_PAIR)
    sink = jnp.broadcast_to((sinks.astype(F32) * LOG2E).reshape(A_KV_HEADS, 1, A_GROUP, 1),
                            (A_KV_HEADS, 1, A_GROUP, A_PAIR)).reshape(A_KV_HEADS, 1, A_GROUP * A_PAIR)
    return bias, sink


def _bias_b_table(rel_bias):
    period = TQ + B_WIN
    d = np.arange(period)
    d = np.where(d >= B_WIN, d - period, d)
    dist = (B_BAND_CHUNKS - 1) * CHUNK - d
    rel = np.clip(dist, -B_MAX_REL, B_MAX_REL) + B_MAX_REL
    row = (rel_bias.astype(F32) * LOG2E)[:, jnp.asarray(rel)]
    flat = jnp.tile(row, (1, TQ))[:, :TQ * (period - 1)]
    bias = flat.reshape(B_HEADS, TQ, period - 1)[:, :, :B_WIN]
    bias = jnp.swapaxes(bias, 1, 2)
    k = np.arange(B_WIN)[:, None]
    q = np.arange(TQ)[None, :]
    qc = q // CHUNK
    kc = k // CHUNK
    allowed = (kc >= qc) & (kc <= qc + B_BAND_CHUNKS - 1)
    variants = []
    for v in range(B_BLOCKS):
        ok = allowed & (k // TQ >= B_BLOCKS - 1 - v)
        variants.append(jnp.where(jnp.asarray(ok)[None], bias, NEG_INF))
    return jnp.stack(variants)


def kernel(x, norm1_g, w_in, sinks_a, rel_bias_b, out_norm_a_g, out_norm_b_g, w_out, norm2_g,
           w_ff1, w_ff2, final_norm_g):
    b, s, d = x.shape
    assert d == D_MODEL and s % TM_PROJ == 0 and s % TM_FFN == 0 and s % TQ == 0
    assert norm1_g.shape[0] == 1, "single-layer block"
    scale = HEAD_DIM ** -0.5 * LOG2E

    w = w_in[0]
    o_ka, o_va, o_qb, o_kb, o_vb = QA_W, QA_W + KA_W, QA_W + 2 * KA_W, QA_W + 2 * KA_W + QB_W, QA_W + 2 * KA_W + 2 * QB_W
    wk = jnp.concatenate([w[:, o_kb:o_vb], w[:, o_ka:o_va]], axis=1).astype(BF16)
    wt = jnp.concatenate([w[:, :QA_W] * scale, w[:, o_qb:o_kb] * scale, w[:, o_vb:], w[:, o_va:o_qb]],
                         axis=1).T.astype(BF16)
    kb, ka, qat, qbt, vbt, vat = _proj_call(x, norm1_g[0].reshape(1, d), wk, wt)

    bias_a, sink_a = _bias_a_table(sinks_a[0])
    bias_b = _bias_b_table(rel_bias_b[0])
    ga = jnp.broadcast_to(out_norm_a_g[0].astype(F32)[:, None], (QA_W, TQ))
    gb = jnp.broadcast_to(out_norm_b_g[0].astype(F32)[:, None], (QB_W, TQ))
    yt = _attn_call(kb, ka, qat, qbt, vbt, vat, bias_a, sink_a, bias_b, ga, gb)

    return _ffn_call(yt, x, w_out[0].astype(BF16), norm2_g[0].reshape(1, d),
                     w_ff1[0].astype(BF16), w_ff2[0].astype(BF16), final_norm_g.reshape(1, d))
```

```python
import functools

import jax
import jax.numpy as jnp
import numpy as np
from jax import lax
from jax.experimental import pallas as pl
from jax.experimental.pallas import tpu as pltpu

D_MODEL = 1024
CHUNK = 64
HEAD_DIM = 64
A_HEADS = 8
A_KV_HEADS = 2
A_GROUP = A_HEADS // A_KV_HEADS
A_BAND_CHUNKS = 3
B_HEADS = 8
B_BAND_CHUNKS = 9
B_MAX_REL = 128
D_FF = 4 * D_MODEL
EPS = 1e-6
NEG_INF = -1e30
LOG2E = 1.4426950408889634

QA_W = A_HEADS * HEAD_DIM
KA_W = A_KV_HEADS * HEAD_DIM
QB_W = B_HEADS * HEAD_DIM
MIX_W = QA_W + QB_W
K_W = QB_W + KA_W
T_W = QA_W + 2 * QB_W + KA_W

LANES = 128
TQ = 256
A_PAIR = 2 * CHUNK
A_WIN = 4 * CHUNK
B_BLOCKS = 3
B_WIN = B_BLOCKS * TQ
PIPE_DEPTH = 5
TM_PROJ = 1024
PROJ_SUB = 512
TM_FFN = 1024
FF_CHUNK = 1024
VMEM_LIMIT = 56 * 1024 * 1024

F32 = jnp.float32
BF16 = jnp.bfloat16


def _rms(x, g):
    ms = jnp.mean(x * x, axis=-1, keepdims=True)
    return x * lax.rsqrt(ms + EPS) * g


def _rms_rows(xt, g):
    ms = jnp.mean(xt * xt, axis=0, keepdims=True)
    return xt * lax.rsqrt(ms + EPS) * g


def _dot(a, b):
    return jnp.dot(a, b, preferred_element_type=F32)


def _dot_nt(a, b):
    return lax.dot_general(a, b, (((1,), (1,)), ((), ())), preferred_element_type=F32)


def _dot_tn(a, b):
    return lax.dot_general(a, b, (((0,), (0,)), ((), ())), preferred_element_type=F32)


def _const_spec(shape):
    nd = len(shape)
    return pl.BlockSpec(shape, lambda *_: (0,) * nd, pipeline_mode=pl.Buffered(1))


def _proj_kernel(x_ref, g_ref, wk_ref, wt_ref, kb_ref, ka_ref, qat_ref, qbt_ref, vbt_ref, vat_ref):
    for i in range(TM_PROJ // PROJ_SUB):
        tok = slice(i * PROJ_SUB, (i + 1) * PROJ_SUB)
        n = _rms(x_ref[0, tok, :], g_ref[...]).astype(BF16)
        k = _dot(n, wk_ref[...]).astype(BF16)
        kb_ref[0, tok, :] = k[:, :QB_W]
        ka_ref[0, tok, :] = k[:, QB_W:]
        t = _dot_nt(wt_ref[...], n).astype(BF16)
        qat_ref[0, :, tok] = t[:QA_W]
        qbt_ref[0, :, tok] = t[QA_W:QA_W + QB_W]
        vbt_ref[0, :, tok] = t[QA_W + QB_W:QA_W + 2 * QB_W]
        vat_ref[0, :, tok] = t[QA_W + 2 * QB_W:]


def _proj_call(x, g, wk, wt):
    b, s, _ = x.shape

    def tok(width):
        return pl.BlockSpec((1, TM_PROJ, width), lambda bi, i: (bi, i, 0))

    def feat(width):
        return pl.BlockSpec((1, width, TM_PROJ), lambda bi, i: (bi, 0, i))

    return pl.pallas_call(
        _proj_kernel,
        grid=(b, s // TM_PROJ),
        in_specs=[tok(D_MODEL), _const_spec(g.shape), _const_spec(wk.shape), _const_spec(wt.shape)],
        out_specs=[tok(QB_W), tok(KA_W), feat(QA_W), feat(QB_W), feat(QB_W), feat(KA_W)],
        out_shape=[
            jax.ShapeDtypeStruct((b, s, QB_W), BF16), jax.ShapeDtypeStruct((b, s, KA_W), BF16),
            jax.ShapeDtypeStruct((b, QA_W, s), BF16), jax.ShapeDtypeStruct((b, QB_W, s), BF16),
            jax.ShapeDtypeStruct((b, QB_W, s), BF16), jax.ShapeDtypeStruct((b, KA_W, s), BF16),
        ],
        compiler_params=pltpu.CompilerParams(
            dimension_semantics=("arbitrary", "arbitrary"), vmem_limit_bytes=VMEM_LIMIT),
        name="norm_in_proj",
    )(x, g, wk, wt)


def _attn_kernel(qat_ref, qbt_ref, kb_ref, vbt_ref, ka_ref, vat_ref,
                 bias_a_ref, sink_ref, bias_b_ref, ga_ref, gb_ref,
                 y_ref, ya_scr, yb_scr):
    j = pl.program_id(1)
    zeros_q = jnp.zeros((HEAD_DIM, TQ), BF16)
    ones_k = jnp.ones((HEAD_DIM, TQ), BF16)
    zeros_p = jnp.zeros((HEAD_DIM, A_PAIR), BF16)
    b_start = [pl.multiple_of(jnp.maximum(j - (B_BLOCKS - 1 - p), 0) * TQ, TQ) for p in range(B_BLOCKS)]
    a_prev = pl.multiple_of(jnp.maximum(2 * j - 1, 0) * A_PAIR, A_PAIR)
    a_cur = pl.multiple_of(j * TQ, TQ)
    pen_a = jnp.where(j >= 1, 0.0, NEG_INF).astype(F32)


    b_parts = {}

    def b_scores(h, p):
        hp, half = divmod(h, 2)
        qt = qbt_ref[0, h * HEAD_DIM:(h + 1) * HEAD_DIM, :]
        qm = jnp.concatenate([qt, zeros_q] if half == 0 else [zeros_q, qt], axis=0)
        kp = kb_ref[0, pl.ds(b_start[p], TQ), hp * LANES:(hp + 1) * LANES]
        st = _dot(kp, qm) + bias_b_ref[0, h, p * TQ:(p + 1) * TQ, :]
        return st, jnp.max(st, axis=0, keepdims=True)

    def b_output(h, p, st, mp):
        rows = slice(h * HEAD_DIM, (h + 1) * HEAD_DIM)
        pt = jnp.exp2(st - mp).astype(BF16)
        vt = jnp.concatenate([vbt_ref[0, rows, pl.ds(b_start[p], TQ)], ones_k], axis=0)
        b_parts.setdefault(h, []).append((_dot(vt, pt)[:HEAD_DIM + 8], mp))
        if p == B_BLOCKS - 1:
            parts = b_parts.pop(h)
            m = functools.reduce(jnp.maximum, [mq for _, mq in parts])
            ot = sum(op * jnp.exp2(mq - m) for op, mq in parts)
            yb_scr[rows, :] = ot[:HEAD_DIM] * (1.0 / ot[HEAD_DIM:HEAD_DIM + 1])

    def a_windows(r):
        if r == 0:
            kwin = jnp.concatenate([ka_ref[0, pl.ds(a_prev, A_PAIR), :],
                                    ka_ref[0, pl.ds(a_cur, A_PAIR), :]], axis=0)
            vwin = jnp.concatenate([vat_ref[0, :, pl.ds(a_prev, A_PAIR)],
                                    vat_ref[0, :, pl.ds(a_cur, A_PAIR)]], axis=1)
            return kwin, vwin
        return ka_ref[0, pl.ds(a_cur, TQ), :], vat_ref[0, :, pl.ds(a_cur, TQ)]

    def a_scores(r, kvh):
        tok = slice(r * A_PAIR, (r + 1) * A_PAIR)
        kwin, _ = a_windows(r)
        blocks = []
        for g in range(A_GROUP):
            h = kvh * A_GROUP + g
            qt = qat_ref[0, h * HEAD_DIM:(h + 1) * HEAD_DIM, tok]
            blocks.append(jnp.concatenate([qt, zeros_p] if kvh == 0 else [zeros_p, qt], axis=0))
        qst = jnp.concatenate(blocks, axis=1)
        st = _dot(kwin, qst) + bias_a_ref[kvh]
        s0 = st[:A_PAIR]
        s1 = st[A_PAIR:]
        if r == 0:
            s0 = s0 + pen_a
        m = jnp.maximum(jnp.max(jnp.maximum(s0, s1), axis=0, keepdims=True), sink_ref[kvh])
        return (s0, s1), m

    def a_output(r, kvh, st, m):
        tok = slice(r * A_PAIR, (r + 1) * A_PAIR)
        _, vwin = a_windows(r)
        pt = jnp.concatenate([jnp.exp2(st[0] - m), jnp.exp2(st[1] - m)], axis=0).astype(BF16)
        vt = jnp.concatenate([vwin[kvh * HEAD_DIM:(kvh + 1) * HEAD_DIM, :], ones_k], axis=0)
        ot = _dot(vt, pt)
        den = ot[HEAD_DIM:HEAD_DIM + 1] + jnp.exp2(sink_ref[kvh] - m)
        yt = ot[:HEAD_DIM] * (1.0 / den)
        for g in range(A_GROUP):
            h = kvh * A_GROUP + g
            ya_scr[h * HEAD_DIM:(h + 1) * HEAD_DIM, tok] = yt[:, g * A_PAIR:(g + 1) * A_PAIR]

    units = [(b_scores, b_output, (h, p)) for h in range(B_HEADS) for p in range(B_BLOCKS)]
    units += [(a_scores, a_output, (r, kvh)) for r in range(TQ // A_PAIR) for kvh in range(A_KV_HEADS)]
    pending = []
    for scores, output, args in units:
        pending.append((output, args, scores(*args)))
        if len(pending) > PIPE_DEPTH:
            output, args, staged = pending.pop(0)
            output(*args, *staged)
    for output, args, staged in pending:
        output(*args, *staged)

    y_ref[0, QA_W:, :] = _rms_rows(yb_scr[...], gb_ref[...]).astype(BF16)
    y_ref[0, :QA_W, :] = _rms_rows(ya_scr[...], ga_ref[...]).astype(BF16)


def _attn_call(kb, ka, qat, qbt, vbt, vat, bias_a, sink_a, bias_b, ga, gb):
    b, s, _ = kb.shape
    nq = s // TQ

    def feat_cur(width):
        return pl.BlockSpec((1, width, TQ), lambda bi, j: (bi, 0, j))

    def whole_seq(shape):
        return pl.BlockSpec((1,) + shape[1:], lambda bi, j: (bi, 0, 0))

    in_specs = [
        feat_cur(QA_W), feat_cur(QB_W),
        whole_seq(kb.shape), whole_seq(vbt.shape), whole_seq(ka.shape), whole_seq(vat.shape),
        _const_spec(bias_a.shape), _const_spec(sink_a.shape),
        pl.BlockSpec((1,) + bias_b.shape[1:], lambda bi, j: (jnp.minimum(j, B_BLOCKS - 1), 0, 0, 0)),
        _const_spec(ga.shape), _const_spec(gb.shape),
    ]
    return pl.pallas_call(
        _attn_kernel,
        grid=(b, nq),
        in_specs=in_specs,
        out_specs=pl.BlockSpec((1, MIX_W, TQ), lambda bi, j: (bi, 0, j)),
        out_shape=jax.ShapeDtypeStruct((b, MIX_W, s), BF16),
        scratch_shapes=[pltpu.VMEM((QA_W, TQ), F32), pltpu.VMEM((QB_W, TQ), F32)],
        compiler_params=pltpu.CompilerParams(
            dimension_semantics=("arbitrary", "arbitrary"), vmem_limit_bytes=VMEM_LIMIT),
        name="attention",
    )(qat, qbt, kb, vbt, ka, vat, bias_a, sink_a, bias_b, ga, gb)


def _ffn_kernel(yt_ref, x_ref, wo_ref, g2_ref, w1_ref, w2_ref, gf_ref, o_ref, h_scr, n2_scr):
    h = x_ref[0] + _dot_tn(yt_ref[0], wo_ref[...])
    h_scr[...] = h
    n2_scr[...] = _rms(h, g2_ref[...]).astype(BF16)

    for c in range(D_FF // FF_CHUNK):
        cols = slice(c * FF_CHUNK, (c + 1) * FF_CHUNK)
        u = _dot(n2_scr[...], w1_ref[:, cols])
        u = jnp.square(jnp.maximum(u, 0.0)).astype(BF16)
        h_scr[...] += _dot(u, w2_ref[cols, :])
    o_ref[0] = _rms(h_scr[...], gf_ref[...])


def _ffn_call(yt, x, wo, g2, w1, w2, gf):
    b, s, _ = x.shape
    return pl.pallas_call(
        _ffn_kernel,
        grid=(b, s // TM_FFN),
        in_specs=[
            pl.BlockSpec((1, MIX_W, TM_FFN), lambda bi, i: (bi, 0, i)),
            pl.BlockSpec((1, TM_FFN, D_MODEL), lambda bi, i: (bi, i, 0)),
            _const_spec(wo.shape), _const_spec(g2.shape),
            _const_spec(w1.shape), _const_spec(w2.shape), _const_spec(gf.shape),
        ],
        out_specs=pl.BlockSpec((1, TM_FFN, D_MODEL), lambda bi, i: (bi, i, 0)),
        out_shape=jax.ShapeDtypeStruct((b, s, D_MODEL), F32),
        scratch_shapes=[pltpu.VMEM((TM_FFN, D_MODEL), F32), pltpu.VMEM((TM_FFN, D_MODEL), BF16)],
        compiler_params=pltpu.CompilerParams(
            dimension_semantics=("arbitrary", "arbitrary"), vmem_limit_bytes=VMEM_LIMIT),
        name="out_proj_mlp",
    )(yt, x, wo, g2, w1, w2, gf)


def _bias_a_table(sinks):
    k = np.arange(A_WIN)[:, None]
    i = np.arange(A_PAIR)[None, :]
    dist = np.abs(A_PAIR + i - k).astype(np.float32)
    qc = i // CHUNK
    kc = k // CHUNK
    allowed = (kc >= qc) & (kc <= qc + A_BAND_CHUNKS - 1)
    slopes = jnp.exp2(-8.0 * (jnp.arange(A_HEADS, dtype=F32) + 1.0) / A_HEADS)
    bias = -slopes[:, None, None] * jnp.asarray(dist)[None] * LOG2E
    bias = jnp.where(jnp.asarray(allowed)[None], bias, NEG_INF)
    bias = bias.reshape(A_KV_HEADS, A_GROUP, A_WIN, A_PAIR).transpose(0, 2, 1, 3)
    bias = bias.reshape(A_KV_HEADS, A_WIN, A_GROUP * A_PAIR)
    sink = jnp.broadcast_to((sinks.astype(F32) * LOG2E).reshape(A_KV_HEADS, 1, A_GROUP, 1),
                            (A_KV_HEADS, 1, A_GROUP, A_PAIR)).reshape(A_KV_HEADS, 1, A_GROUP * A_PAIR)
    return bias, sink


def _bias_b_table(rel_bias):
    period = TQ + B_WIN
    d = np.arange(period)
    d = np.where(d >= B_WIN, d - period, d)
    dist = (B_BAND_CHUNKS - 1) * CHUNK - d
    rel = np.clip(dist, -B_MAX_REL, B_MAX_REL) + B_MAX_REL
    row = (rel_bias.astype(F32) * LOG2E)[:, jnp.asarray(rel)]
    flat = jnp.tile(row, (1, TQ))[:, :TQ * (period - 1)]
    bias = flat.reshape(B_HEADS, TQ, period - 1)[:, :, :B_WIN]
    bias = jnp.swapaxes(bias, 1, 2)
    k = np.arange(B_WIN)[:, None]
    q = np.arange(TQ)[None, :]
    qc = q // CHUNK
    kc = k // CHUNK
    allowed = (kc >= qc) & (kc <= qc + B_BAND_CHUNKS - 1)
    variants = []
    for v in range(B_BLOCKS):
        ok = allowed & (k // TQ >= B_BLOCKS - 1 - v)
        variants.append(jnp.where(jnp.asarray(ok)[None], bias, NEG_INF))
    return jnp.stack(variants)


def kernel(x, norm1_g, w_in, sinks_a, rel_bias_b, out_norm_a_g, out_norm_b_g, w_out, norm2_g,
           w_ff1, w_ff2, final_norm_g):
    b, s, d = x.shape
    assert d == D_MODEL and s % TM_PROJ == 0 and s % TM_FFN == 0 and s % TQ == 0
    assert norm1_g.shape[0] == 1, "single-layer block"
    scale = HEAD_DIM ** -0.5 * LOG2E

    w = w_in[0]
    o_ka, o_va, o_qb, o_kb, o_vb = QA_W, QA_W + KA_W, QA_W + 2 * KA_W, QA_W + 2 * KA_W + QB_W, QA_W + 2 * KA_W + 2 * QB_W
    wk = jnp.concatenate([w[:, o_kb:o_vb], w[:, o_ka:o_va]], axis=1).astype(BF16)
    wt = jnp.concatenate([w[:, :QA_W] * scale, w[:, o_qb:o_kb] * scale, w[:, o_vb:], w[:, o_va:o_qb]],
                         axis=1).T.astype(BF16)
    kb, ka, qat, qbt, vbt, vat = _proj_call(x, norm1_g[0].reshape(1, d), wk, wt)

    bias_a, sink_a = _bias_a_table(sinks_a[0])
    bias_b = _bias_b_table(rel_bias_b[0])
    ga = jnp.broadcast_to(out_norm_a_g[0].astype(F32)[:, None], (QA_W, TQ))
    gb = jnp.broadcast_to(out_norm_b_g[0].astype(F32)[:, None], (QB_W, TQ))
    yt = _attn_call(kb, ka, qat, qbt, vbt, vat, bias_a, sink_a, bias_b, ga, gb)

    return _ffn_call(yt, x, w_out[0].astype(BF16), norm2_g[0].reshape(1, d),
                     w_ff1[0].astype(BF16), w_ff2[0].astype(BF16), final_norm_g.reshape(1, d))
```

```python
import functools

import jax
import jax.numpy as jnp
import numpy as np
from jax import lax
from jax.experimental import pallas as pl
from jax.experimental.pallas import tpu as pltpu

D_MODEL = 1024
CHUNK = 64
HEAD_DIM = 64
A_HEADS = 8
A_KV_HEADS = 2
A_GROUP = A_HEADS // A_KV_HEADS
A_BAND_CHUNKS = 3
B_HEADS = 8
B_BAND_CHUNKS = 9
B_MAX_REL = 128
D_FF = 4 * D_MODEL
EPS = 1e-6
NEG_INF = -1e30
LOG2E = 1.4426950408889634

QA_W = A_HEADS * HEAD_DIM
KA_W = A_KV_HEADS * HEAD_DIM
QB_W = B_HEADS * HEAD_DIM
MIX_W = QA_W + QB_W
K_W = QB_W + KA_W
T_W = QA_W + 2 * QB_W + KA_W

LANES = 128
TQ = 256
A_PAIR = 2 * CHUNK
A_WIN = 4 * CHUNK
B_BLOCKS = 3
B_WIN = B_BLOCKS * TQ
def _b_live_tiles():
    chunks_per_tile = A_PAIR // CHUNK
    live = []
    for p in range(B_BLOCKS):
        tiles = []
        for rt in range(TQ // A_PAIR):
            for lt in range(TQ // LANES):
                kcs = [p * (TQ // CHUNK) + rt * chunks_per_tile + i for i in range(chunks_per_tile)]
                qcs = [lt * (LANES // CHUNK) + i for i in range(LANES // CHUNK)]
                if any(qc <= kc <= qc + B_BAND_CHUNKS - 1 for kc in kcs for qc in qcs):
                    tiles.append((rt, lt))
        live.append(tuple(tiles))
    return tuple(live)


B_LIVE_TILES = _b_live_tiles()
PIPE_DEPTH = 7
TM_PROJ = 1024
PROJ_SUB = 512
TM_FFN = 1024
FF_CHUNK = 1024
VMEM_LIMIT = 56 * 1024 * 1024

F32 = jnp.float32
BF16 = jnp.bfloat16


def _rms(x, g):
    ms = jnp.mean(x * x, axis=-1, keepdims=True)
    return x * lax.rsqrt(ms + EPS) * g


def _rms_rows(xt, g):
    ms = jnp.mean(xt * xt, axis=0, keepdims=True)
    return xt * lax.rsqrt(ms + EPS) * g


def _dot(a, b):
    return jnp.dot(a, b, preferred_element_type=F32)


def _dot_nt(a, b):
    return lax.dot_general(a, b, (((1,), (1,)), ((), ())), preferred_element_type=F32)


def _dot_tn(a, b):
    return lax.dot_general(a, b, (((0,), (0,)), ((), ())), preferred_element_type=F32)


def _const_spec(shape):
    nd = len(shape)
    return pl.BlockSpec(shape, lambda *_: (0,) * nd, pipeline_mode=pl.Buffered(1))


def _proj_kernel(x_ref, g_ref, wk_ref, wt_ref, kb_ref, ka_ref, qat_ref, qbt_ref, vbt_ref, vat_ref):
    for i in range(TM_PROJ // PROJ_SUB):
        tok = slice(i * PROJ_SUB, (i + 1) * PROJ_SUB)
        n = _rms(x_ref[0, tok, :], g_ref[...]).astype(BF16)
        k = _dot(n, wk_ref[...]).astype(BF16)
        kb_ref[0, tok, :] = k[:, :QB_W]
        ka_ref[0, tok, :] = k[:, QB_W:]
        t = _dot_nt(wt_ref[...], n).astype(BF16)
        qat_ref[0, :, tok] = t[:QA_W]
        qbt_ref[0, :, tok] = t[QA_W:QA_W + QB_W]
        vbt_ref[0, :, tok] = t[QA_W + QB_W:QA_W + 2 * QB_W]
        vat_ref[0, :, tok] = t[QA_W + 2 * QB_W:]


def _proj_call(x, g, wk, wt):
    b, s, _ = x.shape

    def tok(width):
        return pl.BlockSpec((1, TM_PROJ, width), lambda bi, i: (bi, i, 0))

    def feat(width):
        return pl.BlockSpec((1, width, TM_PROJ), lambda bi, i: (bi, 0, i))

    return pl.pallas_call(
        _proj_kernel,
        grid=(b, s // TM_PROJ),
        in_specs=[tok(D_MODEL), _const_spec(g.shape), _const_spec(wk.shape), _const_spec(wt.shape)],
        out_specs=[tok(QB_W), tok(KA_W), feat(QA_W), feat(QB_W), feat(QB_W), feat(KA_W)],
        out_shape=[
            jax.ShapeDtypeStruct((b, s, QB_W), BF16), jax.ShapeDtypeStruct((b, s, KA_W), BF16),
            jax.ShapeDtypeStruct((b, QA_W, s), BF16), jax.ShapeDtypeStruct((b, QB_W, s), BF16),
            jax.ShapeDtypeStruct((b, QB_W, s), BF16), jax.ShapeDtypeStruct((b, KA_W, s), BF16),
        ],
        compiler_params=pltpu.CompilerParams(
            dimension_semantics=("arbitrary", "arbitrary"), vmem_limit_bytes=VMEM_LIMIT),
        name="norm_in_proj",
    )(x, g, wk, wt)


def _attn_kernel(qat_ref, qbt_ref, kb0_ref, kb1_ref, kb2_ref, vbt0_ref, vbt1_ref, vbt2_ref,
                 kap_ref, kac_ref, vatp_ref, vatc_ref,
                 bias_a_ref, sink_ref, bias_b_ref, ga_ref, gb_ref,
                 y_ref, ya_scr, yb_scr):
    j = pl.program_id(1)
    zeros_q = jnp.zeros((HEAD_DIM, TQ), BF16)
    ones_k = jnp.ones((HEAD_DIM, TQ), BF16)
    zeros_p = jnp.zeros((HEAD_DIM, A_PAIR), BF16)
    kb_refs = (kb0_ref, kb1_ref, kb2_ref)
    vbt_refs = (vbt0_ref, vbt1_ref, vbt2_ref)
    pen_a = jnp.where(j >= 1, 0.0, NEG_INF).astype(F32)


    b_parts = {}

    def b_scores(h, p):
        hp, half = divmod(h, 2)
        qt = qbt_ref[0, h * HEAD_DIM:(h + 1) * HEAD_DIM, :]
        qm = jnp.concatenate([qt, zeros_q] if half == 0 else [zeros_q, qt], axis=0)
        raw = _dot(kb_refs[p][0, :, hp * LANES:(hp + 1) * LANES], qm)
        tiles = {}
        for rt, lt in B_LIVE_TILES[p]:
            ks = slice(rt * A_PAIR, (rt + 1) * A_PAIR)
            qs = slice(lt * LANES, (lt + 1) * LANES)
            tiles[rt, lt] = raw[ks, qs] + bias_b_ref[0, h, p * TQ + rt * A_PAIR:p * TQ + (rt + 1) * A_PAIR, qs]
        mcols = []
        for lt in range(TQ // LANES):
            col = functools.reduce(jnp.maximum, [t for (_, l2), t in tiles.items() if l2 == lt])
            mcols.append(jnp.max(col, axis=0, keepdims=True))
        return tiles, mcols

    def b_output(h, p, tiles, mcols):
        rows = slice(h * HEAD_DIM, (h + 1) * HEAD_DIM)
        dead = jnp.zeros((A_PAIR, LANES), BF16)
        pt = jnp.concatenate([
            jnp.concatenate([jnp.exp2(tiles[rt, lt] - mcols[lt]).astype(BF16) if (rt, lt) in tiles else dead
                             for rt in range(TQ // A_PAIR)], axis=0)
            for lt in range(TQ // LANES)], axis=1)
        mp = jnp.concatenate(mcols, axis=1)
        vt = jnp.concatenate([vbt_refs[p][0, rows, :], ones_k], axis=0)
        b_parts.setdefault(h, []).append((_dot(vt, pt)[:HEAD_DIM + 8], mp))
        if p == B_BLOCKS - 1:
            parts = b_parts.pop(h)
            m = functools.reduce(jnp.maximum, [mq for _, mq in parts])
            ot = sum(op * jnp.exp2(mq - m) for op, mq in parts)
            yb_scr[rows, :] = ot[:HEAD_DIM] * (1.0 / ot[HEAD_DIM:HEAD_DIM + 1])

    def a_windows(r):
        if r == 0:
            kwin = jnp.concatenate([kap_ref[0], kac_ref[0, :A_PAIR, :]], axis=0)
            vwin = jnp.concatenate([vatp_ref[0], vatc_ref[0, :, :A_PAIR]], axis=1)
            return kwin, vwin
        return kac_ref[0], vatc_ref[0]

    def a_scores(r, kvh):
        tok = slice(r * A_PAIR, (r + 1) * A_PAIR)
        kwin, _ = a_windows(r)
        blocks = []
        for g in range(A_GROUP):
            h = kvh * A_GROUP + g
            qt = qat_ref[0, h * HEAD_DIM:(h + 1) * HEAD_DIM, tok]
            blocks.append(jnp.concatenate([qt, zeros_p] if kvh == 0 else [zeros_p, qt], axis=0))
        qst = jnp.concatenate(blocks, axis=1)
        st = _dot(kwin, qst) + bias_a_ref[kvh]
        s0 = st[:A_PAIR]
        s1 = st[A_PAIR:]
        if r == 0:
            s0 = s0 + pen_a
        m = jnp.maximum(jnp.max(jnp.maximum(s0, s1), axis=0, keepdims=True), sink_ref[kvh])
        return (s0, s1), m

    def a_output(r, kvh, st, m):
        tok = slice(r * A_PAIR, (r + 1) * A_PAIR)
        _, vwin = a_windows(r)
        pt = jnp.concatenate([jnp.exp2(st[0] - m), jnp.exp2(st[1] - m)], axis=0).astype(BF16)
        vt = jnp.concatenate([vwin[kvh * HEAD_DIM:(kvh + 1) * HEAD_DIM, :], ones_k], axis=0)
        ot = _dot(vt, pt)
        den = ot[HEAD_DIM:HEAD_DIM + 1] + jnp.exp2(sink_ref[kvh] - m)
        yt = ot[:HEAD_DIM] * (1.0 / den)
        for g in range(A_GROUP):
            h = kvh * A_GROUP + g
            ya_scr[h * HEAD_DIM:(h + 1) * HEAD_DIM, tok] = yt[:, g * A_PAIR:(g + 1) * A_PAIR]

    units = [(b_scores, b_output, (h, p)) for h in range(B_HEADS) for p in range(B_BLOCKS)]
    units += [(a_scores, a_output, (r, kvh)) for r in range(TQ // A_PAIR) for kvh in range(A_KV_HEADS)]
    pending = []
    for scores, output, args in units:
        pending.append((output, args, scores(*args)))
        if len(pending) > PIPE_DEPTH:
            output, args, staged = pending.pop(0)
            output(*args, *staged)
    for output, args, staged in pending:
        output(*args, *staged)

    y_ref[0, QA_W:, :] = _rms_rows(yb_scr[...], gb_ref[...]).astype(BF16)
    y_ref[0, :QA_W, :] = _rms_rows(ya_scr[...], ga_ref[...]).astype(BF16)


def _attn_call(kb, ka, qat, qbt, vbt, vat, bias_a, sink_a, bias_b, ga, gb):
    b, s, _ = kb.shape
    nq = s // TQ

    def feat_cur(width):
        return pl.BlockSpec((1, width, TQ), lambda bi, j: (bi, 0, j))

    def tok_back(back):
        return pl.BlockSpec((1, TQ, QB_W), lambda bi, j: (bi, jnp.maximum(j - back, 0), 0))

    def feat_back(back):
        return pl.BlockSpec((1, QB_W, TQ), lambda bi, j: (bi, 0, jnp.maximum(j - back, 0)))

    in_specs = [
        feat_cur(QA_W), feat_cur(QB_W),
        tok_back(2), tok_back(1), tok_back(0),
        feat_back(2), feat_back(1), feat_back(0),
        pl.BlockSpec((1, A_PAIR, KA_W), lambda bi, j: (bi, jnp.maximum(2 * j - 1, 0), 0)),
        pl.BlockSpec((1, TQ, KA_W), lambda bi, j: (bi, j, 0)),
        pl.BlockSpec((1, KA_W, A_PAIR), lambda bi, j: (bi, 0, jnp.maximum(2 * j - 1, 0))),
        pl.BlockSpec((1, KA_W, TQ), lambda bi, j: (bi, 0, j)),
        _const_spec(bias_a.shape), _const_spec(sink_a.shape),
        pl.BlockSpec((1,) + bias_b.shape[1:], lambda bi, j: (jnp.minimum(j, B_BLOCKS - 1), 0, 0, 0)),
        _const_spec(ga.shape), _const_spec(gb.shape),
    ]
    return pl.pallas_call(
        _attn_kernel,
        grid=(b, nq),
        in_specs=in_specs,
        out_specs=pl.BlockSpec((1, MIX_W, TQ), lambda bi, j: (bi, 0, j)),
        out_shape=jax.ShapeDtypeStruct((b, MIX_W, s), BF16),
        scratch_shapes=[pltpu.VMEM((QA_W, TQ), F32), pltpu.VMEM((QB_W, TQ), F32)],
        compiler_params=pltpu.CompilerParams(
            dimension_semantics=("arbitrary", "arbitrary"), vmem_limit_bytes=VMEM_LIMIT),
        name="attention",
    )(qat, qbt, kb, kb, kb, vbt, vbt, vbt, ka, ka, vat, vat, bias_a, sink_a, bias_b, ga, gb)


def _ffn_kernel(yt_ref, x_ref, wo_ref, g2_ref, w1_ref, w2_ref, gf_ref, o_ref, h_scr, n2_scr):
    h = x_ref[0] + _dot_tn(yt_ref[0], wo_ref[...])
    h_scr[...] = h
    n2_scr[...] = _rms(h, g2_ref[...]).astype(BF16)

    for c in range(D_FF // FF_CHUNK):
        cols = slice(c * FF_CHUNK, (c + 1) * FF_CHUNK)
        u = _dot(n2_scr[...], w1_ref[:, cols])
        u = jnp.square(jnp.maximum(u, 0.0)).astype(BF16)
        h_scr[...] += _dot(u, w2_ref[cols, :])
    o_ref[0] = _rms(h_scr[...], gf_ref[...])


def _ffn_call(yt, x, wo, g2, w1, w2, gf):
    b, s, _ = x.shape
    return pl.pallas_call(
        _ffn_kernel,
        grid=(b, s // TM_FFN),
        in_specs=[
            pl.BlockSpec((1, MIX_W, TM_FFN), lambda bi, i: (bi, 0, i)),
            pl.BlockSpec((1, TM_FFN, D_MODEL), lambda bi, i: (bi, i, 0)),
            _const_spec(wo.shape), _const_spec(g2.shape),
            _const_spec(w1.shape), _const_spec(w2.shape), _const_spec(gf.shape),
        ],
        out_specs=pl.BlockSpec((1, TM_FFN, D_MODEL), lambda bi, i: (bi, i, 0)),
        out_shape=jax.ShapeDtypeStruct((b, s, D_MODEL), F32),
        scratch_shapes=[pltpu.VMEM((TM_FFN, D_MODEL), F32), pltpu.VMEM((TM_FFN, D_MODEL), BF16)],
        compiler_params=pltpu.CompilerParams(
            dimension_semantics=("arbitrary", "arbitrary"), vmem_limit_bytes=VMEM_LIMIT),
        name="out_proj_mlp",
    )(yt, x, wo, g2, w1, w2, gf)


def _bias_a_table(sinks):
    k = np.arange(A_WIN)[:, None]
    i = np.arange(A_PAIR)[None, :]
    dist = np.abs(A_PAIR + i - k).astype(np.float32)
    qc = i // CHUNK
    kc = k // CHUNK
    allowed = (kc >= qc) & (kc <= qc + A_BAND_CHUNKS - 1)
    slopes = jnp.exp2(-8.0 * (jnp.arange(A_HEADS, dtype=F32) + 1.0) / A_HEADS)
    bias = -slopes[:, None, None] * jnp.asarray(dist)[None] * LOG2E
    bias = jnp.where(jnp.asarray(allowed)[None], bias, NEG_INF)
    bias = bias.reshape(A_KV_HEADS, A_GROUP, A_WIN, A_PAIR).transpose(0, 2, 1, 3)
    bias = bias.reshape(A_KV_HEADS, A_WIN, A_GROUP * A_PAIR)
    sink = jnp.broadcast_to((sinks.astype(F32) * LOG2E).reshape(A_KV_HEADS, 1, A_GROUP, 1),
                            (A_KV_HEADS, 1, A_GROUP, A_PAIR)).reshape(A_KV_HEADS, 1, A_GROUP * A_PAIR)
    return bias, sink


def _bias_b_table(rel_bias):
    period = TQ + B_WIN
    d = np.arange(period)
    d = np.where(d >= B_WIN, d - period, d)
    dist = (B_BAND_CHUNKS - 1) * CHUNK - d
    rel = np.clip(dist, -B_MAX_REL, B_MAX_REL) + B_MAX_REL
    row = (rel_bias.astype(F32) * LOG2E)[:, jnp.asarray(rel)]
    flat = jnp.tile(row, (1, TQ))[:, :TQ * (period - 1)]
    bias = flat.reshape(B_HEADS, TQ, period - 1)[:, :, :B_WIN]
    bias = jnp.swapaxes(bias, 1, 2)
    k = np.arange(B_WIN)[:, None]
    q = np.arange(TQ)[None, :]
    qc = q // CHUNK
    kc = k // CHUNK
    allowed = (kc >= qc) & (kc <= qc + B_BAND_CHUNKS - 1)
    ok = np.stack([allowed & (k // TQ >= B_BLOCKS - 1 - v) for v in range(B_BLOCKS)])
    return jnp.where(jnp.asarray(ok)[:, None], bias[None], NEG_INF)


def kernel(x, norm1_g, w_in, sinks_a, rel_bias_b, out_norm_a_g, out_norm_b_g, w_out, norm2_g,
           w_ff1, w_ff2, final_norm_g):
    b, s, d = x.shape
    assert d == D_MODEL and s % TM_PROJ == 0 and s % TM_FFN == 0 and s % TQ == 0
    assert norm1_g.shape[0] == 1, "single-layer block"
    scale = HEAD_DIM ** -0.5 * LOG2E

    w = w_in[0]
    o_ka, o_va, o_qb, o_kb, o_vb = QA_W, QA_W + KA_W, QA_W + 2 * KA_W, QA_W + 2 * KA_W + QB_W, QA_W + 2 * KA_W + 2 * QB_W
    wk = jnp.concatenate([w[:, o_kb:o_vb], w[:, o_ka:o_va]], axis=1).astype(BF16)
    wt = jnp.concatenate([w[:, :QA_W] * scale, w[:, o_qb:o_kb] * scale, w[:, o_vb:], w[:, o_va:o_qb]],
                         axis=1).T.astype(BF16)
    kb, ka, qat, qbt, vbt, vat = _proj_call(x, norm1_g[0].reshape(1, d), wk, wt)

    bias_a, sink_a = _bias_a_table(sinks_a[0])
    bias_b = _bias_b_table(rel_bias_b[0])
    ga = jnp.broadcast_to(out_norm_a_g[0].astype(F32)[:, None], (QA_W, TQ))
    gb = jnp.broadcast_to(out_norm_b_g[0].astype(F32)[:, None], (QB_W, TQ))
    yt = _attn_call(kb, ka, qat, qbt, vbt, vat, bias_a, sink_a, bias_b, ga, gb)

    return _ffn_call(yt, x, w_out[0].astype(BF16), norm2_g[0].reshape(1, d),
                     w_ff1[0].astype(BF16), w_ff2[0].astype(BF16), final_norm_g.reshape(1, d))
```

```python
import functools

import jax
import jax.numpy as jnp
import numpy as np
from jax import lax
from jax.experimental import pallas as pl
from jax.experimental.pallas import tpu as pltpu

D_MODEL = 1024
CHUNK = 64
HEAD_DIM = 64
A_HEADS = 8
A_KV_HEADS = 2
A_GROUP = A_HEADS // A_KV_HEADS
A_BAND_CHUNKS = 3
B_HEADS = 8
B_BAND_CHUNKS = 9
B_MAX_REL = 128
D_FF = 4 * D_MODEL
EPS = 1e-6
NEG_INF = -1e30
LOG2E = 1.4426950408889634

QA_W = A_HEADS * HEAD_DIM
KA_W = A_KV_HEADS * HEAD_DIM
QB_W = B_HEADS * HEAD_DIM
MIX_W = QA_W + QB_W

LANES = 128
TQ = 256
A_PAIR = 2 * CHUNK
A_WIN = 4 * CHUNK
B_BLOCKS = 3
B_WIN = B_BLOCKS * TQ
def _b_live_tiles():
    chunks_per_tile = A_PAIR // CHUNK
    live = []
    for p in range(B_BLOCKS):
        tiles = []
        for rt in range(TQ // A_PAIR):
            for lt in range(TQ // LANES):
                kcs = [p * (TQ // CHUNK) + rt * chunks_per_tile + i for i in range(chunks_per_tile)]
                qcs = [lt * (LANES // CHUNK) + i for i in range(LANES // CHUNK)]
                if any(qc <= kc <= qc + B_BAND_CHUNKS - 1 for kc in kcs for qc in qcs):
                    tiles.append((rt, lt))
        live.append(tuple(tiles))
    return tuple(live)


B_LIVE_TILES = _b_live_tiles()
ONES_ROWS = 16
PIPE_DEPTH = 7
TM_PROJ = 1024
PROJ_SUB = 512
TM_FFN = 1024
FF_CHUNK = 1024
VMEM_LIMIT = 56 * 1024 * 1024

F32 = jnp.float32
BF16 = jnp.bfloat16


def _rms(x, g):
    ms = jnp.mean(x * x, axis=-1, keepdims=True)
    return x * lax.rsqrt(ms + EPS) * g


def _rms_rows(xt, g):
    ms = jnp.mean(xt * xt, axis=0, keepdims=True)
    return xt * lax.rsqrt(ms + EPS) * g


def _dot(a, b):
    return jnp.dot(a, b, preferred_element_type=F32)


def _dot_nt(a, b):
    return lax.dot_general(a, b, (((1,), (1,)), ((), ())), preferred_element_type=F32)


def _dot_tn(a, b):
    return lax.dot_general(a, b, (((0,), (0,)), ((), ())), preferred_element_type=F32)


def _const_spec(shape):
    nd = len(shape)
    return pl.BlockSpec(shape, lambda *_: (0,) * nd, pipeline_mode=pl.Buffered(1))


def _proj_kernel(x_ref, g_ref, wk_ref, wt_ref, wo_ref, w1_ref, w2_ref,
                 kb_ref, qat_ref, qbt_ref, vbt_ref, kvat_ref, wo_bf_ref, w1_bf_ref, w2_bf_ref):
    for i in range(TM_PROJ // PROJ_SUB):
        tok = slice(i * PROJ_SUB, (i + 1) * PROJ_SUB)
        n = _rms(x_ref[0, tok, :], g_ref[...]).astype(BF16)
        kb_ref[0, tok, :] = _dot(n, wk_ref[...]).astype(BF16)
        t = _dot_nt(wt_ref[...], n).astype(BF16)
        qat_ref[0, :, tok] = t[:QA_W]
        qbt_ref[0, :, tok] = t[QA_W:QA_W + QB_W]
        vbt_ref[0, :, tok] = t[QA_W + QB_W:QA_W + 2 * QB_W]
        kvat_ref[0, :, tok] = t[QA_W + 2 * QB_W:]
    wo_bf_ref[...] = wo_ref[...].astype(BF16)
    w1_bf_ref[...] = w1_ref[...].astype(BF16)
    w2_bf_ref[...] = w2_ref[...].astype(BF16)


def _proj_call(x, g, wk, wt, wo, w1, w2):
    b, s, _ = x.shape
    steps = b * (s // TM_PROJ)

    def tok(width):
        return pl.BlockSpec((1, TM_PROJ, width), lambda bi, i: (bi, i, 0))

    def feat(width):
        return pl.BlockSpec((1, width, TM_PROJ), lambda bi, i: (bi, 0, i))

    def row_share(w):
        rows = w.shape[0] // steps
        assert rows * steps == w.shape[0] and rows % 16 == 0, "weight rows must split into bf16 row tiles"
        return pl.BlockSpec((rows, w.shape[1]), lambda bi, i: (bi * (s // TM_PROJ) + i, 0))

    def bf(w):
        return jax.ShapeDtypeStruct(w.shape, BF16)

    return pl.pallas_call(
        _proj_kernel,
        grid=(b, s // TM_PROJ),
        in_specs=[tok(D_MODEL), _const_spec(g.shape), _const_spec(wk.shape), _const_spec(wt.shape),
                  row_share(wo), row_share(w1), row_share(w2)],
        out_specs=[tok(QB_W), feat(QA_W), feat(QB_W), feat(QB_W), feat(2 * KA_W),
                   row_share(wo), row_share(w1), row_share(w2)],
        out_shape=[
            jax.ShapeDtypeStruct((b, s, QB_W), BF16),
            jax.ShapeDtypeStruct((b, QA_W, s), BF16), jax.ShapeDtypeStruct((b, QB_W, s), BF16),
            jax.ShapeDtypeStruct((b, QB_W, s), BF16), jax.ShapeDtypeStruct((b, 2 * KA_W, s), BF16),
            bf(wo), bf(w1), bf(w2),
        ],
        compiler_params=pltpu.CompilerParams(
            dimension_semantics=("arbitrary", "arbitrary"), vmem_limit_bytes=VMEM_LIMIT),
        name="norm_in_proj",
    )(x, g, wk, wt, wo, w1, w2)


def _attn_kernel(qat_ref, qbt_ref, kb0_ref, kb1_ref, kb2_ref, vbt0_ref, vbt1_ref, vbt2_ref,
                 kvap_ref, kvac_ref,
                 bias_a_ref, sink_ref, bias_b_ref, ga_ref, gb_ref,
                 y_ref, ya_scr, yb_scr):
    j = pl.program_id(1)
    zeros_q = jnp.zeros((HEAD_DIM, TQ), BF16)
    ones_k = jnp.ones((ONES_ROWS, TQ), BF16)
    zeros_p = jnp.zeros((HEAD_DIM, A_PAIR), BF16)
    kb_refs = (kb0_ref, kb1_ref, kb2_ref)
    vbt_refs = (vbt0_ref, vbt1_ref, vbt2_ref)
    pen_a = jnp.where(j >= 1, 0.0, NEG_INF).astype(F32)


    b_parts = {}

    def b_scores(h, p):
        hp, half = divmod(h, 2)
        qt = qbt_ref[0, h * HEAD_DIM:(h + 1) * HEAD_DIM, :]
        qm = jnp.concatenate([qt, zeros_q] if half == 0 else [zeros_q, qt], axis=0)
        raw = _dot(kb_refs[p][0, :, hp * LANES:(hp + 1) * LANES], qm)
        tiles = {}
        for rt, lt in B_LIVE_TILES[p]:
            ks = slice(rt * A_PAIR, (rt + 1) * A_PAIR)
            qs = slice(lt * LANES, (lt + 1) * LANES)
            tiles[rt, lt] = raw[ks, qs] + bias_b_ref[0, h, p * TQ + rt * A_PAIR:p * TQ + (rt + 1) * A_PAIR, qs]
        mcols = []
        for lt in range(TQ // LANES):
            col = functools.reduce(jnp.maximum, [t for (_, l2), t in tiles.items() if l2 == lt])
            mcols.append(jnp.max(col, axis=0, keepdims=True))
        return tiles, mcols

    def b_output(h, p, tiles, mcols):
        rows = slice(h * HEAD_DIM, (h + 1) * HEAD_DIM)
        dead = jnp.zeros((A_PAIR, LANES), BF16)
        pt = jnp.concatenate([
            jnp.concatenate([jnp.exp2(tiles[rt, lt] - mcols[lt]).astype(BF16) if (rt, lt) in tiles else dead
                             for rt in range(TQ // A_PAIR)], axis=0)
            for lt in range(TQ // LANES)], axis=1)
        mp = jnp.concatenate(mcols, axis=1)
        vt = jnp.concatenate([vbt_refs[p][0, rows, :], ones_k], axis=0)
        b_parts.setdefault(h, []).append((_dot(vt, pt)[:HEAD_DIM + 8], mp))
        if p == B_BLOCKS - 1:
            parts = b_parts.pop(h)
            m = functools.reduce(jnp.maximum, [mq for _, mq in parts])
            ot = sum(op * jnp.exp2(mq - m) for op, mq in parts)
            yb_scr[rows, :] = ot[:HEAD_DIM] * (1.0 / ot[HEAD_DIM:HEAD_DIM + 1])

    def a_windows(r):
        if r == 0:
            kvwin = jnp.concatenate([kvap_ref[0], kvac_ref[0, :, :A_PAIR]], axis=1)
        else:
            kvwin = kvac_ref[0]
        return kvwin[:KA_W], kvwin[KA_W:]

    def a_scores(r, kvh):
        tok = slice(r * A_PAIR, (r + 1) * A_PAIR)
        kwin, _ = a_windows(r)
        blocks = []
        for g in range(A_GROUP):
            h = kvh * A_GROUP + g
            qt = qat_ref[0, h * HEAD_DIM:(h + 1) * HEAD_DIM, tok]
            blocks.append(jnp.concatenate([qt, zeros_p] if kvh == 0 else [zeros_p, qt], axis=0))
        qst = jnp.concatenate(blocks, axis=1)
        st = _dot_tn(kwin, qst) + bias_a_ref[kvh]
        s0 = st[:A_PAIR]
        s1 = st[A_PAIR:]
        if r == 0:
            s0 = s0 + pen_a
        m = jnp.maximum(jnp.max(jnp.maximum(s0, s1), axis=0, keepdims=True), sink_ref[kvh])
        return (s0, s1), m

    def a_output(r, kvh, st, m):
        tok = slice(r * A_PAIR, (r + 1) * A_PAIR)
        _, vwin = a_windows(r)
        pt = jnp.concatenate([jnp.exp2(st[0] - m), jnp.exp2(st[1] - m)], axis=0).astype(BF16)
        vt = jnp.concatenate([vwin[kvh * HEAD_DIM:(kvh + 1) * HEAD_DIM, :], ones_k], axis=0)
        ot = _dot(vt, pt)
        den = ot[HEAD_DIM:HEAD_DIM + 1] + jnp.exp2(sink_ref[kvh] - m)
        yt = ot[:HEAD_DIM] * (1.0 / den)
        for g in range(A_GROUP):
            h = kvh * A_GROUP + g
            ya_scr[h * HEAD_DIM:(h + 1) * HEAD_DIM, tok] = yt[:, g * A_PAIR:(g + 1) * A_PAIR]

    units = [(b_scores, b_output, (h, p)) for h in range(B_HEADS) for p in range(B_BLOCKS)]
    units += [(a_scores, a_output, (r, kvh)) for r in range(TQ // A_PAIR) for kvh in range(A_KV_HEADS)]
    pending = []
    for scores, output, args in units:
        pending.append((output, args, scores(*args)))
        if len(pending) > PIPE_DEPTH:
            output, args, staged = pending.pop(0)
            output(*args, *staged)
    for output, args, staged in pending:
        output(*args, *staged)

    y_ref[0, QA_W:, :] = _rms_rows(yb_scr[...], gb_ref[...]).astype(BF16)
    y_ref[0, :QA_W, :] = _rms_rows(ya_scr[...], ga_ref[...]).astype(BF16)


def _attn_call(kb, qat, qbt, vbt, kvat, bias_a, sink_a, bias_b, ga, gb):
    b, s, _ = kb.shape
    nq = s // TQ

    def feat_cur(width):
        return pl.BlockSpec((1, width, TQ), lambda bi, j: (bi, 0, j))

    def tok_back(back):
        return pl.BlockSpec((1, TQ, QB_W), lambda bi, j: (bi, jnp.maximum(j - back, 0), 0))

    def feat_back(back):
        return pl.BlockSpec((1, QB_W, TQ), lambda bi, j: (bi, 0, jnp.maximum(j - back, 0)))

    in_specs = [
        feat_cur(QA_W), feat_cur(QB_W),
        tok_back(2), tok_back(1), tok_back(0),
        feat_back(2), feat_back(1), feat_back(0),
        pl.BlockSpec((1, 2 * KA_W, A_PAIR), lambda bi, j: (bi, 0, jnp.maximum(2 * j - 1, 0))),
        pl.BlockSpec((1, 2 * KA_W, TQ), lambda bi, j: (bi, 0, j)),
        _const_spec(bias_a.shape), _const_spec(sink_a.shape),
        pl.BlockSpec((1,) + bias_b.shape[1:], lambda bi, j: (jnp.minimum(j, B_BLOCKS - 1), 0, 0, 0)),
        _const_spec(ga.shape), _const_spec(gb.shape),
    ]
    return pl.pallas_call(
        _attn_kernel,
        grid=(b, nq),
        in_specs=in_specs,
        out_specs=pl.BlockSpec((1, MIX_W, TQ), lambda bi, j: (bi, 0, j)),
        out_shape=jax.ShapeDtypeStruct((b, MIX_W, s), BF16),
        scratch_shapes=[pltpu.VMEM((QA_W, TQ), F32), pltpu.VMEM((QB_W, TQ), F32)],
        compiler_params=pltpu.CompilerParams(
            dimension_semantics=("arbitrary", "arbitrary"), vmem_limit_bytes=VMEM_LIMIT),
        name="attention",
    )(qat, qbt, kb, kb, kb, vbt, vbt, vbt, kvat, kvat, bias_a, sink_a, bias_b, ga, gb)


def _ffn_kernel(yt_ref, x_ref, wo_ref, g2_ref, w1_ref, w2_ref, gf_ref, o_ref, h_scr, n2_scr):
    h = x_ref[0] + _dot_tn(yt_ref[0], wo_ref[...])
    h_scr[...] = h
    n2_scr[...] = _rms(h, g2_ref[...]).astype(BF16)

    for c in range(D_FF // FF_CHUNK):
        cols = slice(c * FF_CHUNK, (c + 1) * FF_CHUNK)
        u = _dot(n2_scr[...], w1_ref[:, cols])
        u = jnp.square(jnp.maximum(u, 0.0)).astype(BF16)
        h_scr[...] += _dot(u, w2_ref[cols, :])
    o_ref[0] = _rms(h_scr[...], gf_ref[...])


def _ffn_call(yt, x, wo, g2, w1, w2, gf):
    b, s, _ = x.shape
    return pl.pallas_call(
        _ffn_kernel,
        grid=(b, s // TM_FFN),
        in_specs=[
            pl.BlockSpec((1, MIX_W, TM_FFN), lambda bi, i: (bi, 0, i)),
            pl.BlockSpec((1, TM_FFN, D_MODEL), lambda bi, i: (bi, i, 0)),
            _const_spec(wo.shape), _const_spec(g2.shape),
            _const_spec(w1.shape), _const_spec(w2.shape), _const_spec(gf.shape),
        ],
        out_specs=pl.BlockSpec((1, TM_FFN, D_MODEL), lambda bi, i: (bi, i, 0)),
        out_shape=jax.ShapeDtypeStruct((b, s, D_MODEL), F32),
        scratch_shapes=[pltpu.VMEM((TM_FFN, D_MODEL), F32), pltpu.VMEM((TM_FFN, D_MODEL), BF16)],
        compiler_params=pltpu.CompilerParams(
            dimension_semantics=("arbitrary", "arbitrary"), vmem_limit_bytes=VMEM_LIMIT),
        name="out_proj_mlp",
    )(yt, x, wo, g2, w1, w2, gf)


def _bias_a_table(sinks):
    k = np.arange(A_WIN)[:, None]
    i = np.arange(A_PAIR)[None, :]
    dist = np.abs(A_PAIR + i - k).astype(np.float32)
    qc = i // CHUNK
    kc = k // CHUNK
    allowed = (kc >= qc) & (kc <= qc + A_BAND_CHUNKS - 1)
    slopes = jnp.exp2(-8.0 * (jnp.arange(A_HEADS, dtype=F32) + 1.0) / A_HEADS)
    bias = -slopes[:, None, None] * jnp.asarray(dist)[None] * LOG2E
    bias = jnp.where(jnp.asarray(allowed)[None], bias, NEG_INF)
    bias = bias.reshape(A_KV_HEADS, A_GROUP, A_WIN, A_PAIR).transpose(0, 2, 1, 3)
    bias = bias.reshape(A_KV_HEADS, A_WIN, A_GROUP * A_PAIR)
    sink = jnp.broadcast_to((sinks.astype(F32) * LOG2E).reshape(A_KV_HEADS, 1, A_GROUP, 1),
                            (A_KV_HEADS, 1, A_GROUP, A_PAIR)).reshape(A_KV_HEADS, 1, A_GROUP * A_PAIR)
    return bias, sink


def _bias_b_table(rel_bias):
    period = TQ + B_WIN
    d = np.arange(period)
    d = np.where(d >= B_WIN, d - period, d)
    dist = (B_BAND_CHUNKS - 1) * CHUNK - d
    rel = np.clip(dist, -B_MAX_REL, B_MAX_REL) + B_MAX_REL
    row = (rel_bias.astype(F32) * LOG2E)[:, jnp.asarray(rel)]
    flat = jnp.tile(row, (1, TQ))[:, :TQ * (period - 1)]
    bias = flat.reshape(B_HEADS, TQ, period - 1)[:, :, :B_WIN]
    bias = jnp.swapaxes(bias, 1, 2)
    k = np.arange(B_WIN)[:, None]
    q = np.arange(TQ)[None, :]
    qc = q // CHUNK
    kc = k // CHUNK
    allowed = (kc >= qc) & (kc <= qc + B_BAND_CHUNKS - 1)
    ok = np.stack([allowed & (k // TQ >= B_BLOCKS - 1 - v) for v in range(B_BLOCKS)])
    return jnp.where(jnp.asarray(ok)[:, None], bias[None], NEG_INF)


def kernel(x, norm1_g, w_in, sinks_a, rel_bias_b, out_norm_a_g, out_norm_b_g, w_out, norm2_g,
           w_ff1, w_ff2, final_norm_g):
    b, s, d = x.shape
    assert d == D_MODEL and s % TM_PROJ == 0 and s % TM_FFN == 0 and s % TQ == 0
    assert norm1_g.shape[0] == 1, "single-layer block"
    scale = HEAD_DIM ** -0.5 * LOG2E

    w = w_in[0]
    o_ka, o_va, o_qb, o_kb, o_vb = QA_W, QA_W + KA_W, QA_W + 2 * KA_W, QA_W + 2 * KA_W + QB_W, QA_W + 2 * KA_W + 2 * QB_W
    wk = w[:, o_kb:o_vb].astype(BF16)
    wt = jnp.concatenate([w[:, :QA_W] * scale, w[:, o_qb:o_kb] * scale, w[:, o_vb:], w[:, o_ka:o_qb]],
                         axis=1).T.astype(BF16)
    kb, qat, qbt, vbt, kvat, wo, w1, w2 = _proj_call(
        x, norm1_g[0].reshape(1, d), wk, wt, w_out[0], w_ff1[0], w_ff2[0])

    bias_a, sink_a = _bias_a_table(sinks_a[0])
    bias_b = _bias_b_table(rel_bias_b[0])
    ga = jnp.broadcast_to(out_norm_a_g[0].astype(F32)[:, None], (QA_W, TQ))
    gb = jnp.broadcast_to(out_norm_b_g[0].astype(F32)[:, None], (QB_W, TQ))
    yt = _attn_call(kb, qat, qbt, vbt, kvat, bias_a, sink_a, bias_b, ga, gb)

    return _ffn_call(yt, x, wo, norm2_g[0].reshape(1, d), w1, w2, final_norm_g.reshape(1, d))
```

```python
import functools

import jax
import jax.numpy as jnp
import numpy as np
from jax import lax
from jax.experimental import pallas as pl
from jax.experimental.pallas import tpu as pltpu

D_MODEL = 1024
CHUNK = 64
HEAD_DIM = 64
A_HEADS = 8
A_KV_HEADS = 2
A_GROUP = A_HEADS // A_KV_HEADS
A_BAND_CHUNKS = 3
B_HEADS = 8
B_BAND_CHUNKS = 9
B_MAX_REL = 128
D_FF = 4 * D_MODEL
EPS = 1e-6
NEG_INF = -1e30
LOG2E = 1.4426950408889634

QA_W = A_HEADS * HEAD_DIM
KA_W = A_KV_HEADS * HEAD_DIM
QB_W = B_HEADS * HEAD_DIM
MIX_W = QA_W + QB_W

LANES = 128
TQ = 256
A_PAIR = 2 * CHUNK
A_WIN = 4 * CHUNK
B_BLOCKS = 3
B_WIN = B_BLOCKS * TQ
def _b_live_tiles():
    chunks_per_tile = A_PAIR // CHUNK
    live = []
    for p in range(B_BLOCKS):
        tiles = []
        for rt in range(TQ // A_PAIR):
            for lt in range(TQ // LANES):
                kcs = [p * (TQ // CHUNK) + rt * chunks_per_tile + i for i in range(chunks_per_tile)]
                qcs = [lt * (LANES // CHUNK) + i for i in range(LANES // CHUNK)]
                if any(qc <= kc <= qc + B_BAND_CHUNKS - 1 for kc in kcs for qc in qcs):
                    tiles.append((rt, lt))
        live.append(tuple(tiles))
    return tuple(live)


B_LIVE_TILES = _b_live_tiles()
ONES_ROWS = 16
PIPE_DEPTH = 7
TM_PROJ = 1024
PROJ_SUB = 512
TM_FFN = 1024
FF_CHUNK = 1024
VMEM_LIMIT = 56 * 1024 * 1024

F32 = jnp.float32
BF16 = jnp.bfloat16


def _rms(x, g):
    ms = jnp.mean(x * x, axis=-1, keepdims=True)
    return x * lax.rsqrt(ms + EPS) * g


def _rms_rows(xt, g):
    ms = jnp.mean(xt * xt, axis=0, keepdims=True)
    return xt * lax.rsqrt(ms + EPS) * g


def _dot(a, b):
    return jnp.dot(a, b, preferred_element_type=F32)


def _dot_nt(a, b):
    return lax.dot_general(a, b, (((1,), (1,)), ((), ())), preferred_element_type=F32)


def _dot_tn(a, b):
    return lax.dot_general(a, b, (((0,), (0,)), ((), ())), preferred_element_type=F32)


def _const_spec(shape):
    nd = len(shape)
    return pl.BlockSpec(shape, lambda *_: (0,) * nd, pipeline_mode=pl.Buffered(1))


def _proj_kernel(x_ref, g_ref, wk_ref, wt_ref, wo_ref, w1_ref, w2_ref,
                 kb_ref, qat_ref, qbt_ref, vbt_ref, kvat_ref, wo_bf_ref, w1_bf_ref, w2_bf_ref):
    for i in range(TM_PROJ // PROJ_SUB):
        tok = slice(i * PROJ_SUB, (i + 1) * PROJ_SUB)
        n = _rms(x_ref[0, tok, :], g_ref[...]).astype(BF16)
        kb_ref[0, tok, :] = _dot(n, wk_ref[...]).astype(BF16)
        t = _dot_nt(wt_ref[...], n).astype(BF16)
        qat_ref[0, :, tok] = t[:QA_W]
        qbt_ref[0, :, tok] = t[QA_W:QA_W + QB_W]
        vbt_ref[0, :, tok] = t[QA_W + QB_W:QA_W + 2 * QB_W]
        kvat_ref[0, :, tok] = t[QA_W + 2 * QB_W:]
    wo_bf_ref[...] = wo_ref[...].astype(BF16)
    w1_bf_ref[...] = w1_ref[...].astype(BF16)
    w2_bf_ref[...] = w2_ref[...].astype(BF16)


def _proj_call(x, g, wk, wt, wo, w1, w2):
    b, s, _ = x.shape
    steps = b * (s // TM_PROJ)

    def tok(width):
        return pl.BlockSpec((1, TM_PROJ, width), lambda bi, i: (bi, i, 0))

    def feat(width):
        return pl.BlockSpec((1, width, TM_PROJ), lambda bi, i: (bi, 0, i))

    def row_share(w):
        rows = w.shape[0] // steps
        assert rows * steps == w.shape[0] and rows % 16 == 0, "weight rows must split into bf16 row tiles"
        return pl.BlockSpec((rows, w.shape[1]), lambda bi, i: (bi * (s // TM_PROJ) + i, 0))

    def bf(w):
        return jax.ShapeDtypeStruct(w.shape, BF16)

    return pl.pallas_call(
        _proj_kernel,
        grid=(b, s // TM_PROJ),
        in_specs=[tok(D_MODEL), _const_spec(g.shape), _const_spec(wk.shape), _const_spec(wt.shape),
                  row_share(wo), row_share(w1), row_share(w2)],
        out_specs=[tok(QB_W), feat(QA_W), feat(QB_W), feat(QB_W), feat(2 * KA_W),
                   row_share(wo), row_share(w1), row_share(w2)],
        out_shape=[
            jax.ShapeDtypeStruct((b, s, QB_W), BF16),
            jax.ShapeDtypeStruct((b, QA_W, s), BF16), jax.ShapeDtypeStruct((b, QB_W, s), BF16),
            jax.ShapeDtypeStruct((b, QB_W, s), BF16), jax.ShapeDtypeStruct((b, 2 * KA_W, s), BF16),
            bf(wo), bf(w1), bf(w2),
        ],
        compiler_params=pltpu.CompilerParams(
            dimension_semantics=("arbitrary", "arbitrary"), vmem_limit_bytes=VMEM_LIMIT),
        name="norm_in_proj",
    )(x, g, wk, wt, wo, w1, w2)


def _attn_kernel(qat_ref, qbt_ref, kb0_ref, kb1_ref, kb2_ref, vbt0_ref, vbt1_ref, vbt2_ref,
                 kvap_ref, kvac_ref,
                 bias_a_ref, sink_ref, bias_b_ref, ga_ref, gb_ref,
                 y_ref, ya_scr, yb_scr):
    j = pl.program_id(1)
    zeros_q = jnp.zeros((HEAD_DIM, TQ), BF16)
    ones_k = jnp.ones((ONES_ROWS, TQ), BF16)
    zeros_p = jnp.zeros((HEAD_DIM, A_PAIR), BF16)
    kb_refs = (kb0_ref, kb1_ref, kb2_ref)
    vbt_refs = (vbt0_ref, vbt1_ref, vbt2_ref)
    pen_a = jnp.where(j >= 1, 0.0, NEG_INF).astype(F32)
    before_start = [jnp.where(j - (B_BLOCKS - 1 - p) < 0, jnp.ones((TQ, LANES), BF16), jnp.zeros((TQ, LANES), BF16))
                    for p in range(B_BLOCKS - 1)]
    pen_rows = jnp.where(lax.broadcasted_iota(jnp.int32, (LANES, TQ), 0) == 0, NEG_INF, 0.0).astype(BF16)


    b_parts = {}

    def b_scores(h, p):
        hp, half = divmod(h, 2)
        qt = qbt_ref[0, h * HEAD_DIM:(h + 1) * HEAD_DIM, :]
        qm = jnp.concatenate([qt, zeros_q] if half == 0 else [zeros_q, qt], axis=0)
        kp = kb_refs[p][0, :, hp * LANES:(hp + 1) * LANES]
        if p < B_BLOCKS - 1:
            kp = jnp.concatenate([kp, before_start[p]], axis=1)
            qm = jnp.concatenate([qm, pen_rows], axis=0)
        raw = _dot(kp, qm)
        tiles = {}
        for rt, lt in B_LIVE_TILES[p]:
            ks = slice(rt * A_PAIR, (rt + 1) * A_PAIR)
            qs = slice(lt * LANES, (lt + 1) * LANES)
            tiles[rt, lt] = raw[ks, qs] + bias_b_ref[h, p * TQ + rt * A_PAIR:p * TQ + (rt + 1) * A_PAIR, qs]
        mcols = []
        for lt in range(TQ // LANES):
            col = functools.reduce(jnp.maximum, [t for (_, l2), t in tiles.items() if l2 == lt])
            mcols.append(jnp.max(col, axis=0, keepdims=True))
        return tiles, mcols

    def b_output(h, p, tiles, mcols):
        rows = slice(h * HEAD_DIM, (h + 1) * HEAD_DIM)
        dead = jnp.zeros((A_PAIR, LANES), BF16)
        pt = jnp.concatenate([
            jnp.concatenate([jnp.exp2(tiles[rt, lt] - mcols[lt]).astype(BF16) if (rt, lt) in tiles else dead
                             for rt in range(TQ // A_PAIR)], axis=0)
            for lt in range(TQ // LANES)], axis=1)
        mp = jnp.concatenate(mcols, axis=1)
        vt = jnp.concatenate([vbt_refs[p][0, rows, :], ones_k], axis=0)
        b_parts.setdefault(h, []).append((_dot(vt, pt)[:HEAD_DIM + 8], mp))
        if p == B_BLOCKS - 1:
            parts = b_parts.pop(h)
            m = functools.reduce(jnp.maximum, [mq for _, mq in parts])
            ot = sum(op * jnp.exp2(mq - m) for op, mq in parts)
            yb_scr[rows, :] = ot[:HEAD_DIM] * (1.0 / ot[HEAD_DIM:HEAD_DIM + 1])

    def a_windows(r):
        if r == 0:
            kvwin = jnp.concatenate([kvap_ref[0], kvac_ref[0, :, :A_PAIR]], axis=1)
        else:
            kvwin = kvac_ref[0]
        return kvwin[:KA_W], kvwin[KA_W:]

    def a_scores(r, kvh):
        tok = slice(r * A_PAIR, (r + 1) * A_PAIR)
        kwin, _ = a_windows(r)
        blocks = []
        for g in range(A_GROUP):
            h = kvh * A_GROUP + g
            qt = qat_ref[0, h * HEAD_DIM:(h + 1) * HEAD_DIM, tok]
            blocks.append(jnp.concatenate([qt, zeros_p] if kvh == 0 else [zeros_p, qt], axis=0))
        qst = jnp.concatenate(blocks, axis=1)
        st = _dot_tn(kwin, qst) + bias_a_ref[kvh]
        s0 = st[:A_PAIR]
        s1 = st[A_PAIR:]
        if r == 0:
            s0 = s0 + pen_a
        m = jnp.maximum(jnp.max(jnp.maximum(s0, s1), axis=0, keepdims=True), sink_ref[kvh])
        return (s0, s1), m

    def a_output(r, kvh, st, m):
        tok = slice(r * A_PAIR, (r + 1) * A_PAIR)
        _, vwin = a_windows(r)
        pt = jnp.concatenate([jnp.exp2(st[0] - m), jnp.exp2(st[1] - m)], axis=0).astype(BF16)
        vt = jnp.concatenate([vwin[kvh * HEAD_DIM:(kvh + 1) * HEAD_DIM, :], ones_k], axis=0)
        ot = _dot(vt, pt)
        den = ot[HEAD_DIM:HEAD_DIM + 1] + jnp.exp2(sink_ref[kvh] - m)
        yt = ot[:HEAD_DIM] * (1.0 / den)
        for g in range(A_GROUP):
            h = kvh * A_GROUP + g
            ya_scr[h * HEAD_DIM:(h + 1) * HEAD_DIM, tok] = yt[:, g * A_PAIR:(g + 1) * A_PAIR]

    units = [(b_scores, b_output, (h, p)) for h in range(B_HEADS) for p in range(B_BLOCKS)]
    units += [(a_scores, a_output, (r, kvh)) for r in range(TQ // A_PAIR) for kvh in range(A_KV_HEADS)]
    pending = []
    for scores, output, args in units:
        pending.append((output, args, scores(*args)))
        if len(pending) > PIPE_DEPTH:
            output, args, staged = pending.pop(0)
            output(*args, *staged)
    for output, args, staged in pending:
        output(*args, *staged)

    y_ref[0, QA_W:, :] = _rms_rows(yb_scr[...], gb_ref[...]).astype(BF16)
    y_ref[0, :QA_W, :] = _rms_rows(ya_scr[...], ga_ref[...]).astype(BF16)


def _attn_call(kb, qat, qbt, vbt, kvat, bias_a, sink_a, bias_b, ga, gb):
    b, s, _ = kb.shape
    nq = s // TQ

    def feat_cur(width):
        return pl.BlockSpec((1, width, TQ), lambda bi, j: (bi, 0, j))

    def tok_back(back):
        return pl.BlockSpec((1, TQ, QB_W), lambda bi, j: (bi, jnp.maximum(j - back, 0), 0))

    def feat_back(back):
        return pl.BlockSpec((1, QB_W, TQ), lambda bi, j: (bi, 0, jnp.maximum(j - back, 0)))

    in_specs = [
        feat_cur(QA_W), feat_cur(QB_W),
        tok_back(2), tok_back(1), tok_back(0),
        feat_back(2), feat_back(1), feat_back(0),
        pl.BlockSpec((1, 2 * KA_W, A_PAIR), lambda bi, j: (bi, 0, jnp.maximum(2 * j - 1, 0))),
        pl.BlockSpec((1, 2 * KA_W, TQ), lambda bi, j: (bi, 0, j)),
        _const_spec(bias_a.shape), _const_spec(sink_a.shape), _const_spec(bias_b.shape),
        _const_spec(ga.shape), _const_spec(gb.shape),
    ]
    return pl.pallas_call(
        _attn_kernel,
        grid=(b, nq),
        in_specs=in_specs,
        out_specs=pl.BlockSpec((1, MIX_W, TQ), lambda bi, j: (bi, 0, j)),
        out_shape=jax.ShapeDtypeStruct((b, MIX_W, s), BF16),
        scratch_shapes=[pltpu.VMEM((QA_W, TQ), F32), pltpu.VMEM((QB_W, TQ), F32)],
        compiler_params=pltpu.CompilerParams(
            dimension_semantics=("arbitrary", "arbitrary"), vmem_limit_bytes=VMEM_LIMIT),
        name="attention",
    )(qat, qbt, kb, kb, kb, vbt, vbt, vbt, kvat, kvat, bias_a, sink_a, bias_b, ga, gb)


def _ffn_kernel(yt_ref, x_ref, wo_ref, g2_ref, w1_ref, w2_ref, gf_ref, o_ref, h_scr, n2_scr):
    h = x_ref[0] + _dot_tn(yt_ref[0], wo_ref[...])
    h_scr[...] = h
    n2_scr[...] = _rms(h, g2_ref[...]).astype(BF16)

    for c in range(D_FF // FF_CHUNK):
        cols = slice(c * FF_CHUNK, (c + 1) * FF_CHUNK)
        u = _dot(n2_scr[...], w1_ref[:, cols])
        u = jnp.square(jnp.maximum(u, 0.0)).astype(BF16)
        h_scr[...] += _dot(u, w2_ref[cols, :])
    o_ref[0] = _rms(h_scr[...], gf_ref[...])


def _ffn_call(yt, x, wo, g2, w1, w2, gf):
    b, s, _ = x.shape
    return pl.pallas_call(
        _ffn_kernel,
        grid=(b, s // TM_FFN),
        in_specs=[
            pl.BlockSpec((1, MIX_W, TM_FFN), lambda bi, i: (bi, 0, i)),
            pl.BlockSpec((1, TM_FFN, D_MODEL), lambda bi, i: (bi, i, 0)),
            _const_spec(wo.shape), _const_spec(g2.shape),
            _const_spec(w1.shape), _const_spec(w2.shape), _const_spec(gf.shape),
        ],
        out_specs=pl.BlockSpec((1, TM_FFN, D_MODEL), lambda bi, i: (bi, i, 0)),
        out_shape=jax.ShapeDtypeStruct((b, s, D_MODEL), F32),
        scratch_shapes=[pltpu.VMEM((TM_FFN, D_MODEL), F32), pltpu.VMEM((TM_FFN, D_MODEL), BF16)],
        compiler_params=pltpu.CompilerParams(
            dimension_semantics=("arbitrary", "arbitrary"), vmem_limit_bytes=VMEM_LIMIT),
        name="out_proj_mlp",
    )(yt, x, wo, g2, w1, w2, gf)


def _bias_a_table(sinks):
    k = np.arange(A_WIN)[:, None]
    i = np.arange(A_PAIR)[None, :]
    dist = np.abs(A_PAIR + i - k).astype(np.float32)
    qc = i // CHUNK
    kc = k // CHUNK
    allowed = (kc >= qc) & (kc <= qc + A_BAND_CHUNKS - 1)
    slopes = jnp.exp2(-8.0 * (jnp.arange(A_HEADS, dtype=F32) + 1.0) / A_HEADS)
    bias = -slopes[:, None, None] * jnp.asarray(dist)[None] * LOG2E
    bias = jnp.where(jnp.asarray(allowed)[None], bias, NEG_INF)
    bias = bias.reshape(A_KV_HEADS, A_GROUP, A_WIN, A_PAIR).transpose(0, 2, 1, 3)
    bias = bias.reshape(A_KV_HEADS, A_WIN, A_GROUP * A_PAIR)
    sink = jnp.broadcast_to((sinks.astype(F32) * LOG2E).reshape(A_KV_HEADS, 1, A_GROUP, 1),
                            (A_KV_HEADS, 1, A_GROUP, A_PAIR)).reshape(A_KV_HEADS, 1, A_GROUP * A_PAIR)
    return bias, sink


def _bias_b_table(rel_bias):
    period = TQ + B_WIN
    d = np.arange(period)
    d = np.where(d >= B_WIN, d - period, d)
    dist = (B_BAND_CHUNKS - 1) * CHUNK - d
    rel = np.clip(dist, -B_MAX_REL, B_MAX_REL) + B_MAX_REL
    row = (rel_bias.astype(F32) * LOG2E)[:, jnp.asarray(rel)]
    flat = jnp.tile(row, (1, TQ))[:, :TQ * (period - 1)]
    bias = flat.reshape(B_HEADS, TQ, period - 1)[:, :, :B_WIN]
    bias = jnp.swapaxes(bias, 1, 2)
    k = np.arange(B_WIN)[:, None]
    q = np.arange(TQ)[None, :]
    qc = q // CHUNK
    kc = k // CHUNK
    allowed = (kc >= qc) & (kc <= qc + B_BAND_CHUNKS - 1)
    return jnp.where(jnp.asarray(allowed)[None], bias, NEG_INF)


def kernel(x, norm1_g, w_in, sinks_a, rel_bias_b, out_norm_a_g, out_norm_b_g, w_out, norm2_g,
           w_ff1, w_ff2, final_norm_g):
    b, s, d = x.shape
    assert d == D_MODEL and s % TM_PROJ == 0 and s % TM_FFN == 0 and s % TQ == 0
    assert norm1_g.shape[0] == 1, "single-layer block"
    scale = HEAD_DIM ** -0.5 * LOG2E

    w = w_in[0]
    o_ka, o_va, o_qb, o_kb, o_vb = QA_W, QA_W + KA_W, QA_W + 2 * KA_W, QA_W + 2 * KA_W + QB_W, QA_W + 2 * KA_W + 2 * QB_W
    wk = w[:, o_kb:o_vb].astype(BF16)
    wt = jnp.concatenate([w[:, :QA_W] * scale, w[:, o_qb:o_kb] * scale, w[:, o_vb:], w[:, o_ka:o_qb]],
                         axis=1).T.astype(BF16)
    kb, qat, qbt, vbt, kvat, wo, w1, w2 = _proj_call(
        x, norm1_g[0].reshape(1, d), wk, wt, w_out[0], w_ff1[0], w_ff2[0])

    bias_a, sink_a = _bias_a_table(sinks_a[0])
    bias_b = _bias_b_table(rel_bias_b[0])
    ga = jnp.broadcast_to(out_norm_a_g[0].astype(F32)[:, None], (QA_W, TQ))
    gb = jnp.broadcast_to(out_norm_b_g[0].astype(F32)[:, None], (QB_W, TQ))
    yt = _attn_call(kb, qat, qbt, vbt, kvat, bias_a, sink_a, bias_b, ga, gb)

    return _ffn_call(yt, x, wo, norm2_g[0].reshape(1, d), w1, w2, final_norm_g.reshape(1, d))
```

```python
import functools

import jax
import jax.numpy as jnp
import numpy as np
from jax import lax
from jax.experimental import pallas as pl
from jax.experimental.pallas import tpu as pltpu

D_MODEL = 1024
CHUNK = 64
HEAD_DIM = 64
A_HEADS = 8
A_KV_HEADS = 2
A_GROUP = A_HEADS // A_KV_HEADS
A_BAND_CHUNKS = 3
B_HEADS = 8
B_BAND_CHUNKS = 9
B_MAX_REL = 128
D_FF = 4 * D_MODEL
EPS = 1e-6
NEG_INF = -1e30
LOG2E = 1.4426950408889634

QA_W = A_HEADS * HEAD_DIM
KA_W = A_KV_HEADS * HEAD_DIM
QB_W = B_HEADS * HEAD_DIM
MIX_W = QA_W + QB_W

LANES = 128
TQ = 256
A_PAIR = 2 * CHUNK
A_WIN = 4 * CHUNK
B_BLOCKS = 3
B_WIN = B_BLOCKS * TQ
ONES_ROWS = 16
PIPE_DEPTH = 7
TM_PROJ = 1024
PROJ_SUB = 512
TM_FFN = 1024
FF_CHUNK = 1024
V7X_VMEM_BYTES = 64 * 1024 * 1024
VMEM_LIMIT = V7X_VMEM_BYTES * 7 // 8

F32 = jnp.float32
BF16 = jnp.bfloat16


def _b_live_tiles():
    chunks_per_tile = A_PAIR // CHUNK
    live = []
    for p in range(B_BLOCKS):
        tiles = []
        for rt in range(TQ // A_PAIR):
            for lt in range(TQ // LANES):
                kcs = [p * (TQ // CHUNK) + rt * chunks_per_tile + i for i in range(chunks_per_tile)]
                qcs = [lt * (LANES // CHUNK) + i for i in range(LANES // CHUNK)]
                if any(qc <= kc <= qc + B_BAND_CHUNKS - 1 for kc in kcs for qc in qcs):
                    tiles.append((rt, lt))
        live.append(tuple(tiles))
    return tuple(live)


B_LIVE_TILES = _b_live_tiles()


def _rms(x, g):
    ms = jnp.mean(x * x, axis=-1, keepdims=True)
    return x * lax.rsqrt(ms + EPS) * g


def _rms_rows(xt, g):
    ms = jnp.mean(xt * xt, axis=0, keepdims=True)
    return xt * lax.rsqrt(ms + EPS) * g


def _dot(a, b):
    return jnp.dot(a, b, preferred_element_type=F32)


def _dot_nt(a, b):
    return lax.dot_general(a, b, (((1,), (1,)), ((), ())), preferred_element_type=F32)


def _dot_tn(a, b):
    return lax.dot_general(a, b, (((0,), (0,)), ((), ())), preferred_element_type=F32)


def _const_spec(shape):
    nd = len(shape)
    return pl.BlockSpec(shape, lambda *_: (0,) * nd, pipeline_mode=pl.Buffered(1))


def _proj_kernel(x_ref, g_ref, win_ref, wo_ref, w1_ref, w2_ref,
                 kb_ref, qat_ref, qbt_ref, vbt_ref, kvat_ref, wo_bf_ref, w1_bf_ref, w2_bf_ref,
                 wk_scr, wt_scr):
    @pl.when((pl.program_id(0) == 0) & (pl.program_id(1) == 0))
    def _prepare_weights():
        o_ka, o_qb, o_kb, o_vb = QA_W, QA_W + 2 * KA_W, QA_W + 2 * KA_W + QB_W, QA_W + 2 * KA_W + 2 * QB_W
        q_scale = HEAD_DIM ** -0.5 * LOG2E
        wk_scr[...] = win_ref[:, o_kb:o_vb].astype(BF16)
        row = 0
        for lo, hi, scale in ((0, o_ka, q_scale), (o_qb, o_kb, q_scale), (o_vb, o_vb + QB_W, None),
                              (o_ka, o_qb, None)):
            part = win_ref[:, lo:hi]
            if scale is not None:
                part = part * scale
            wt_scr[row:row + hi - lo, :] = part.T.astype(BF16)
            row += hi - lo

    for i in range(TM_PROJ // PROJ_SUB):
        tok = slice(i * PROJ_SUB, (i + 1) * PROJ_SUB)
        n = _rms(x_ref[0, tok, :], g_ref[...]).astype(BF16)
        kb_ref[0, tok, :] = _dot(n, wk_scr[...]).astype(BF16)
        t = _dot_nt(wt_scr[...], n).astype(BF16)
        qat_ref[0, :, tok] = t[:QA_W]
        qbt_ref[0, :, tok] = t[QA_W:QA_W + QB_W]
        vbt_ref[0, :, tok] = t[QA_W + QB_W:QA_W + 2 * QB_W]
        kvat_ref[0, :, tok] = t[QA_W + 2 * QB_W:]
    wo_bf_ref[...] = wo_ref[...].astype(BF16)
    w1_bf_ref[...] = w1_ref[...].astype(BF16)
    w2_bf_ref[...] = w2_ref[...].astype(BF16)


def _proj_call(x, g, w_in, wo, w1, w2):
    b, s, _ = x.shape
    steps = b * (s // TM_PROJ)

    def tok(width):
        return pl.BlockSpec((1, TM_PROJ, width), lambda bi, i: (bi, i, 0))

    def feat(width):
        return pl.BlockSpec((1, width, TM_PROJ), lambda bi, i: (bi, 0, i))

    def row_share(w):
        rows = w.shape[0] // steps
        assert rows * steps == w.shape[0] and rows % 16 == 0, "weight rows must split into bf16 row tiles"
        return pl.BlockSpec((rows, w.shape[1]), lambda bi, i: (bi * (s // TM_PROJ) + i, 0))

    def bf(w):
        return jax.ShapeDtypeStruct(w.shape, BF16)

    return pl.pallas_call(
        _proj_kernel,
        grid=(b, s // TM_PROJ),
        in_specs=[tok(D_MODEL), _const_spec(g.shape), _const_spec(w_in.shape),
                  row_share(wo), row_share(w1), row_share(w2)],
        out_specs=[tok(QB_W), feat(QA_W), feat(QB_W), feat(QB_W), feat(2 * KA_W),
                   row_share(wo), row_share(w1), row_share(w2)],
        out_shape=[
            jax.ShapeDtypeStruct((b, s, QB_W), BF16),
            jax.ShapeDtypeStruct((b, QA_W, s), BF16), jax.ShapeDtypeStruct((b, QB_W, s), BF16),
            jax.ShapeDtypeStruct((b, QB_W, s), BF16), jax.ShapeDtypeStruct((b, 2 * KA_W, s), BF16),
            bf(wo), bf(w1), bf(w2),
        ],
        scratch_shapes=[pltpu.VMEM((D_MODEL, QB_W), BF16),
                        pltpu.VMEM((w_in.shape[1] - QB_W, D_MODEL), BF16)],
        compiler_params=pltpu.CompilerParams(
            dimension_semantics=("arbitrary", "arbitrary"), vmem_limit_bytes=VMEM_LIMIT),
        name="norm_in_proj",
    )(x, g, w_in, wo, w1, w2)


def _attn_kernel(qat_ref, qbt_ref, kb0_ref, kb1_ref, kb2_ref, vbt0_ref, vbt1_ref, vbt2_ref,
                 kvap_ref, kvac_ref,
                 bias_a_ref, sink_ref, bias_b_ref, ga_ref, gb_ref,
                 y_ref, ya_scr, yb_scr):
    j = pl.program_id(1)
    zeros_q = jnp.zeros((HEAD_DIM, TQ), BF16)
    ones_k = jnp.ones((ONES_ROWS, TQ), BF16)
    zeros_p = jnp.zeros((HEAD_DIM, A_PAIR), BF16)
    kb_refs = (kb0_ref, kb1_ref, kb2_ref)
    vbt_refs = (vbt0_ref, vbt1_ref, vbt2_ref)
    pen_a = jnp.where(j >= 1, 0.0, NEG_INF).astype(F32)
    before_start = [jnp.where(j - (B_BLOCKS - 1 - p) < 0, jnp.ones((TQ, LANES), BF16), jnp.zeros((TQ, LANES), BF16))
                    for p in range(B_BLOCKS - 1)]
    pen_rows = jnp.where(lax.broadcasted_iota(jnp.int32, (LANES, TQ), 0) == 0, NEG_INF, 0.0).astype(BF16)


    b_parts = {}

    def b_scores(h, p):
        hp, half = divmod(h, 2)
        qt = qbt_ref[0, h * HEAD_DIM:(h + 1) * HEAD_DIM, :]
        qm = jnp.concatenate([qt, zeros_q] if half == 0 else [zeros_q, qt], axis=0)
        kp = kb_refs[p][0, :, hp * LANES:(hp + 1) * LANES]
        if p < B_BLOCKS - 1:
            kp = jnp.concatenate([kp, before_start[p]], axis=1)
            qm = jnp.concatenate([qm, pen_rows], axis=0)
        raw = _dot(kp, qm)
        tiles = {}
        for rt, lt in B_LIVE_TILES[p]:
            ks = slice(rt * A_PAIR, (rt + 1) * A_PAIR)
            qs = slice(lt * LANES, (lt + 1) * LANES)
            tiles[rt, lt] = raw[ks, qs] + bias_b_ref[h, p * TQ + rt * A_PAIR:p * TQ + (rt + 1) * A_PAIR, qs]
        mcols = []
        for lt in range(TQ // LANES):
            col = functools.reduce(jnp.maximum, [t for (_, l2), t in tiles.items() if l2 == lt])
            mcols.append(jnp.max(col, axis=0, keepdims=True))
        return tiles, mcols

    def b_output(h, p, tiles, mcols):
        rows = slice(h * HEAD_DIM, (h + 1) * HEAD_DIM)
        dead = jnp.zeros((A_PAIR, LANES), BF16)
        pt = jnp.concatenate([
            jnp.concatenate([jnp.exp2(tiles[rt, lt] - mcols[lt]).astype(BF16) if (rt, lt) in tiles else dead
                             for rt in range(TQ // A_PAIR)], axis=0)
            for lt in range(TQ // LANES)], axis=1)
        mp = jnp.concatenate(mcols, axis=1)
        vt = jnp.concatenate([vbt_refs[p][0, rows, :], ones_k], axis=0)
        b_parts.setdefault(h, []).append((_dot(vt, pt)[:HEAD_DIM + 8], mp))
        if p == B_BLOCKS - 1:
            parts = b_parts.pop(h)
            m = functools.reduce(jnp.maximum, [mq for _, mq in parts])
            ot = sum(op * jnp.exp2(mq - m) for op, mq in parts)
            yb_scr[rows, :] = ot[:HEAD_DIM] * (1.0 / ot[HEAD_DIM:HEAD_DIM + 1])

    def a_windows(r):
        if r == 0:
            kvwin = jnp.concatenate([kvap_ref[0], kvac_ref[0, :, :A_PAIR]], axis=1)
        else:
            kvwin = kvac_ref[0]
        return kvwin[:KA_W], kvwin[KA_W:]

    def a_scores(r, kvh):
        tok = slice(r * A_PAIR, (r + 1) * A_PAIR)
        kwin, _ = a_windows(r)
        blocks = []
        for g in range(A_GROUP):
            h = kvh * A_GROUP + g
            qt = qat_ref[0, h * HEAD_DIM:(h + 1) * HEAD_DIM, tok]
            blocks.append(jnp.concatenate([qt, zeros_p] if kvh == 0 else [zeros_p, qt], axis=0))
        qst = jnp.concatenate(blocks, axis=1)
        st = _dot_tn(kwin, qst) + bias_a_ref[kvh]
        s0 = st[:A_PAIR]
        s1 = st[A_PAIR:]
        if r == 0:
            s0 = s0 + pen_a
        m = jnp.maximum(jnp.max(jnp.maximum(s0, s1), axis=0, keepdims=True), sink_ref[kvh])
        return (s0, s1), m

    def a_output(r, kvh, st, m):
        tok = slice(r * A_PAIR, (r + 1) * A_PAIR)
        _, vwin = a_windows(r)
        pt = jnp.concatenate([jnp.exp2(st[0] - m), jnp.exp2(st[1] - m)], axis=0).astype(BF16)
        vt = jnp.concatenate([vwin[kvh * HEAD_DIM:(kvh + 1) * HEAD_DIM, :], ones_k], axis=0)
        ot = _dot(vt, pt)
        den = ot[HEAD_DIM:HEAD_DIM + 1] + jnp.exp2(sink_ref[kvh] - m)
        yt = ot[:HEAD_DIM] * (1.0 / den)
        for g in range(A_GROUP):
            h = kvh * A_GROUP + g
            ya_scr[h * HEAD_DIM:(h + 1) * HEAD_DIM, tok] = yt[:, g * A_PAIR:(g + 1) * A_PAIR]

    units = [(b_scores, b_output, (h, p)) for h in range(B_HEADS) for p in range(B_BLOCKS)]
    units += [(a_scores, a_output, (r, kvh)) for r in range(TQ // A_PAIR) for kvh in range(A_KV_HEADS)]
    pending = []
    for scores, output, args in units:
        pending.append((output, args, scores(*args)))
        if len(pending) > PIPE_DEPTH:
            output, args, staged = pending.pop(0)
            output(*args, *staged)
    for output, args, staged in pending:
        output(*args, *staged)

    y_ref[0, QA_W:, :] = _rms_rows(yb_scr[...], gb_ref[...]).astype(BF16)
    y_ref[0, :QA_W, :] = _rms_rows(ya_scr[...], ga_ref[...]).astype(BF16)


def _attn_call(kb, qat, qbt, vbt, kvat, bias_a, sink_a, bias_b, ga, gb):
    b, s, _ = kb.shape
    nq = s // TQ

    def feat_cur(width):
        return pl.BlockSpec((1, width, TQ), lambda bi, j: (bi, 0, j))

    def tok_back(back):
        return pl.BlockSpec((1, TQ, QB_W), lambda bi, j: (bi, jnp.maximum(j - back, 0), 0))

    def feat_back(back):
        return pl.BlockSpec((1, QB_W, TQ), lambda bi, j: (bi, 0, jnp.maximum(j - back, 0)))

    in_specs = [
        feat_cur(QA_W), feat_cur(QB_W),
        tok_back(2), tok_back(1), tok_back(0),
        feat_back(2), feat_back(1), feat_back(0),
        pl.BlockSpec((1, 2 * KA_W, A_PAIR), lambda bi, j: (bi, 0, jnp.maximum(2 * j - 1, 0))),
        pl.BlockSpec((1, 2 * KA_W, TQ), lambda bi, j: (bi, 0, j)),
        _const_spec(bias_a.shape), _const_spec(sink_a.shape), _const_spec(bias_b.shape),
        _const_spec(ga.shape), _const_spec(gb.shape),
    ]
    return pl.pallas_call(
        _attn_kernel,
        grid=(b, nq),
        in_specs=in_specs,
        out_specs=pl.BlockSpec((1, MIX_W, TQ), lambda bi, j: (bi, 0, j)),
        out_shape=jax.ShapeDtypeStruct((b, MIX_W, s), BF16),
        scratch_shapes=[pltpu.VMEM((QA_W, TQ), F32), pltpu.VMEM((QB_W, TQ), F32)],
        compiler_params=pltpu.CompilerParams(
            dimension_semantics=("arbitrary", "arbitrary"), vmem_limit_bytes=VMEM_LIMIT),
        name="attention",
    )(qat, qbt, kb, kb, kb, vbt, vbt, vbt, kvat, kvat, bias_a, sink_a, bias_b, ga, gb)


def _ffn_kernel(yt_ref, x_ref, wo_ref, g2_ref, w1_ref, w2_ref, gf_ref, o_ref, h_scr, n2_scr):
    h = x_ref[0] + _dot_tn(yt_ref[0], wo_ref[...])
    h_scr[...] = h
    n2_scr[...] = _rms(h, g2_ref[...]).astype(BF16)

    for c in range(D_FF // FF_CHUNK):
        cols = slice(c * FF_CHUNK, (c + 1) * FF_CHUNK)
        u = _dot(n2_scr[...], w1_ref[:, cols])
        u = jnp.square(jnp.maximum(u, 0.0)).astype(BF16)
        h_scr[...] += _dot(u, w2_ref[cols, :])
    o_ref[0] = _rms(h_scr[...], gf_ref[...])


def _ffn_call(yt, x, wo, g2, w1, w2, gf):
    b, s, _ = x.shape
    return pl.pallas_call(
        _ffn_kernel,
        grid=(b, s // TM_FFN),
        in_specs=[
            pl.BlockSpec((1, MIX_W, TM_FFN), lambda bi, i: (bi, 0, i)),
            pl.BlockSpec((1, TM_FFN, D_MODEL), lambda bi, i: (bi, i, 0)),
            _const_spec(wo.shape), _const_spec(g2.shape),
            _const_spec(w1.shape), _const_spec(w2.shape), _const_spec(gf.shape),
        ],
        out_specs=pl.BlockSpec((1, TM_FFN, D_MODEL), lambda bi, i: (bi, i, 0)),
        out_shape=jax.ShapeDtypeStruct((b, s, D_MODEL), F32),
        scratch_shapes=[pltpu.VMEM((TM_FFN, D_MODEL), F32), pltpu.VMEM((TM_FFN, D_MODEL), BF16)],
        compiler_params=pltpu.CompilerParams(
            dimension_semantics=("arbitrary", "arbitrary"), vmem_limit_bytes=VMEM_LIMIT),
        name="out_proj_mlp",
    )(yt, x, wo, g2, w1, w2, gf)


def _bias_a_table(sinks):
    k = np.arange(A_WIN)[:, None]
    i = np.arange(A_PAIR)[None, :]
    dist = np.abs(A_PAIR + i - k).astype(np.float32)
    qc = i // CHUNK
    kc = k // CHUNK
    allowed = (kc >= qc) & (kc <= qc + A_BAND_CHUNKS - 1)
    slopes = jnp.exp2(-8.0 * (jnp.arange(A_HEADS, dtype=F32) + 1.0) / A_HEADS)
    bias = -slopes[:, None, None] * jnp.asarray(dist)[None] * LOG2E
    bias = jnp.where(jnp.asarray(allowed)[None], bias, NEG_INF)
    bias = bias.reshape(A_KV_HEADS, A_GROUP, A_WIN, A_PAIR).transpose(0, 2, 1, 3)
    bias = bias.reshape(A_KV_HEADS, A_WIN, A_GROUP * A_PAIR)
    sink = jnp.broadcast_to((sinks.astype(F32) * LOG2E).reshape(A_KV_HEADS, 1, A_GROUP, 1),
                            (A_KV_HEADS, 1, A_GROUP, A_PAIR)).reshape(A_KV_HEADS, 1, A_GROUP * A_PAIR)
    return bias, sink


def _bias_b_table(rel_bias):
    rb = rel_bias.astype(F32) * LOG2E
    base = (B_BAND_CHUNKS - 1) * CHUNK

    def tile(delta):
        lo = base + LANES * delta - (LANES - 1)
        if lo >= B_MAX_REL:
            return jnp.broadcast_to(rb[:, -1][:, None, None], (B_HEADS, LANES, LANES))
        off = np.arange(2 * LANES)
        off = np.where(off >= LANES, off - 2 * LANES, off)
        rel = np.clip(base + LANES * delta + off, -B_MAX_REL, B_MAX_REL) + B_MAX_REL
        row = rb[:, jnp.asarray(rel)]
        flat = jnp.tile(row, (1, LANES))[:, :LANES * (2 * LANES - 1)]
        return flat.reshape(B_HEADS, LANES, 2 * LANES - 1)[:, :, :LANES]

    tiles = {delta: tile(delta) for delta in range(-(B_WIN // LANES - 1), TQ // LANES)}
    bias = jnp.concatenate(
        [jnp.concatenate([tiles[t - a] for t in range(TQ // LANES)], axis=2) for a in range(B_WIN // LANES)],
        axis=1)
    k = np.arange(B_WIN)[:, None]
    q = np.arange(TQ)[None, :]
    qc = q // CHUNK
    kc = k // CHUNK
    allowed = (kc >= qc) & (kc <= qc + B_BAND_CHUNKS - 1)
    return jnp.where(jnp.asarray(allowed)[None], bias, NEG_INF)


def kernel(x, norm1_g, w_in, sinks_a, rel_bias_b, out_norm_a_g, out_norm_b_g, w_out, norm2_g,
           w_ff1, w_ff2, final_norm_g):
    b, s, d = x.shape
    assert d == D_MODEL and s % TM_PROJ == 0 and s % TM_FFN == 0 and s % TQ == 0
    assert norm1_g.shape[0] == 1, "single-layer block"
    assert w_in.shape[2] == 2 * QA_W + 2 * KA_W + 2 * QB_W
    kb, qat, qbt, vbt, kvat, wo, w1, w2 = _proj_call(
        x, norm1_g[0].reshape(1, d), w_in[0], w_out[0], w_ff1[0], w_ff2[0])

    bias_a, sink_a = _bias_a_table(sinks_a[0])
    bias_b = _bias_b_table(rel_bias_b[0])
    ga = jnp.broadcast_to(out_norm_a_g[0].astype(F32)[:, None], (QA_W, TQ))
    gb = jnp.broadcast_to(out_norm_b_g[0].astype(F32)[:, None], (QB_W, TQ))
    yt = _attn_call(kb, qat, qbt, vbt, kvat, bias_a, sink_a, bias_b, ga, gb)

    return _ffn_call(yt, x, wo, norm2_g[0].reshape(1, d), w1, w2, final_norm_g.reshape(1, d))
```

```python
import functools

import jax
import jax.numpy as jnp
import numpy as np
from jax import lax
from jax.experimental import pallas as pl
from jax.experimental.pallas import tpu as pltpu

D_MODEL = 1024
CHUNK = 64
HEAD_DIM = 64
A_HEADS = 8
A_KV_HEADS = 2
A_GROUP = A_HEADS // A_KV_HEADS
A_BAND_CHUNKS = 3
B_HEADS = 8
B_BAND_CHUNKS = 9
B_MAX_REL = 128
D_FF = 4 * D_MODEL
EPS = 1e-6
NEG_INF = -1e30
LOG2E = 1.4426950408889634

QA_W = A_HEADS * HEAD_DIM
KA_W = A_KV_HEADS * HEAD_DIM
QB_W = B_HEADS * HEAD_DIM
MIX_W = QA_W + QB_W

LANES = 128
TQ = 256
A_PAIR = 2 * CHUNK
A_WIN = 4 * CHUNK
B_BLOCKS = 3
B_WIN = B_BLOCKS * TQ
ONES_ROWS = 16
PIPE_DEPTH = 7
TM_PROJ = 1024
PROJ_SUB = 512
TM_FFN = 1024
FF_CHUNK = 1024
V7X_VMEM_BYTES = 64 * 1024 * 1024
VMEM_LIMIT = V7X_VMEM_BYTES * 7 // 8

F32 = jnp.float32
BF16 = jnp.bfloat16


def _b_live_tiles():
    chunks_per_tile = A_PAIR // CHUNK
    live = []
    for p in range(B_BLOCKS):
        tiles = []
        for rt in range(TQ // A_PAIR):
            for lt in range(TQ // LANES):
                kcs = [p * (TQ // CHUNK) + rt * chunks_per_tile + i for i in range(chunks_per_tile)]
                qcs = [lt * (LANES // CHUNK) + i for i in range(LANES // CHUNK)]
                if any(qc <= kc <= qc + B_BAND_CHUNKS - 1 for kc in kcs for qc in qcs):
                    tiles.append((rt, lt))
        live.append(tuple(tiles))
    return tuple(live)


B_LIVE_TILES = _b_live_tiles()


def _rms(x, g):
    ms = jnp.mean(x * x, axis=-1, keepdims=True)
    return x * lax.rsqrt(ms + EPS) * g


def _rms_rows(xt, g):
    ms = jnp.mean(xt * xt, axis=0, keepdims=True)
    return xt * lax.rsqrt(ms + EPS) * g


def _dot(a, b):
    return jnp.dot(a, b, preferred_element_type=F32)


def _dot_nt(a, b):
    return lax.dot_general(a, b, (((1,), (1,)), ((), ())), preferred_element_type=F32)


def _dot_tn(a, b):
    return lax.dot_general(a, b, (((0,), (0,)), ((), ())), preferred_element_type=F32)


def _const_spec(shape):
    nd = len(shape)
    return pl.BlockSpec(shape, lambda *_: (0,) * nd, pipeline_mode=pl.Buffered(1))


def _proj_kernel(x_ref, g_ref, win_ref, kb_ref, qat_ref, qbt_ref, vbt_ref, kvat_ref, wk_scr, wt_scr):
    @pl.when((pl.program_id(0) == 0) & (pl.program_id(1) == 0))
    def _prepare_weights():
        o_ka, o_qb, o_kb, o_vb = QA_W, QA_W + 2 * KA_W, QA_W + 2 * KA_W + QB_W, QA_W + 2 * KA_W + 2 * QB_W
        q_scale = HEAD_DIM ** -0.5 * LOG2E
        wk_scr[...] = win_ref[:, o_kb:o_vb].astype(BF16)
        row = 0
        for lo, hi, scale in ((0, o_ka, q_scale), (o_qb, o_kb, q_scale), (o_vb, o_vb + QB_W, None),
                              (o_ka, o_qb, None)):
            part = win_ref[:, lo:hi]
            if scale is not None:
                part = part * scale
            wt_scr[row:row + hi - lo, :] = part.T.astype(BF16)
            row += hi - lo

    for i in range(TM_PROJ // PROJ_SUB):
        tok = slice(i * PROJ_SUB, (i + 1) * PROJ_SUB)
        n = _rms(x_ref[0, tok, :], g_ref[...]).astype(BF16)
        kb_ref[0, tok, :] = _dot(n, wk_scr[...]).astype(BF16)
        t = _dot_nt(wt_scr[...], n).astype(BF16)
        qat_ref[0, :, tok] = t[:QA_W]
        qbt_ref[0, :, tok] = t[QA_W:QA_W + QB_W]
        vbt_ref[0, :, tok] = t[QA_W + QB_W:QA_W + 2 * QB_W]
        kvat_ref[0, :, tok] = t[QA_W + 2 * QB_W:]


def _proj_call(x, g, w_in):
    b, s, _ = x.shape

    def tok(width):
        return pl.BlockSpec((1, TM_PROJ, width), lambda bi, i: (bi, i, 0))

    def feat(width):
        return pl.BlockSpec((1, width, TM_PROJ), lambda bi, i: (bi, 0, i))

    return pl.pallas_call(
        _proj_kernel,
        grid=(b, s // TM_PROJ),
        in_specs=[tok(D_MODEL), _const_spec(g.shape), _const_spec(w_in.shape)],
        out_specs=[tok(QB_W), feat(QA_W), feat(QB_W), feat(QB_W), feat(2 * KA_W)],
        out_shape=[
            jax.ShapeDtypeStruct((b, s, QB_W), BF16),
            jax.ShapeDtypeStruct((b, QA_W, s), BF16), jax.ShapeDtypeStruct((b, QB_W, s), BF16),
            jax.ShapeDtypeStruct((b, QB_W, s), BF16), jax.ShapeDtypeStruct((b, 2 * KA_W, s), BF16),
        ],
        scratch_shapes=[pltpu.VMEM((D_MODEL, QB_W), BF16),
                        pltpu.VMEM((w_in.shape[1] - QB_W, D_MODEL), BF16)],
        compiler_params=pltpu.CompilerParams(
            dimension_semantics=("arbitrary", "arbitrary"), vmem_limit_bytes=VMEM_LIMIT),
        name="norm_in_proj",
    )(x, g, w_in)


def _attn_kernel(qat_ref, qbt_ref, kb0_ref, kb1_ref, kb2_ref, vbt0_ref, vbt1_ref, vbt2_ref,
                 kvap_ref, kvac_ref,
                 bias_a_ref, sink_ref, bias_b_ref, ga_ref, gb_ref, wo_ref, w1_ref, w2_ref,
                 y_ref, wo_bf_ref, w1_bf_ref, w2_bf_ref, ya_scr, yb_scr):
    j = pl.program_id(1)
    zeros_q = jnp.zeros((HEAD_DIM, TQ), BF16)
    ones_k = jnp.ones((ONES_ROWS, TQ), BF16)
    zeros_p = jnp.zeros((HEAD_DIM, A_PAIR), BF16)
    kb_refs = (kb0_ref, kb1_ref, kb2_ref)
    vbt_refs = (vbt0_ref, vbt1_ref, vbt2_ref)
    pen_a = jnp.where(j >= 1, 0.0, NEG_INF).astype(F32)
    before_start = [jnp.where(j - (B_BLOCKS - 1 - p) < 0, jnp.ones((TQ, LANES), BF16), jnp.zeros((TQ, LANES), BF16))
                    for p in range(B_BLOCKS - 1)]
    pen_rows = jnp.where(lax.broadcasted_iota(jnp.int32, (LANES, TQ), 0) == 0, NEG_INF, 0.0).astype(BF16)


    b_parts = {}

    def b_scores(h, p):
        hp, half = divmod(h, 2)
        qt = qbt_ref[0, h * HEAD_DIM:(h + 1) * HEAD_DIM, :]
        qm = jnp.concatenate([qt, zeros_q] if half == 0 else [zeros_q, qt], axis=0)
        kp = kb_refs[p][0, :, hp * LANES:(hp + 1) * LANES]
        if p < B_BLOCKS - 1:
            kp = jnp.concatenate([kp, before_start[p]], axis=1)
            qm = jnp.concatenate([qm, pen_rows], axis=0)
        raw = _dot(kp, qm)
        tiles = {}
        for rt, lt in B_LIVE_TILES[p]:
            ks = slice(rt * A_PAIR, (rt + 1) * A_PAIR)
            qs = slice(lt * LANES, (lt + 1) * LANES)
            tiles[rt, lt] = raw[ks, qs] + bias_b_ref[h, p * TQ + rt * A_PAIR:p * TQ + (rt + 1) * A_PAIR, qs]
        mcols = []
        for lt in range(TQ // LANES):
            col = functools.reduce(jnp.maximum, [t for (_, l2), t in tiles.items() if l2 == lt])
            mcols.append(jnp.max(col, axis=0, keepdims=True))
        return tiles, mcols

    def b_output(h, p, tiles, mcols):
        rows = slice(h * HEAD_DIM, (h + 1) * HEAD_DIM)
        dead = jnp.zeros((A_PAIR, LANES), BF16)
        pt = jnp.concatenate([
            jnp.concatenate([jnp.exp2(tiles[rt, lt] - mcols[lt]).astype(BF16) if (rt, lt) in tiles else dead
                             for rt in range(TQ // A_PAIR)], axis=0)
            for lt in range(TQ // LANES)], axis=1)
        mp = jnp.concatenate(mcols, axis=1)
        vt = jnp.concatenate([vbt_refs[p][0, rows, :], ones_k], axis=0)
        b_parts.setdefault(h, []).append((_dot(vt, pt)[:HEAD_DIM + 8], mp))
        if p == B_BLOCKS - 1:
            parts = b_parts.pop(h)
            m = functools.reduce(jnp.maximum, [mq for _, mq in parts])
            ot = sum(op * jnp.exp2(mq - m) for op, mq in parts)
            yb_scr[rows, :] = ot[:HEAD_DIM] * (1.0 / ot[HEAD_DIM:HEAD_DIM + 1])

    def a_windows(r):
        if r == 0:
            kvwin = jnp.concatenate([kvap_ref[0], kvac_ref[0, :, :A_PAIR]], axis=1)
        else:
            kvwin = kvac_ref[0]
        return kvwin[:KA_W], kvwin[KA_W:]

    def a_scores(r, kvh):
        tok = slice(r * A_PAIR, (r + 1) * A_PAIR)
        kwin, _ = a_windows(r)
        blocks = []
        for g in range(A_GROUP):
            h = kvh * A_GROUP + g
            qt = qat_ref[0, h * HEAD_DIM:(h + 1) * HEAD_DIM, tok]
            blocks.append(jnp.concatenate([qt, zeros_p] if kvh == 0 else [zeros_p, qt], axis=0))
        qst = jnp.concatenate(blocks, axis=1)
        st = _dot_tn(kwin, qst) + bias_a_ref[kvh]
        s0 = st[:A_PAIR]
        s1 = st[A_PAIR:]
        if r == 0:
            s0 = s0 + pen_a
        m = jnp.maximum(jnp.max(jnp.maximum(s0, s1), axis=0, keepdims=True), sink_ref[kvh])
        return (s0, s1), m

    def a_output(r, kvh, st, m):
        tok = slice(r * A_PAIR, (r + 1) * A_PAIR)
        _, vwin = a_windows(r)
        pt = jnp.concatenate([jnp.exp2(st[0] - m), jnp.exp2(st[1] - m)], axis=0).astype(BF16)
        vt = jnp.concatenate([vwin[kvh * HEAD_DIM:(kvh + 1) * HEAD_DIM, :], ones_k], axis=0)
        ot = _dot(vt, pt)
        den = ot[HEAD_DIM:HEAD_DIM + 1] + jnp.exp2(sink_ref[kvh] - m)
        yt = ot[:HEAD_DIM] * (1.0 / den)
        for g in range(A_GROUP):
            h = kvh * A_GROUP + g
            ya_scr[h * HEAD_DIM:(h + 1) * HEAD_DIM, tok] = yt[:, g * A_PAIR:(g + 1) * A_PAIR]

    units = [(b_scores, b_output, (h, p)) for h in range(B_HEADS) for p in range(B_BLOCKS)]
    units += [(a_scores, a_output, (r, kvh)) for r in range(TQ // A_PAIR) for kvh in range(A_KV_HEADS)]
    pending = []
    for scores, output, args in units:
        pending.append((output, args, scores(*args)))
        if len(pending) > PIPE_DEPTH:
            output, args, staged = pending.pop(0)
            output(*args, *staged)
    for output, args, staged in pending:
        output(*args, *staged)

    y_ref[0, QA_W:, :] = _rms_rows(yb_scr[...], gb_ref[...]).astype(BF16)
    y_ref[0, :QA_W, :] = _rms_rows(ya_scr[...], ga_ref[...]).astype(BF16)

    wo_bf_ref[...] = wo_ref[...].astype(BF16)
    w1_bf_ref[...] = w1_ref[...].astype(BF16)
    w2_bf_ref[...] = w2_ref[...].astype(BF16)


def _step_share_spec(shape, steps, step_of):
    rows, cols = shape
    row_blk = max(rows // steps, 16)
    col_splits = steps // (rows // row_blk)
    col_blk = cols // col_splits
    assert row_blk % 16 == 0 and col_blk % LANES == 0 and (rows // row_blk) * col_splits == steps
    assert (rows // row_blk) * row_blk == rows and col_blk * col_splits == cols
    return pl.BlockSpec((row_blk, col_blk), lambda *idx: (step_of(*idx) // col_splits, step_of(*idx) % col_splits))


def _attn_call(kb, qat, qbt, vbt, kvat, bias_a, sink_a, bias_b, ga, gb, wo, w1, w2):
    b, s, _ = kb.shape
    nq = s // TQ
    cast_specs = [_step_share_spec(w.shape, b * nq, lambda bi, j: bi * nq + j) for w in (wo, w1, w2)]

    def feat_cur(width):
        return pl.BlockSpec((1, width, TQ), lambda bi, j: (bi, 0, j))

    def tok_back(back):
        return pl.BlockSpec((1, TQ, QB_W), lambda bi, j: (bi, jnp.maximum(j - back, 0), 0))

    def feat_back(back):
        return pl.BlockSpec((1, QB_W, TQ), lambda bi, j: (bi, 0, jnp.maximum(j - back, 0)))

    in_specs = [
        feat_cur(QA_W), feat_cur(QB_W),
        tok_back(2), tok_back(1), tok_back(0),
        feat_back(2), feat_back(1), feat_back(0),
        pl.BlockSpec((1, 2 * KA_W, A_PAIR), lambda bi, j: (bi, 0, jnp.maximum(2 * j - 1, 0))),
        pl.BlockSpec((1, 2 * KA_W, TQ), lambda bi, j: (bi, 0, j)),
        _const_spec(bias_a.shape), _const_spec(sink_a.shape), _const_spec(bias_b.shape),
        _const_spec(ga.shape), _const_spec(gb.shape),
    ] + cast_specs
    return pl.pallas_call(
        _attn_kernel,
        grid=(b, nq),
        in_specs=in_specs,
        out_specs=[pl.BlockSpec((1, MIX_W, TQ), lambda bi, j: (bi, 0, j))] + cast_specs,
        out_shape=[jax.ShapeDtypeStruct((b, MIX_W, s), BF16)]
        + [jax.ShapeDtypeStruct(w.shape, BF16) for w in (wo, w1, w2)],
        scratch_shapes=[pltpu.VMEM((QA_W, TQ), F32), pltpu.VMEM((QB_W, TQ), F32)],
        compiler_params=pltpu.CompilerParams(
            dimension_semantics=("arbitrary", "arbitrary"), vmem_limit_bytes=VMEM_LIMIT),
        name="attention",
    )(qat, qbt, kb, kb, kb, vbt, vbt, vbt, kvat, kvat, bias_a, sink_a, bias_b, ga, gb, wo, w1, w2)


def _ffn_kernel(yt_ref, x_ref, wo_ref, g2_ref, w1_ref, w2_ref, gf_ref, o_ref, h_scr, n2_scr):
    h = x_ref[0] + _dot_tn(yt_ref[0], wo_ref[...])
    h_scr[...] = h
    n2_scr[...] = _rms(h, g2_ref[...]).astype(BF16)

    for c in range(D_FF // FF_CHUNK):
        cols = slice(c * FF_CHUNK, (c + 1) * FF_CHUNK)
        u = _dot(n2_scr[...], w1_ref[:, cols])
        u = jnp.square(jnp.maximum(u, 0.0)).astype(BF16)
        h_scr[...] += _dot(u, w2_ref[cols, :])
    o_ref[0] = _rms(h_scr[...], gf_ref[...])


def _ffn_call(yt, x, wo, g2, w1, w2, gf):
    b, s, _ = x.shape
    return pl.pallas_call(
        _ffn_kernel,
        grid=(b, s // TM_FFN),
        in_specs=[
            pl.BlockSpec((1, MIX_W, TM_FFN), lambda bi, i: (bi, 0, i)),
            pl.BlockSpec((1, TM_FFN, D_MODEL), lambda bi, i: (bi, i, 0)),
            _const_spec(wo.shape), _const_spec(g2.shape),
            _const_spec(w1.shape), _const_spec(w2.shape), _const_spec(gf.shape),
        ],
        out_specs=pl.BlockSpec((1, TM_FFN, D_MODEL), lambda bi, i: (bi, i, 0)),
        out_shape=jax.ShapeDtypeStruct((b, s, D_MODEL), F32),
        scratch_shapes=[pltpu.VMEM((TM_FFN, D_MODEL), F32), pltpu.VMEM((TM_FFN, D_MODEL), BF16)],
        compiler_params=pltpu.CompilerParams(
            dimension_semantics=("arbitrary", "arbitrary"), vmem_limit_bytes=VMEM_LIMIT),
        name="out_proj_mlp",
    )(yt, x, wo, g2, w1, w2, gf)


def _bias_a_table(sinks):
    k = np.arange(A_WIN)[:, None]
    i = np.arange(A_PAIR)[None, :]
    dist = np.abs(A_PAIR + i - k).astype(np.float32)
    qc = i // CHUNK
    kc = k // CHUNK
    allowed = (kc >= qc) & (kc <= qc + A_BAND_CHUNKS - 1)
    slopes = jnp.exp2(-8.0 * (jnp.arange(A_HEADS, dtype=F32) + 1.0) / A_HEADS)
    bias = -slopes[:, None, None] * jnp.asarray(dist)[None] * LOG2E
    bias = jnp.where(jnp.asarray(allowed)[None], bias, NEG_INF)
    bias = bias.reshape(A_KV_HEADS, A_GROUP, A_WIN, A_PAIR).transpose(0, 2, 1, 3)
    bias = bias.reshape(A_KV_HEADS, A_WIN, A_GROUP * A_PAIR)
    sink = jnp.broadcast_to((sinks.astype(F32) * LOG2E).reshape(A_KV_HEADS, 1, A_GROUP, 1),
                            (A_KV_HEADS, 1, A_GROUP, A_PAIR)).reshape(A_KV_HEADS, 1, A_GROUP * A_PAIR)
    return bias, sink


def _bias_b_table(rel_bias):
    rb = rel_bias.astype(F32) * LOG2E
    base = (B_BAND_CHUNKS - 1) * CHUNK

    def tile(delta):
        lo = base + LANES * delta - (LANES - 1)
        if lo >= B_MAX_REL:
            return jnp.broadcast_to(rb[:, -1][:, None, None], (B_HEADS, LANES, LANES))
        off = np.arange(2 * LANES)
        off = np.where(off >= LANES, off - 2 * LANES, off)
        rel = np.clip(base + LANES * delta + off, -B_MAX_REL, B_MAX_REL) + B_MAX_REL
        row = rb[:, jnp.asarray(rel)]
        flat = jnp.tile(row, (1, LANES))[:, :LANES * (2 * LANES - 1)]
        return flat.reshape(B_HEADS, LANES, 2 * LANES - 1)[:, :, :LANES]

    tiles = {delta: tile(delta) for delta in range(-(B_WIN // LANES - 1), TQ // LANES)}
    bias = jnp.concatenate(
        [jnp.concatenate([tiles[t - a] for t in range(TQ // LANES)], axis=2) for a in range(B_WIN // LANES)],
        axis=1)
    k = np.arange(B_WIN)[:, None]
    q = np.arange(TQ)[None, :]
    qc = q // CHUNK
    kc = k // CHUNK
    allowed = (kc >= qc) & (kc <= qc + B_BAND_CHUNKS - 1)
    return jnp.where(jnp.asarray(allowed)[None], bias, NEG_INF)


def kernel(x, norm1_g, w_in, sinks_a, rel_bias_b, out_norm_a_g, out_norm_b_g, w_out, norm2_g,
           w_ff1, w_ff2, final_norm_g):
    b, s, d = x.shape
    assert d == D_MODEL and s % TM_PROJ == 0 and s % TM_FFN == 0 and s % TQ == 0
    assert norm1_g.shape[0] == 1, "single-layer block"
    assert w_in.shape[2] == 2 * QA_W + 2 * KA_W + 2 * QB_W
    kb, qat, qbt, vbt, kvat = _proj_call(x, norm1_g[0].reshape(1, d), w_in[0])

    bias_a, sink_a = _bias_a_table(sinks_a[0])
    bias_b = _bias_b_table(rel_bias_b[0])
    ga = jnp.broadcast_to(out_norm_a_g[0].astype(F32)[:, None], (QA_W, TQ))
    gb = jnp.broadcast_to(out_norm_b_g[0].astype(F32)[:, None], (QB_W, TQ))
    yt, wo, w1, w2 = _attn_call(kb, qat, qbt, vbt, kvat, bias_a, sink_a, bias_b, ga, gb,
                                w_out[0], w_ff1[0], w_ff2[0])

    return _ffn_call(yt, x, wo, norm2_g[0].reshape(1, d), w1, w2, final_norm_g.reshape(1, d))
```

```python
import functools

import jax
import jax.numpy as jnp
import numpy as np
from jax import lax
from jax.experimental import pallas as pl
from jax.experimental.pallas import tpu as pltpu

D_MODEL = 1024
CHUNK = 64
HEAD_DIM = 64
A_HEADS = 8
A_KV_HEADS = 2
A_GROUP = A_HEADS // A_KV_HEADS
A_BAND_CHUNKS = 3
B_HEADS = 8
B_BAND_CHUNKS = 9
B_MAX_REL = 128
D_FF = 4 * D_MODEL
EPS = 1e-6
NEG_INF = -1e30
LOG2E = 1.4426950408889634

QA_W = A_HEADS * HEAD_DIM
KA_W = A_KV_HEADS * HEAD_DIM
QB_W = B_HEADS * HEAD_DIM
MIX_W = QA_W + QB_W
FT_QB, FT_VB, FT_KVA, FT_ROWS = QA_W, QA_W + QB_W, QA_W + 2 * QB_W, QA_W + 2 * QB_W + 2 * KA_W

LANES = 128
TQ = 256
A_PAIR = 2 * CHUNK
A_WIN = 4 * CHUNK
B_BLOCKS = 3
B_WIN = B_BLOCKS * TQ
ONES_ROWS = 16
PIPE_DEPTH = 7
TM_PROJ = 1024
PROJ_SUB = 512
TM_FFN = 1024
FF_CHUNK = 1024
V7X_VMEM_BYTES = 64 * 1024 * 1024
VMEM_LIMIT = V7X_VMEM_BYTES * 7 // 8

F32 = jnp.float32
BF16 = jnp.bfloat16


def _b_live_tiles():
    chunks_per_tile = A_PAIR // CHUNK
    live = []
    for p in range(B_BLOCKS):
        tiles = []
        for rt in range(TQ // A_PAIR):
            for lt in range(TQ // LANES):
                kcs = [p * (TQ // CHUNK) + rt * chunks_per_tile + i for i in range(chunks_per_tile)]
                qcs = [lt * (LANES // CHUNK) + i for i in range(LANES // CHUNK)]
                if any(qc <= kc <= qc + B_BAND_CHUNKS - 1 for kc in kcs for qc in qcs):
                    tiles.append((rt, lt))
        live.append(tuple(tiles))
    return tuple(live)


B_LIVE_TILES = _b_live_tiles()


def _rms(x, g):
    ms = jnp.mean(x * x, axis=-1, keepdims=True)
    return x * lax.rsqrt(ms + EPS) * g


def _rms_rows(xt, g):
    ms = jnp.mean(xt * xt, axis=0, keepdims=True)
    return xt * lax.rsqrt(ms + EPS) * g


def _dot(a, b):
    return jnp.dot(a, b, preferred_element_type=F32)


def _dot_nt(a, b):
    return lax.dot_general(a, b, (((1,), (1,)), ((), ())), preferred_element_type=F32)


def _dot_tn(a, b):
    return lax.dot_general(a, b, (((0,), (0,)), ((), ())), preferred_element_type=F32)


def _const_spec(shape):
    nd = len(shape)
    return pl.BlockSpec(shape, lambda *_: (0,) * nd, pipeline_mode=pl.Buffered(1))


def _proj_kernel(x_ref, g_ref, win_ref, wo_ref, w1_ref, w2_ref,
                 kb_ref, ft_ref, wo_bf_ref, w1_bf_ref, w2_bf_ref,
                 wk_scr, wt_scr):
    @pl.when((pl.program_id(0) == 0) & (pl.program_id(1) == 0))
    def _prepare_weights():
        o_ka, o_qb, o_kb, o_vb = QA_W, QA_W + 2 * KA_W, QA_W + 2 * KA_W + QB_W, QA_W + 2 * KA_W + 2 * QB_W
        q_scale = HEAD_DIM ** -0.5 * LOG2E
        wk_scr[...] = win_ref[:, o_kb:o_vb].astype(BF16)
        row = 0
        for lo, hi, scale in ((0, o_ka, q_scale), (o_qb, o_kb, q_scale), (o_vb, o_vb + QB_W, None),
                              (o_ka, o_qb, None)):
            part = win_ref[:, lo:hi]
            if scale is not None:
                part = part * scale
            wt_scr[row:row + hi - lo, :] = part.T.astype(BF16)
            row += hi - lo

    for i in range(TM_PROJ // PROJ_SUB):
        tok = slice(i * PROJ_SUB, (i + 1) * PROJ_SUB)
        n = _rms(x_ref[0, tok, :], g_ref[...]).astype(BF16)
        kb_ref[0, tok, :] = _dot(n, wk_scr[...]).astype(BF16)
        ft_ref[0, :, tok] = _dot_nt(wt_scr[...], n).astype(BF16)
    wo_bf_ref[...] = wo_ref[...].astype(BF16)
    w1_bf_ref[...] = w1_ref[...].astype(BF16)
    w2_bf_ref[...] = w2_ref[...].astype(BF16)


def _proj_call(x, g, w_in, wo, w1, w2):
    b, s, _ = x.shape
    steps = b * (s // TM_PROJ)

    def tok(width):
        return pl.BlockSpec((1, TM_PROJ, width), lambda bi, i: (bi, i, 0))

    def feat(width):
        return pl.BlockSpec((1, width, TM_PROJ), lambda bi, i: (bi, 0, i))

    def row_share(w):
        rows = w.shape[0] // steps
        assert rows * steps == w.shape[0] and rows % 16 == 0, "weight rows must split into bf16 row tiles"
        return pl.BlockSpec((rows, w.shape[1]), lambda bi, i: (bi * (s // TM_PROJ) + i, 0))

    def bf(w):
        return jax.ShapeDtypeStruct(w.shape, BF16)

    return pl.pallas_call(
        _proj_kernel,
        grid=(b, s // TM_PROJ),
        in_specs=[tok(D_MODEL), _const_spec(g.shape), _const_spec(w_in.shape),
                  row_share(wo), row_share(w1), row_share(w2)],
        out_specs=[tok(QB_W), feat(FT_ROWS), row_share(wo), row_share(w1), row_share(w2)],
        out_shape=[jax.ShapeDtypeStruct((b, s, QB_W), BF16), jax.ShapeDtypeStruct((b, FT_ROWS, s), BF16),
                   bf(wo), bf(w1), bf(w2)],
        scratch_shapes=[pltpu.VMEM((D_MODEL, QB_W), BF16),
                        pltpu.VMEM((w_in.shape[1] - QB_W, D_MODEL), BF16)],
        compiler_params=pltpu.CompilerParams(
            dimension_semantics=("arbitrary", "arbitrary"), vmem_limit_bytes=VMEM_LIMIT),
        name="norm_in_proj",
    )(x, g, w_in, wo, w1, w2)


def _attn_kernel(ft_ref, kb0_ref, kb1_ref, kb2_ref, vbt0_ref, vbt1_ref, kvap_ref,
                 bias_a_ref, sink_ref, bias_b_ref, ga_ref, gb_ref,
                 y_ref, ya_scr, yb_scr):
    j = pl.program_id(1)
    zeros_q = jnp.zeros((HEAD_DIM, TQ), BF16)
    ones_k = jnp.ones((ONES_ROWS, TQ), BF16)
    zeros_p = jnp.zeros((HEAD_DIM, A_PAIR), BF16)
    kb_refs = (kb0_ref, kb1_ref, kb2_ref)
    vbt_prev = (vbt0_ref, vbt1_ref)
    pen_a = jnp.where(j >= 1, 0.0, NEG_INF).astype(F32)
    before_start = [jnp.where(j - (B_BLOCKS - 1 - p) < 0, jnp.ones((TQ, LANES), BF16), jnp.zeros((TQ, LANES), BF16))
                    for p in range(B_BLOCKS - 1)]
    pen_rows = jnp.where(lax.broadcasted_iota(jnp.int32, (LANES, TQ), 0) == 0, NEG_INF, 0.0).astype(BF16)


    b_parts = {}

    def b_scores(h, p):
        hp, half = divmod(h, 2)
        qt = ft_ref[0, FT_QB + h * HEAD_DIM:FT_QB + (h + 1) * HEAD_DIM, :]
        qm = jnp.concatenate([qt, zeros_q] if half == 0 else [zeros_q, qt], axis=0)
        kp = kb_refs[p][0, :, hp * LANES:(hp + 1) * LANES]
        if p < B_BLOCKS - 1:
            kp = jnp.concatenate([kp, before_start[p]], axis=1)
            qm = jnp.concatenate([qm, pen_rows], axis=0)
        raw = _dot(kp, qm)
        tiles = {}
        for rt, lt in B_LIVE_TILES[p]:
            ks = slice(rt * A_PAIR, (rt + 1) * A_PAIR)
            qs = slice(lt * LANES, (lt + 1) * LANES)
            tiles[rt, lt] = raw[ks, qs] + bias_b_ref[h, p * TQ + rt * A_PAIR:p * TQ + (rt + 1) * A_PAIR, qs]
        mcols = []
        for lt in range(TQ // LANES):
            col = functools.reduce(jnp.maximum, [t for (_, l2), t in tiles.items() if l2 == lt])
            mcols.append(jnp.max(col, axis=0, keepdims=True))
        return tiles, mcols

    def b_output(h, p, tiles, mcols):
        rows = slice(h * HEAD_DIM, (h + 1) * HEAD_DIM)
        dead = jnp.zeros((A_PAIR, LANES), BF16)
        pt = jnp.concatenate([
            jnp.concatenate([jnp.exp2(tiles[rt, lt] - mcols[lt]).astype(BF16) if (rt, lt) in tiles else dead
                             for rt in range(TQ // A_PAIR)], axis=0)
            for lt in range(TQ // LANES)], axis=1)
        mp = jnp.concatenate(mcols, axis=1)
        if p < B_BLOCKS - 1:
            vh = vbt_prev[p][0, rows, :]
        else:
            vh = ft_ref[0, FT_VB + h * HEAD_DIM:FT_VB + (h + 1) * HEAD_DIM, :]
        vt = jnp.concatenate([vh, ones_k], axis=0)
        b_parts.setdefault(h, []).append((_dot(vt, pt)[:HEAD_DIM + 8], mp))
        if p == B_BLOCKS - 1:
            parts = b_parts.pop(h)
            m = functools.reduce(jnp.maximum, [mq for _, mq in parts])
            ot = sum(op * jnp.exp2(mq - m) for op, mq in parts)
            yb_scr[rows, :] = ot[:HEAD_DIM] * (1.0 / ot[HEAD_DIM:HEAD_DIM + 1])

    def a_windows(r):
        if r == 0:
            kvwin = jnp.concatenate([kvap_ref[0], ft_ref[0, FT_KVA:, :A_PAIR]], axis=1)
        else:
            kvwin = ft_ref[0, FT_KVA:, :]
        return kvwin[:KA_W], kvwin[KA_W:]

    def a_scores(r, kvh):
        tok = slice(r * A_PAIR, (r + 1) * A_PAIR)
        kwin, _ = a_windows(r)
        blocks = []
        for g in range(A_GROUP):
            h = kvh * A_GROUP + g
            qt = ft_ref[0, h * HEAD_DIM:(h + 1) * HEAD_DIM, tok]
            blocks.append(jnp.concatenate([qt, zeros_p] if kvh == 0 else [zeros_p, qt], axis=0))
        qst = jnp.concatenate(blocks, axis=1)
        st = _dot_tn(kwin, qst) + bias_a_ref[kvh]
        s0 = st[:A_PAIR]
        s1 = st[A_PAIR:]
        if r == 0:
            s0 = s0 + pen_a
        m = jnp.maximum(jnp.max(jnp.maximum(s0, s1), axis=0, keepdims=True), sink_ref[kvh])
        return (s0, s1), m

    def a_output(r, kvh, st, m):
        tok = slice(r * A_PAIR, (r + 1) * A_PAIR)
        _, vwin = a_windows(r)
        pt = jnp.concatenate([jnp.exp2(st[0] - m), jnp.exp2(st[1] - m)], axis=0).astype(BF16)
        vt = jnp.concatenate([vwin[kvh * HEAD_DIM:(kvh + 1) * HEAD_DIM, :], ones_k], axis=0)
        ot = _dot(vt, pt)
        den = ot[HEAD_DIM:HEAD_DIM + 1] + jnp.exp2(sink_ref[kvh] - m)
        yt = ot[:HEAD_DIM] * (1.0 / den)
        for g in range(A_GROUP):
            h = kvh * A_GROUP + g
            ya_scr[h * HEAD_DIM:(h + 1) * HEAD_DIM, tok] = yt[:, g * A_PAIR:(g + 1) * A_PAIR]

    units = [(b_scores, b_output, (h, p)) for h in range(B_HEADS) for p in range(B_BLOCKS)]
    units += [(a_scores, a_output, (r, kvh)) for r in range(TQ // A_PAIR) for kvh in range(A_KV_HEADS)]
    pending = []
    for scores, output, args in units:
        pending.append((output, args, scores(*args)))
        if len(pending) > PIPE_DEPTH:
            output, args, staged = pending.pop(0)
            output(*args, *staged)
    for output, args, staged in pending:
        output(*args, *staged)

    y_ref[0, QA_W:, :] = _rms_rows(yb_scr[...], gb_ref[...]).astype(BF16)
    y_ref[0, :QA_W, :] = _rms_rows(ya_scr[...], ga_ref[...]).astype(BF16)


def _attn_call(kb, ft, bias_a, sink_a, bias_b, ga, gb):
    b, s, _ = kb.shape
    nq = s // TQ
    assert FT_VB % QB_W == 0 and FT_KVA % (2 * KA_W) == 0

    def tok_back(back):
        return pl.BlockSpec((1, TQ, QB_W), lambda bi, j: (bi, jnp.maximum(j - back, 0), 0))

    def vb_back(back):
        return pl.BlockSpec((1, QB_W, TQ), lambda bi, j: (bi, FT_VB // QB_W, jnp.maximum(j - back, 0)))

    in_specs = [
        pl.BlockSpec((1, FT_ROWS, TQ), lambda bi, j: (bi, 0, j)),
        tok_back(2), tok_back(1), tok_back(0),
        vb_back(2), vb_back(1),
        pl.BlockSpec((1, 2 * KA_W, A_PAIR),
                     lambda bi, j: (bi, FT_KVA // (2 * KA_W), jnp.maximum(2 * j - 1, 0))),
        _const_spec(bias_a.shape), _const_spec(sink_a.shape), _const_spec(bias_b.shape),
        _const_spec(ga.shape), _const_spec(gb.shape),
    ]
    return pl.pallas_call(
        _attn_kernel,
        grid=(b, nq),
        in_specs=in_specs,
        out_specs=pl.BlockSpec((1, MIX_W, TQ), lambda bi, j: (bi, 0, j)),
        out_shape=jax.ShapeDtypeStruct((b, MIX_W, s), BF16),
        scratch_shapes=[pltpu.VMEM((QA_W, TQ), F32), pltpu.VMEM((QB_W, TQ), F32)],
        compiler_params=pltpu.CompilerParams(
            dimension_semantics=("arbitrary", "arbitrary"), vmem_limit_bytes=VMEM_LIMIT),
        name="attention",
    )(ft, kb, kb, kb, ft, ft, ft, bias_a, sink_a, bias_b, ga, gb)


def _ffn_kernel(yt_ref, x_ref, wo_ref, g2_ref, w1_ref, w2_ref, gf_ref, o_ref, h_scr, n2_scr):
    h = x_ref[0] + _dot_tn(yt_ref[0], wo_ref[...])
    h_scr[...] = h
    n2_scr[...] = _rms(h, g2_ref[...]).astype(BF16)

    for c in range(D_FF // FF_CHUNK):
        cols = slice(c * FF_CHUNK, (c + 1) * FF_CHUNK)
        u = _dot(n2_scr[...], w1_ref[:, cols])
        u = jnp.square(jnp.maximum(u, 0.0)).astype(BF16)
        h_scr[...] += _dot(u, w2_ref[cols, :])
    o_ref[0] = _rms(h_scr[...], gf_ref[...])


def _ffn_call(yt, x, wo, g2, w1, w2, gf):
    b, s, _ = x.shape
    return pl.pallas_call(
        _ffn_kernel,
        grid=(b, s // TM_FFN),
        in_specs=[
            pl.BlockSpec((1, MIX_W, TM_FFN), lambda bi, i: (bi, 0, i)),
            pl.BlockSpec((1, TM_FFN, D_MODEL), lambda bi, i: (bi, i, 0)),
            _const_spec(wo.shape), _const_spec(g2.shape),
            _const_spec(w1.shape), _const_spec(w2.shape), _const_spec(gf.shape),
        ],
        out_specs=pl.BlockSpec((1, TM_FFN, D_MODEL), lambda bi, i: (bi, i, 0)),
        out_shape=jax.ShapeDtypeStruct((b, s, D_MODEL), F32),
        scratch_shapes=[pltpu.VMEM((TM_FFN, D_MODEL), F32), pltpu.VMEM((TM_FFN, D_MODEL), BF16)],
        compiler_params=pltpu.CompilerParams(
            dimension_semantics=("arbitrary", "arbitrary"), vmem_limit_bytes=VMEM_LIMIT),
        name="out_proj_mlp",
    )(yt, x, wo, g2, w1, w2, gf)


def _bias_a_table(sinks):
    k = np.arange(A_WIN)[:, None]
    i = np.arange(A_PAIR)[None, :]
    dist = np.abs(A_PAIR + i - k).astype(np.float32)
    qc = i // CHUNK
    kc = k // CHUNK
    allowed = (kc >= qc) & (kc <= qc + A_BAND_CHUNKS - 1)
    slopes = jnp.exp2(-8.0 * (jnp.arange(A_HEADS, dtype=F32) + 1.0) / A_HEADS)
    bias = -slopes[:, None, None] * jnp.asarray(dist)[None] * LOG2E
    bias = jnp.where(jnp.asarray(allowed)[None], bias, NEG_INF)
    bias = bias.reshape(A_KV_HEADS, A_GROUP, A_WIN, A_PAIR).transpose(0, 2, 1, 3)
    bias = bias.reshape(A_KV_HEADS, A_WIN, A_GROUP * A_PAIR)
    sink = jnp.broadcast_to((sinks.astype(F32) * LOG2E).reshape(A_KV_HEADS, 1, A_GROUP, 1),
                            (A_KV_HEADS, 1, A_GROUP, A_PAIR)).reshape(A_KV_HEADS, 1, A_GROUP * A_PAIR)
    return bias, sink


def _bias_b_table(rel_bias):
    rb = rel_bias.astype(F32) * LOG2E
    base = (B_BAND_CHUNKS - 1) * CHUNK

    def tile(delta):
        lo = base + LANES * delta - (LANES - 1)
        if lo >= B_MAX_REL:
            return jnp.broadcast_to(rb[:, -1][:, None, None], (B_HEADS, LANES, LANES))
        off = np.arange(2 * LANES)
        off = np.where(off >= LANES, off - 2 * LANES, off)
        rel = np.clip(base + LANES * delta + off, -B_MAX_REL, B_MAX_REL) + B_MAX_REL
        row = rb[:, jnp.asarray(rel)]
        flat = jnp.tile(row, (1, LANES))[:, :LANES * (2 * LANES - 1)]
        return flat.reshape(B_HEADS, LANES, 2 * LANES - 1)[:, :, :LANES]

    tiles = {delta: tile(delta) for delta in range(-(B_WIN // LANES - 1), TQ // LANES)}
    bias = jnp.concatenate(
        [jnp.concatenate([tiles[t - a] for t in range(TQ // LANES)], axis=2) for a in range(B_WIN // LANES)],
        axis=1)
    k = np.arange(B_WIN)[:, None]
    q = np.arange(TQ)[None, :]
    qc = q // CHUNK
    kc = k // CHUNK
    allowed = (kc >= qc) & (kc <= qc + B_BAND_CHUNKS - 1)
    return jnp.where(jnp.asarray(allowed)[None], bias, NEG_INF)


def kernel(x, norm1_g, w_in, sinks_a, rel_bias_b, out_norm_a_g, out_norm_b_g, w_out, norm2_g,
           w_ff1, w_ff2, final_norm_g):
    b, s, d = x.shape
    assert d == D_MODEL and s % TM_PROJ == 0 and s % TM_FFN == 0 and s % TQ == 0
    assert norm1_g.shape[0] == 1, "single-layer block"
    assert w_in.shape[2] == 2 * QA_W + 2 * KA_W + 2 * QB_W
    kb, ft, wo, w1, w2 = _proj_call(
        x, norm1_g[0].reshape(1, d), w_in[0], w_out[0], w_ff1[0], w_ff2[0])

    bias_a, sink_a = _bias_a_table(sinks_a[0])
    bias_b = _bias_b_table(rel_bias_b[0])
    ga = jnp.broadcast_to(out_norm_a_g[0].astype(F32)[:, None], (QA_W, TQ))
    gb = jnp.broadcast_to(out_norm_b_g[0].astype(F32)[:, None], (QB_W, TQ))
    yt = _attn_call(kb, ft, bias_a, sink_a, bias_b, ga, gb)

    return _ffn_call(yt, x, wo, norm2_g[0].reshape(1, d), w1, w2, final_norm_g.reshape(1, d))
```

```python
import functools

import jax
import jax.numpy as jnp
import numpy as np
from jax import lax
from jax.experimental import pallas as pl
from jax.experimental.pallas import tpu as pltpu

D_MODEL = 1024
CHUNK = 64
HEAD_DIM = 64
A_HEADS = 8
A_KV_HEADS = 2
A_GROUP = A_HEADS // A_KV_HEADS
A_BAND_CHUNKS = 3
B_HEADS = 8
B_BAND_CHUNKS = 9
B_MAX_REL = 128
D_FF = 4 * D_MODEL
EPS = 1e-6
NEG_INF = -1e30
LOG2E = 1.4426950408889634

QA_W = A_HEADS * HEAD_DIM
KA_W = A_KV_HEADS * HEAD_DIM
QB_W = B_HEADS * HEAD_DIM
MIX_W = QA_W + QB_W
FT_QB, FT_VB, FT_KVA, FT_ROWS = QA_W, QA_W + QB_W, QA_W + 2 * QB_W, QA_W + 2 * QB_W + 2 * KA_W

LANES = 128
SUBLANES = 8
TQ = 256
Q_PER_STEP = 2
TS = Q_PER_STEP * TQ
A_PAIR = 2 * CHUNK
A_WIN = 4 * CHUNK
B_BLOCKS = 3
B_WIN = B_BLOCKS * TQ
ONES_ROWS = 16
PIPE_DEPTH = 7
TM_PROJ = 1024
PROJ_SUB = 512
TM_FFN = 1024
FF_CHUNK = 1024
V7X_VMEM_BYTES = 64 * 1024 * 1024
VMEM_LIMIT = V7X_VMEM_BYTES * 7 // 8

F32 = jnp.float32
BF16 = jnp.bfloat16


def _b_live_tiles():
    chunks_per_tile = A_PAIR // CHUNK
    live = []
    for p in range(B_BLOCKS):
        tiles = []
        for rt in range(TQ // A_PAIR):
            for lt in range(TQ // LANES):
                kcs = [p * (TQ // CHUNK) + rt * chunks_per_tile + i for i in range(chunks_per_tile)]
                qcs = [lt * (LANES // CHUNK) + i for i in range(LANES // CHUNK)]
                if any(qc <= kc <= qc + B_BAND_CHUNKS - 1 for kc in kcs for qc in qcs):
                    tiles.append((rt, lt))
        live.append(tuple(tiles))
    return tuple(live)


B_LIVE_TILES = _b_live_tiles()


def _rms(x, g):
    ms = jnp.mean(x * x, axis=-1, keepdims=True)
    return x * lax.rsqrt(ms + EPS) * g


def _rms_rows(xt, g):
    ms = jnp.mean(xt * xt, axis=0, keepdims=True)
    return xt * lax.rsqrt(ms + EPS) * g


def _dot(a, b):
    return jnp.dot(a, b, preferred_element_type=F32)


def _dot_nt(a, b):
    return lax.dot_general(a, b, (((1,), (1,)), ((), ())), preferred_element_type=F32)


def _dot_tn(a, b):
    return lax.dot_general(a, b, (((0,), (0,)), ((), ())), preferred_element_type=F32)


def _const_spec(shape):
    nd = len(shape)
    return pl.BlockSpec(shape, lambda *_: (0,) * nd, pipeline_mode=pl.Buffered(1))


def _proj_kernel(x_ref, g_ref, win_ref, wo_ref, w1_ref, w2_ref,
                 kb_ref, ft_ref, wo_bf_ref, w1_bf_ref, w2_bf_ref,
                 wk_scr, wt_scr):
    @pl.when((pl.program_id(0) == 0) & (pl.program_id(1) == 0))
    def _prepare_weights():
        o_ka, o_qb, o_kb, o_vb = QA_W, QA_W + 2 * KA_W, QA_W + 2 * KA_W + QB_W, QA_W + 2 * KA_W + 2 * QB_W
        q_scale = HEAD_DIM ** -0.5 * LOG2E
        wk_scr[...] = win_ref[:, o_kb:o_vb].astype(BF16)
        row = 0
        for lo, hi, scale in ((0, o_ka, q_scale), (o_qb, o_kb, q_scale), (o_vb, o_vb + QB_W, None),
                              (o_ka, o_qb, None)):
            part = win_ref[:, lo:hi]
            if scale is not None:
                part = part * scale
            wt_scr[row:row + hi - lo, :] = part.T.astype(BF16)
            row += hi - lo

    for i in range(TM_PROJ // PROJ_SUB):
        tok = slice(i * PROJ_SUB, (i + 1) * PROJ_SUB)
        n = _rms(x_ref[0, tok, :], g_ref[...]).astype(BF16)
        kb_ref[0, tok, :] = _dot(n, wk_scr[...]).astype(BF16)
        ft_ref[0, :, tok] = _dot_nt(wt_scr[...], n).astype(BF16)
    wo_bf_ref[...] = wo_ref[...].astype(BF16)
    w1_bf_ref[...] = w1_ref[...].astype(BF16)
    w2_bf_ref[...] = w2_ref[...].astype(BF16)


def _proj_call(x, g, w_in, wo, w1, w2):
    b, s, _ = x.shape
    steps = b * (s // TM_PROJ)

    def tok(width):
        return pl.BlockSpec((1, TM_PROJ, width), lambda bi, i: (bi, i, 0))

    def feat(width):
        return pl.BlockSpec((1, width, TM_PROJ), lambda bi, i: (bi, 0, i))

    def row_share(w):
        rows = w.shape[0] // steps
        assert rows * steps == w.shape[0] and rows % 16 == 0, "weight rows must split into bf16 row tiles"
        return pl.BlockSpec((rows, w.shape[1]), lambda bi, i: (bi * (s // TM_PROJ) + i, 0))

    def bf(w):
        return jax.ShapeDtypeStruct(w.shape, BF16)

    return pl.pallas_call(
        _proj_kernel,
        grid=(b, s // TM_PROJ),
        in_specs=[tok(D_MODEL), _const_spec(g.shape), _const_spec(w_in.shape),
                  row_share(wo), row_share(w1), row_share(w2)],
        out_specs=[tok(QB_W), feat(FT_ROWS), row_share(wo), row_share(w1), row_share(w2)],
        out_shape=[jax.ShapeDtypeStruct((b, s, QB_W), BF16), jax.ShapeDtypeStruct((b, FT_ROWS, s), BF16),
                   bf(wo), bf(w1), bf(w2)],
        scratch_shapes=[pltpu.VMEM((D_MODEL, QB_W), BF16),
                        pltpu.VMEM((w_in.shape[1] - QB_W, D_MODEL), BF16)],
        compiler_params=pltpu.CompilerParams(
            dimension_semantics=("arbitrary", "arbitrary"), vmem_limit_bytes=VMEM_LIMIT),
        name="norm_in_proj",
    )(x, g, w_in, wo, w1, w2)


def _attn_kernel(ft_ref, kbp_ref, kbc_ref, vbp_ref, kvap_ref,
                 bias_a_ref, sink_ref, bias_b_ref, ga_ref, gb_ref,
                 y_ref, ya_scr, yb_scr):
    i = pl.program_id(1)
    zeros_q = jnp.zeros((HEAD_DIM, TQ), BF16)
    ones_k = jnp.ones((ONES_ROWS, TQ), BF16)
    zeros_p = jnp.zeros((HEAD_DIM, A_PAIR), BF16)
    pen_a = jnp.where(i >= 1, 0.0, NEG_INF).astype(F32)
    before_start = jnp.where(i == 0, jnp.ones((TQ, LANES), BF16), jnp.zeros((TQ, LANES), BF16))
    pen_rows = jnp.where(lax.broadcasted_iota(jnp.int32, (LANES, TQ), 0) == 0, NEG_INF, 0.0).astype(BF16)

    def key_block(q, p):
        g = q + p
        return g < Q_PER_STEP, slice((g % Q_PER_STEP) * TQ, (g % Q_PER_STEP + 1) * TQ)


    b_parts = {}
    sumsq_b = [jnp.zeros((SUBLANES, TQ), F32) for _ in range(Q_PER_STEP)]
    sumsq_a = [jnp.zeros((SUBLANES, A_PAIR), F32) for _ in range(TS // A_PAIR)]

    def rowgroup_sumsq(t):
        return jnp.sum((t * t).reshape(t.shape[0] // SUBLANES, SUBLANES, t.shape[1]), axis=0)

    def inv_rms(sumsq, width):
        return lax.rsqrt(jnp.sum(sumsq, axis=0, keepdims=True) / width + EPS)

    def b_scores(q, h, p):
        hp, half = divmod(h, 2)
        qt = ft_ref[0, FT_QB + h * HEAD_DIM:FT_QB + (h + 1) * HEAD_DIM, q * TQ:(q + 1) * TQ]
        qm = jnp.concatenate([qt, zeros_q] if half == 0 else [zeros_q, qt], axis=0)
        in_prev, toks = key_block(q, p)
        kp = (kbp_ref if in_prev else kbc_ref)[0, toks, hp * LANES:(hp + 1) * LANES]
        if in_prev:
            kp = jnp.concatenate([kp, before_start], axis=1)
            qm = jnp.concatenate([qm, pen_rows], axis=0)
        raw = _dot(kp, qm)
        tiles = {}
        for rt, lt in B_LIVE_TILES[p]:
            ks = slice(rt * A_PAIR, (rt + 1) * A_PAIR)
            qs = slice(lt * LANES, (lt + 1) * LANES)
            tiles[rt, lt] = raw[ks, qs] + bias_b_ref[h, p * TQ + rt * A_PAIR:p * TQ + (rt + 1) * A_PAIR, qs]
        mcols = []
        for lt in range(TQ // LANES):
            col = functools.reduce(jnp.maximum, [t for (_, l2), t in tiles.items() if l2 == lt])
            mcols.append(jnp.max(col, axis=0, keepdims=True))
        return tiles, mcols

    def b_output(q, h, p, tiles, mcols):
        rows = slice(h * HEAD_DIM, (h + 1) * HEAD_DIM)
        tokq = slice(q * TQ, (q + 1) * TQ)
        dead = jnp.zeros((A_PAIR, LANES), BF16)
        pt = jnp.concatenate([
            jnp.concatenate([jnp.exp2(tiles[rt, lt] - mcols[lt]).astype(BF16) if (rt, lt) in tiles else dead
                             for rt in range(TQ // A_PAIR)], axis=0)
            for lt in range(TQ // LANES)], axis=1)
        mp = jnp.concatenate(mcols, axis=1)
        in_prev, toks = key_block(q, p)
        if in_prev:
            vh = vbp_ref[0, rows, toks]
        else:
            vh = ft_ref[0, FT_VB + h * HEAD_DIM:FT_VB + (h + 1) * HEAD_DIM, toks]
        vt = jnp.concatenate([vh, ones_k], axis=0)
        b_parts.setdefault((q, h), []).append((_dot(vt, pt)[:HEAD_DIM + 8], mp))
        if p == B_BLOCKS - 1:
            parts = b_parts.pop((q, h))
            m = functools.reduce(jnp.maximum, [mq for _, mq in parts])
            ot = sum(op * jnp.exp2(mq - m) for op, mq in parts)
            yt = ot[:HEAD_DIM] * (1.0 / ot[HEAD_DIM:HEAD_DIM + 1])
            sumsq_b[q] = sumsq_b[q] + rowgroup_sumsq(yt)
            yb_scr[rows, tokq] = yt * gb_ref[rows, :]

    def a_windows(r):
        if r == 0:
            kvwin = jnp.concatenate([kvap_ref[0], ft_ref[0, FT_KVA:, :A_PAIR]], axis=1)
        else:
            kvwin = ft_ref[0, FT_KVA:, (r - 1) * A_PAIR:(r + 1) * A_PAIR]
        return kvwin[:KA_W], kvwin[KA_W:]

    def a_scores(r, kvh):
        tok = slice(r * A_PAIR, (r + 1) * A_PAIR)
        kwin, _ = a_windows(r)
        blocks = []
        for g in range(A_GROUP):
            h = kvh * A_GROUP + g
            qt = ft_ref[0, h * HEAD_DIM:(h + 1) * HEAD_DIM, tok]
            blocks.append(jnp.concatenate([qt, zeros_p] if kvh == 0 else [zeros_p, qt], axis=0))
        qst = jnp.concatenate(blocks, axis=1)
        st = _dot_tn(kwin, qst) + bias_a_ref[kvh]
        s0 = st[:A_PAIR]
        s1 = st[A_PAIR:]
        if r == 0:
            s0 = s0 + pen_a
        m = jnp.maximum(jnp.max(jnp.maximum(s0, s1), axis=0, keepdims=True), sink_ref[kvh])
        return (s0, s1), m

    def a_output(r, kvh, st, m):
        tok = slice(r * A_PAIR, (r + 1) * A_PAIR)
        _, vwin = a_windows(r)
        pt = jnp.concatenate([jnp.exp2(st[0] - m), jnp.exp2(st[1] - m)], axis=0).astype(BF16)
        vt = jnp.concatenate([vwin[kvh * HEAD_DIM:(kvh + 1) * HEAD_DIM, :], ones_k], axis=0)
        ot = _dot(vt, pt)
        den = ot[HEAD_DIM:HEAD_DIM + 1] + jnp.exp2(sink_ref[kvh] - m)
        yt = ot[:HEAD_DIM] * (1.0 / den)
        for g in range(A_GROUP):
            rows = slice((kvh * A_GROUP + g) * HEAD_DIM, (kvh * A_GROUP + g + 1) * HEAD_DIM)
            yh = yt[:, g * A_PAIR:(g + 1) * A_PAIR]
            sumsq_a[r] = sumsq_a[r] + rowgroup_sumsq(yh)
            ya_scr[rows, tok] = yh * ga_ref[rows, :]

    def b_finish(q):
        tokq = slice(q * TQ, (q + 1) * TQ)
        y_ref[0, QA_W:, tokq] = (yb_scr[:, tokq] * inv_rms(sumsq_b[q], QB_W)).astype(BF16)

    def a_finish(r):
        tok = slice(r * A_PAIR, (r + 1) * A_PAIR)
        y_ref[0, :QA_W, tok] = (ya_scr[:, tok] * inv_rms(sumsq_a[r], QA_W)).astype(BF16)

    units = []
    for q in range(Q_PER_STEP):
        units += [(b_scores, b_output, (q, h, p),
                   functools.partial(b_finish, q) if (h, p) == (B_HEADS - 1, B_BLOCKS - 1) else None)
                  for h in range(B_HEADS) for p in range(B_BLOCKS)]
        units += [(a_scores, a_output, (r, kvh), functools.partial(a_finish, r) if kvh == A_KV_HEADS - 1 else None)
                  for r in range(q * TQ // A_PAIR, (q + 1) * TQ // A_PAIR) for kvh in range(A_KV_HEADS)]
    pending = []

    def run_output():
        output, args, staged, finish = pending.pop(0)
        output(*args, *staged)
        if finish is not None:
            finish()

    for scores, output, args, finish in units:
        pending.append((output, args, scores(*args), finish))
        if len(pending) > PIPE_DEPTH:
            run_output()
    while pending:
        run_output()


def _attn_call(kb, ft, bias_a, sink_a, bias_b, ga, gb):
    b, s, _ = kb.shape
    assert FT_VB % QB_W == 0 and FT_KVA % (2 * KA_W) == 0
    assert Q_PER_STEP == B_BLOCKS - 1
    pairs_per_step = TS // A_PAIR

    in_specs = [
        pl.BlockSpec((1, FT_ROWS, TS), lambda bi, i: (bi, 0, i)),
        pl.BlockSpec((1, TS, QB_W), lambda bi, i: (bi, jnp.maximum(i - 1, 0), 0)),
        pl.BlockSpec((1, TS, QB_W), lambda bi, i: (bi, i, 0)),
        pl.BlockSpec((1, QB_W, TS), lambda bi, i: (bi, FT_VB // QB_W, jnp.maximum(i - 1, 0))),
        pl.BlockSpec((1, 2 * KA_W, A_PAIR),
                     lambda bi, i: (bi, FT_KVA // (2 * KA_W), jnp.maximum(pairs_per_step * i - 1, 0))),
        _const_spec(bias_a.shape), _const_spec(sink_a.shape), _const_spec(bias_b.shape),
        _const_spec(ga.shape), _const_spec(gb.shape),
    ]
    return pl.pallas_call(
        _attn_kernel,
        grid=(b, s // TS),
        in_specs=in_specs,
        out_specs=pl.BlockSpec((1, MIX_W, TS), lambda bi, i: (bi, 0, i)),
        out_shape=jax.ShapeDtypeStruct((b, MIX_W, s), BF16),
        scratch_shapes=[pltpu.VMEM((QA_W, TS), F32), pltpu.VMEM((QB_W, TS), F32)],
        compiler_params=pltpu.CompilerParams(
            dimension_semantics=("arbitrary", "arbitrary"), vmem_limit_bytes=VMEM_LIMIT),
        name="attention",
    )(ft, kb, kb, ft, ft, bias_a, sink_a, bias_b, ga, gb)


def _ffn_kernel(yt_ref, x_ref, wo_ref, g2_ref, w1_ref, w2_ref, gf_ref, o_ref, h_scr, n2_scr):
    h = x_ref[0] + _dot_tn(yt_ref[0], wo_ref[...])
    h_scr[...] = h
    n2_scr[...] = _rms(h, g2_ref[...]).astype(BF16)

    for c in range(D_FF // FF_CHUNK):
        cols = slice(c * FF_CHUNK, (c + 1) * FF_CHUNK)
        u = _dot(n2_scr[...], w1_ref[:, cols])
        u = jnp.square(jnp.maximum(u, 0.0)).astype(BF16)
        h_scr[...] += _dot(u, w2_ref[cols, :])
    o_ref[0] = _rms(h_scr[...], gf_ref[...])


def _ffn_call(yt, x, wo, g2, w1, w2, gf):
    b, s, _ = x.shape
    return pl.pallas_call(
        _ffn_kernel,
        grid=(b, s // TM_FFN),
        in_specs=[
            pl.BlockSpec((1, MIX_W, TM_FFN), lambda bi, i: (bi, 0, i)),
            pl.BlockSpec((1, TM_FFN, D_MODEL), lambda bi, i: (bi, i, 0)),
            _const_spec(wo.shape), _const_spec(g2.shape),
            _const_spec(w1.shape), _const_spec(w2.shape), _const_spec(gf.shape),
        ],
        out_specs=pl.BlockSpec((1, TM_FFN, D_MODEL), lambda bi, i: (bi, i, 0)),
        out_shape=jax.ShapeDtypeStruct((b, s, D_MODEL), F32),
        scratch_shapes=[pltpu.VMEM((TM_FFN, D_MODEL), F32), pltpu.VMEM((TM_FFN, D_MODEL), BF16)],
        compiler_params=pltpu.CompilerParams(
            dimension_semantics=("arbitrary", "arbitrary"), vmem_limit_bytes=VMEM_LIMIT),
        name="out_proj_mlp",
    )(yt, x, wo, g2, w1, w2, gf)


def _bias_a_table(sinks):
    k = np.arange(A_WIN)[:, None]
    i = np.arange(A_PAIR)[None, :]
    dist = np.abs(A_PAIR + i - k).astype(np.float32)
    qc = i // CHUNK
    kc = k // CHUNK
    allowed = (kc >= qc) & (kc <= qc + A_BAND_CHUNKS - 1)
    slopes = jnp.exp2(-8.0 * (jnp.arange(A_HEADS, dtype=F32) + 1.0) / A_HEADS)
    bias = -slopes[:, None, None] * jnp.asarray(dist)[None] * LOG2E
    bias = jnp.where(jnp.asarray(allowed)[None], bias, NEG_INF)
    bias = bias.reshape(A_KV_HEADS, A_GROUP, A_WIN, A_PAIR).transpose(0, 2, 1, 3)
    bias = bias.reshape(A_KV_HEADS, A_WIN, A_GROUP * A_PAIR)
    sink = jnp.broadcast_to((sinks.astype(F32) * LOG2E).reshape(A_KV_HEADS, 1, A_GROUP, 1),
                            (A_KV_HEADS, 1, A_GROUP, A_PAIR)).reshape(A_KV_HEADS, 1, A_GROUP * A_PAIR)
    return bias, sink


def _bias_b_table(rel_bias):
    rb = rel_bias.astype(F32) * LOG2E
    base = (B_BAND_CHUNKS - 1) * CHUNK

    def tile(delta):
        lo = base + LANES * delta - (LANES - 1)
        if lo >= B_MAX_REL:
            return jnp.broadcast_to(rb[:, -1][:, None, None], (B_HEADS, LANES, LANES))
        off = np.arange(2 * LANES)
        off = np.where(off >= LANES, off - 2 * LANES, off)
        rel = np.clip(base + LANES * delta + off, -B_MAX_REL, B_MAX_REL) + B_MAX_REL
        row = rb[:, jnp.asarray(rel)]
        flat = jnp.tile(row, (1, LANES))[:, :LANES * (2 * LANES - 1)]
        return flat.reshape(B_HEADS, LANES, 2 * LANES - 1)[:, :, :LANES]

    tiles = {delta: tile(delta) for delta in range(-(B_WIN // LANES - 1), TQ // LANES)}
    bias = jnp.concatenate(
        [jnp.concatenate([tiles[t - a] for t in range(TQ // LANES)], axis=2) for a in range(B_WIN // LANES)],
        axis=1)
    k = np.arange(B_WIN)[:, None]
    q = np.arange(TQ)[None, :]
    qc = q // CHUNK
    kc = k // CHUNK
    allowed = (kc >= qc) & (kc <= qc + B_BAND_CHUNKS - 1)
    return jnp.where(jnp.asarray(allowed)[None], bias, NEG_INF)


def kernel(x, norm1_g, w_in, sinks_a, rel_bias_b, out_norm_a_g, out_norm_b_g, w_out, norm2_g,
           w_ff1, w_ff2, final_norm_g):
    b, s, d = x.shape
    assert d == D_MODEL and s % TM_PROJ == 0 and s % TM_FFN == 0 and s % TS == 0
    assert norm1_g.shape[0] == 1, "single-layer block"
    assert w_in.shape[2] == 2 * QA_W + 2 * KA_W + 2 * QB_W
    kb, ft, wo, w1, w2 = _proj_call(
        x, norm1_g[0].reshape(1, d), w_in[0], w_out[0], w_ff1[0], w_ff2[0])

    bias_a, sink_a = _bias_a_table(sinks_a[0])
    bias_b = _bias_b_table(rel_bias_b[0])
    ga = jnp.broadcast_to(out_norm_a_g[0].astype(F32)[:, None], (QA_W, A_PAIR))
    gb = jnp.broadcast_to(out_norm_b_g[0].astype(F32)[:, None], (QB_W, TQ))
    yt = _attn_call(kb, ft, bias_a, sink_a, bias_b, ga, gb)

    return _ffn_call(yt, x, wo, norm2_g[0].reshape(1, d), w1, w2, final_norm_g.reshape(1, d))
```

```python
import functools

import jax
import jax.numpy as jnp
import numpy as np
from jax import lax
from jax.experimental import pallas as pl
from jax.experimental.pallas import tpu as pltpu

D_MODEL = 1024
CHUNK = 64
HEAD_DIM = 64
A_HEADS = 8
A_KV_HEADS = 2
A_GROUP = A_HEADS // A_KV_HEADS
A_BAND_CHUNKS = 3
B_HEADS = 8
B_BAND_CHUNKS = 9
B_MAX_REL = 128
D_FF = 4 * D_MODEL
EPS = 1e-6
NEG_INF = -1e30
LOG2E = 1.4426950408889634

QA_W = A_HEADS * HEAD_DIM
KA_W = A_KV_HEADS * HEAD_DIM
QB_W = B_HEADS * HEAD_DIM
MIX_W = QA_W + QB_W
FT_QB, FT_VB, FT_KVA, FT_ROWS = QA_W, QA_W + QB_W, QA_W + 2 * QB_W, QA_W + 2 * QB_W + 2 * KA_W

LANES = 128
SUBLANES = 8
TQ = 256
Q_PER_STEP = 4
TS = Q_PER_STEP * TQ
A_PAIR = 2 * CHUNK
A_WIN = 4 * CHUNK
B_BLOCKS = 3
B_WIN = B_BLOCKS * TQ
B_LOOKBACK = B_BLOCKS - 1
ONES_ROWS = 16
PIPE_DEPTH = 7
TM_PROJ = 1024
PROJ_SUB = 512
TM_FFN = 1024
FF_CHUNK = 1024
V7X_VMEM_BYTES = 64 * 1024 * 1024
VMEM_LIMIT = V7X_VMEM_BYTES * 7 // 8

F32 = jnp.float32
BF16 = jnp.bfloat16


def _b_live_tiles():
    chunks_per_tile = A_PAIR // CHUNK
    live = []
    for p in range(B_BLOCKS):
        tiles = []
        for rt in range(TQ // A_PAIR):
            for lt in range(TQ // LANES):
                kcs = [p * (TQ // CHUNK) + rt * chunks_per_tile + i for i in range(chunks_per_tile)]
                qcs = [lt * (LANES // CHUNK) + i for i in range(LANES // CHUNK)]
                if any(qc <= kc <= qc + B_BAND_CHUNKS - 1 for kc in kcs for qc in qcs):
                    tiles.append((rt, lt))
        live.append(tuple(tiles))
    return tuple(live)


B_LIVE_TILES = _b_live_tiles()


def _rms(x, g):
    ms = jnp.mean(x * x, axis=-1, keepdims=True)
    return x * lax.rsqrt(ms + EPS) * g


def _rms_rows(xt, g):
    ms = jnp.mean(xt * xt, axis=0, keepdims=True)
    return xt * lax.rsqrt(ms + EPS) * g


def _dot(a, b):
    return jnp.dot(a, b, preferred_element_type=F32)


def _dot_nt(a, b):
    return lax.dot_general(a, b, (((1,), (1,)), ((), ())), preferred_element_type=F32)


def _dot_tn(a, b):
    return lax.dot_general(a, b, (((0,), (0,)), ((), ())), preferred_element_type=F32)


def _const_spec(shape):
    nd = len(shape)
    return pl.BlockSpec(shape, lambda *_: (0,) * nd, pipeline_mode=pl.Buffered(1))


def _proj_kernel(x_ref, g_ref, win_ref, wo_ref, w1_ref, w2_ref,
                 kb_ref, ft_ref, wo_bf_ref, w1_bf_ref, w2_bf_ref,
                 wk_scr, wt_scr):
    @pl.when((pl.program_id(0) == 0) & (pl.program_id(1) == 0))
    def _prepare_weights():
        o_ka, o_qb, o_kb, o_vb = QA_W, QA_W + 2 * KA_W, QA_W + 2 * KA_W + QB_W, QA_W + 2 * KA_W + 2 * QB_W
        q_scale = HEAD_DIM ** -0.5 * LOG2E
        wk_scr[...] = win_ref[:, o_kb:o_vb].astype(BF16)
        row = 0
        for lo, hi, scale in ((0, o_ka, q_scale), (o_qb, o_kb, q_scale), (o_vb, o_vb + QB_W, None),
                              (o_ka, o_qb, None)):
            part = win_ref[:, lo:hi]
            if scale is not None:
                part = part * scale
            wt_scr[row:row + hi - lo, :] = part.T.astype(BF16)
            row += hi - lo

    for i in range(TM_PROJ // PROJ_SUB):
        tok = slice(i * PROJ_SUB, (i + 1) * PROJ_SUB)
        n = _rms(x_ref[0, tok, :], g_ref[...]).astype(BF16)
        kb_ref[0, tok, :] = _dot(n, wk_scr[...]).astype(BF16)
        ft_ref[0, :, tok] = _dot_nt(wt_scr[...], n).astype(BF16)
    wo_bf_ref[...] = wo_ref[...].astype(BF16)
    w1_bf_ref[...] = w1_ref[...].astype(BF16)
    w2_bf_ref[...] = w2_ref[...].astype(BF16)


def _proj_call(x, g, w_in, wo, w1, w2):
    b, s, _ = x.shape
    steps = b * (s // TM_PROJ)

    def tok(width):
        return pl.BlockSpec((1, TM_PROJ, width), lambda bi, i: (bi, i, 0))

    def feat(width):
        return pl.BlockSpec((1, width, TM_PROJ), lambda bi, i: (bi, 0, i))

    def row_share(w):
        rows = w.shape[0] // steps
        assert rows * steps == w.shape[0] and rows % 16 == 0, "weight rows must split into bf16 row tiles"
        return pl.BlockSpec((rows, w.shape[1]), lambda bi, i: (bi * (s // TM_PROJ) + i, 0))

    def bf(w):
        return jax.ShapeDtypeStruct(w.shape, BF16)

    return pl.pallas_call(
        _proj_kernel,
        grid=(b, s // TM_PROJ),
        in_specs=[tok(D_MODEL), _const_spec(g.shape), _const_spec(w_in.shape),
                  row_share(wo), row_share(w1), row_share(w2)],
        out_specs=[tok(QB_W), feat(FT_ROWS), row_share(wo), row_share(w1), row_share(w2)],
        out_shape=[jax.ShapeDtypeStruct((b, s, QB_W), BF16), jax.ShapeDtypeStruct((b, FT_ROWS, s), BF16),
                   bf(wo), bf(w1), bf(w2)],
        scratch_shapes=[pltpu.VMEM((D_MODEL, QB_W), BF16),
                        pltpu.VMEM((w_in.shape[1] - QB_W, D_MODEL), BF16)],
        compiler_params=pltpu.CompilerParams(
            dimension_semantics=("arbitrary", "arbitrary"), vmem_limit_bytes=VMEM_LIMIT),
        name="norm_in_proj",
    )(x, g, w_in, wo, w1, w2)


def _attn_kernel(ft_ref, kbp_ref, kbc_ref, vbp_ref, kvap_ref,
                 bias_a_ref, sink_ref, bias_b_ref, ga_ref, gb_ref,
                 y_ref, ya_scr, yb_scr):
    i = pl.program_id(1)
    zeros_q = jnp.zeros((HEAD_DIM, TQ), BF16)
    ones_k = jnp.ones((ONES_ROWS, TQ), BF16)
    zeros_p = jnp.zeros((HEAD_DIM, A_PAIR), BF16)
    pen_a = jnp.where(i >= 1, 0.0, NEG_INF).astype(F32)
    before_start = jnp.where(i == 0, jnp.ones((TQ, LANES), BF16), jnp.zeros((TQ, LANES), BF16))
    pen_rows = jnp.where(lax.broadcasted_iota(jnp.int32, (LANES, TQ), 0) == 0, NEG_INF, 0.0).astype(BF16)

    def key_block(q, p):
        g = q + p - B_LOOKBACK
        return (True, slice((g + B_LOOKBACK) * TQ, (g + B_LOOKBACK + 1) * TQ)) if g < 0 else \
            (False, slice(g * TQ, (g + 1) * TQ))


    b_parts = {}
    sumsq_b = [jnp.zeros((SUBLANES, TQ), F32) for _ in range(Q_PER_STEP)]
    sumsq_a = [jnp.zeros((SUBLANES, A_PAIR), F32) for _ in range(TS // A_PAIR)]

    def rowgroup_sumsq(t):
        return jnp.sum((t * t).reshape(t.shape[0] // SUBLANES, SUBLANES, t.shape[1]), axis=0)

    def inv_rms(sumsq, width):
        return lax.rsqrt(jnp.sum(sumsq, axis=0, keepdims=True) / width + EPS)

    def b_scores(q, h, p):
        hp, half = divmod(h, 2)
        qt = ft_ref[0, FT_QB + h * HEAD_DIM:FT_QB + (h + 1) * HEAD_DIM, q * TQ:(q + 1) * TQ]
        qm = jnp.concatenate([qt, zeros_q] if half == 0 else [zeros_q, qt], axis=0)
        in_prev, toks = key_block(q, p)
        kp = (kbp_ref if in_prev else kbc_ref)[0, toks, hp * LANES:(hp + 1) * LANES]
        if in_prev:
            kp = jnp.concatenate([kp, before_start], axis=1)
            qm = jnp.concatenate([qm, pen_rows], axis=0)
        raw = _dot(kp, qm)
        tiles = {}
        for rt, lt in B_LIVE_TILES[p]:
            ks = slice(rt * A_PAIR, (rt + 1) * A_PAIR)
            qs = slice(lt * LANES, (lt + 1) * LANES)
            tiles[rt, lt] = raw[ks, qs] + bias_b_ref[h, p * TQ + rt * A_PAIR:p * TQ + (rt + 1) * A_PAIR, qs]
        mcols = []
        for lt in range(TQ // LANES):
            col = functools.reduce(jnp.maximum, [t for (_, l2), t in tiles.items() if l2 == lt])
            mcols.append(jnp.max(col, axis=0, keepdims=True))
        return tiles, mcols

    def b_output(q, h, p, tiles, mcols):
        rows = slice(h * HEAD_DIM, (h + 1) * HEAD_DIM)
        tokq = slice(q * TQ, (q + 1) * TQ)
        dead = jnp.zeros((A_PAIR, LANES), BF16)
        pt = jnp.concatenate([
            jnp.concatenate([jnp.exp2(tiles[rt, lt] - mcols[lt]).astype(BF16) if (rt, lt) in tiles else dead
                             for rt in range(TQ // A_PAIR)], axis=0)
            for lt in range(TQ // LANES)], axis=1)
        mp = jnp.concatenate(mcols, axis=1)
        in_prev, toks = key_block(q, p)
        if in_prev:
            vh = vbp_ref[0, rows, toks]
        else:
            vh = ft_ref[0, FT_VB + h * HEAD_DIM:FT_VB + (h + 1) * HEAD_DIM, toks]
        vt = jnp.concatenate([vh, ones_k], axis=0)
        b_parts.setdefault((q, h), []).append((_dot(vt, pt)[:HEAD_DIM + 8], mp))
        if p == B_BLOCKS - 1:
            parts = b_parts.pop((q, h))
            m = functools.reduce(jnp.maximum, [mq for _, mq in parts])
            ot = sum(op * jnp.exp2(mq - m) for op, mq in parts)
            yt = ot[:HEAD_DIM] * (1.0 / ot[HEAD_DIM:HEAD_DIM + 1])
            sumsq_b[q] = sumsq_b[q] + rowgroup_sumsq(yt)
            yb_scr[rows, tokq] = yt * gb_ref[rows, :]

    def a_windows(r):
        if r == 0:
            kvwin = jnp.concatenate([kvap_ref[0], ft_ref[0, FT_KVA:, :A_PAIR]], axis=1)
        else:
            kvwin = ft_ref[0, FT_KVA:, (r - 1) * A_PAIR:(r + 1) * A_PAIR]
        return kvwin[:KA_W], kvwin[KA_W:]

    def a_scores(r, kvh):
        tok = slice(r * A_PAIR, (r + 1) * A_PAIR)
        kwin, _ = a_windows(r)
        blocks = []
        for g in range(A_GROUP):
            h = kvh * A_GROUP + g
            qt = ft_ref[0, h * HEAD_DIM:(h + 1) * HEAD_DIM, tok]
            blocks.append(jnp.concatenate([qt, zeros_p] if kvh == 0 else [zeros_p, qt], axis=0))
        qst = jnp.concatenate(blocks, axis=1)
        st = _dot_tn(kwin, qst) + bias_a_ref[kvh]
        s0 = st[:A_PAIR]
        s1 = st[A_PAIR:]
        if r == 0:
            s0 = s0 + pen_a
        m = jnp.maximum(jnp.max(jnp.maximum(s0, s1), axis=0, keepdims=True), sink_ref[kvh])
        return (s0, s1), m

    def a_output(r, kvh, st, m):
        tok = slice(r * A_PAIR, (r + 1) * A_PAIR)
        _, vwin = a_windows(r)
        pt = jnp.concatenate([jnp.exp2(st[0] - m), jnp.exp2(st[1] - m)], axis=0).astype(BF16)
        vt = jnp.concatenate([vwin[kvh * HEAD_DIM:(kvh + 1) * HEAD_DIM, :], ones_k], axis=0)
        ot = _dot(vt, pt)
        den = ot[HEAD_DIM:HEAD_DIM + 1] + jnp.exp2(sink_ref[kvh] - m)
        yt = ot[:HEAD_DIM] * (1.0 / den)
        for g in range(A_GROUP):
            rows = slice((kvh * A_GROUP + g) * HEAD_DIM, (kvh * A_GROUP + g + 1) * HEAD_DIM)
            yh = yt[:, g * A_PAIR:(g + 1) * A_PAIR]
            sumsq_a[r] = sumsq_a[r] + rowgroup_sumsq(yh)
            ya_scr[rows, tok] = yh * ga_ref[rows, :]

    def b_finish(q):
        tokq = slice(q * TQ, (q + 1) * TQ)
        y_ref[0, QA_W:, tokq] = (yb_scr[:, tokq] * inv_rms(sumsq_b[q], QB_W)).astype(BF16)

    def a_finish(r):
        tok = slice(r * A_PAIR, (r + 1) * A_PAIR)
        y_ref[0, :QA_W, tok] = (ya_scr[:, tok] * inv_rms(sumsq_a[r], QA_W)).astype(BF16)

    units = []
    for q in range(Q_PER_STEP):
        units += [(b_scores, b_output, (q, h, p),
                   functools.partial(b_finish, q) if (h, p) == (B_HEADS - 1, B_BLOCKS - 1) else None)
                  for h in range(B_HEADS) for p in range(B_BLOCKS)]
        units += [(a_scores, a_output, (r, kvh), functools.partial(a_finish, r) if kvh == A_KV_HEADS - 1 else None)
                  for r in range(q * TQ // A_PAIR, (q + 1) * TQ // A_PAIR) for kvh in range(A_KV_HEADS)]
    pending = []

    def run_output():
        output, args, staged, finish = pending.pop(0)
        output(*args, *staged)
        if finish is not None:
            finish()

    for scores, output, args, finish in units:
        pending.append((output, args, scores(*args), finish))
        if len(pending) > PIPE_DEPTH:
            run_output()
    while pending:
        run_output()


def _attn_call(kb, ft, bias_a, sink_a, bias_b, ga, gb):
    b, s, _ = kb.shape
    assert FT_VB % QB_W == 0 and FT_KVA % (2 * KA_W) == 0
    look = B_LOOKBACK * TQ
    assert TS % look == 0
    pairs_per_step = TS // A_PAIR

    def look_block(i):
        return jnp.maximum(i * (TS // look) - 1, 0)

    in_specs = [
        pl.BlockSpec((1, FT_ROWS, TS), lambda bi, i: (bi, 0, i)),
        pl.BlockSpec((1, look, QB_W), lambda bi, i: (bi, look_block(i), 0)),
        pl.BlockSpec((1, TS, QB_W), lambda bi, i: (bi, i, 0)),
        pl.BlockSpec((1, QB_W, look), lambda bi, i: (bi, FT_VB // QB_W, look_block(i))),
        pl.BlockSpec((1, 2 * KA_W, A_PAIR),
                     lambda bi, i: (bi, FT_KVA // (2 * KA_W), jnp.maximum(pairs_per_step * i - 1, 0))),
        _const_spec(bias_a.shape), _const_spec(sink_a.shape), _const_spec(bias_b.shape),
        _const_spec(ga.shape), _const_spec(gb.shape),
    ]
    return pl.pallas_call(
        _attn_kernel,
        grid=(b, s // TS),
        in_specs=in_specs,
        out_specs=pl.BlockSpec((1, MIX_W, TS), lambda bi, i: (bi, 0, i)),
        out_shape=jax.ShapeDtypeStruct((b, MIX_W, s), BF16),
        scratch_shapes=[pltpu.VMEM((QA_W, TS), F32), pltpu.VMEM((QB_W, TS), F32)],
        compiler_params=pltpu.CompilerParams(
            dimension_semantics=("arbitrary", "arbitrary"), vmem_limit_bytes=VMEM_LIMIT),
        name="attention",
    )(ft, kb, kb, ft, ft, bias_a, sink_a, bias_b, ga, gb)


def _ffn_kernel(yt_ref, x_ref, wo_ref, g2_ref, w1_ref, w2_ref, gf_ref, o_ref, h_scr, n2_scr):
    h = x_ref[0] + _dot_tn(yt_ref[0], wo_ref[...])
    h_scr[...] = h
    n2_scr[...] = _rms(h, g2_ref[...]).astype(BF16)

    for c in range(D_FF // FF_CHUNK):
        cols = slice(c * FF_CHUNK, (c + 1) * FF_CHUNK)
        u = _dot(n2_scr[...], w1_ref[:, cols])
        u = jnp.square(jnp.maximum(u, 0.0)).astype(BF16)
        h_scr[...] += _dot(u, w2_ref[cols, :])
    o_ref[0] = _rms(h_scr[...], gf_ref[...])


def _ffn_call(yt, x, wo, g2, w1, w2, gf):
    b, s, _ = x.shape
    return pl.pallas_call(
        _ffn_kernel,
        grid=(b, s // TM_FFN),
        in_specs=[
            pl.BlockSpec((1, MIX_W, TM_FFN), lambda bi, i: (bi, 0, i)),
            pl.BlockSpec((1, TM_FFN, D_MODEL), lambda bi, i: (bi, i, 0)),
            _const_spec(wo.shape), _const_spec(g2.shape),
            _const_spec(w1.shape), _const_spec(w2.shape), _const_spec(gf.shape),
        ],
        out_specs=pl.BlockSpec((1, TM_FFN, D_MODEL), lambda bi, i: (bi, i, 0)),
        out_shape=jax.ShapeDtypeStruct((b, s, D_MODEL), F32),
        scratch_shapes=[pltpu.VMEM((TM_FFN, D_MODEL), F32), pltpu.VMEM((TM_FFN, D_MODEL), BF16)],
        compiler_params=pltpu.CompilerParams(
            dimension_semantics=("arbitrary", "arbitrary"), vmem_limit_bytes=VMEM_LIMIT),
        name="out_proj_mlp",
    )(yt, x, wo, g2, w1, w2, gf)


def _bias_a_table(sinks):
    k = np.arange(A_WIN)[:, None]
    i = np.arange(A_PAIR)[None, :]
    dist = np.abs(A_PAIR + i - k).astype(np.float32)
    qc = i // CHUNK
    kc = k // CHUNK
    allowed = (kc >= qc) & (kc <= qc + A_BAND_CHUNKS - 1)
    slopes = jnp.exp2(-8.0 * (jnp.arange(A_HEADS, dtype=F32) + 1.0) / A_HEADS)
    bias = -slopes[:, None, None] * jnp.asarray(dist)[None] * LOG2E
    bias = jnp.where(jnp.asarray(allowed)[None], bias, NEG_INF)
    bias = bias.reshape(A_KV_HEADS, A_GROUP, A_WIN, A_PAIR).transpose(0, 2, 1, 3)
    bias = bias.reshape(A_KV_HEADS, A_WIN, A_GROUP * A_PAIR)
    sink = jnp.broadcast_to((sinks.astype(F32) * LOG2E).reshape(A_KV_HEADS, 1, A_GROUP, 1),
                            (A_KV_HEADS, 1, A_GROUP, A_PAIR)).reshape(A_KV_HEADS, 1, A_GROUP * A_PAIR)
    return bias, sink


def _bias_b_table(rel_bias):
    rb = rel_bias.astype(F32) * LOG2E
    base = (B_BAND_CHUNKS - 1) * CHUNK

    def tile(delta):
        lo = base + LANES * delta - (LANES - 1)
        if lo >= B_MAX_REL:
            return jnp.broadcast_to(rb[:, -1][:, None, None], (B_HEADS, LANES, LANES))
        off = np.arange(2 * LANES)
        off = np.where(off >= LANES, off - 2 * LANES, off)
        rel = np.clip(base + LANES * delta + off, -B_MAX_REL, B_MAX_REL) + B_MAX_REL
        row = rb[:, jnp.asarray(rel)]
        flat = jnp.tile(row, (1, LANES))[:, :LANES * (2 * LANES - 1)]
        return flat.reshape(B_HEADS, LANES, 2 * LANES - 1)[:, :, :LANES]

    tiles = {delta: tile(delta) for delta in range(-(B_WIN // LANES - 1), TQ // LANES)}
    bias = jnp.concatenate(
        [jnp.concatenate([tiles[t - a] for t in range(TQ // LANES)], axis=2) for a in range(B_WIN // LANES)],
        axis=1)
    k = np.arange(B_WIN)[:, None]
    q = np.arange(TQ)[None, :]
    qc = q // CHUNK
    kc = k // CHUNK
    allowed = (kc >= qc) & (kc <= qc + B_BAND_CHUNKS - 1)
    return jnp.where(jnp.asarray(allowed)[None], bias, NEG_INF)


def kernel(x, norm1_g, w_in, sinks_a, rel_bias_b, out_norm_a_g, out_norm_b_g, w_out, norm2_g,
           w_ff1, w_ff2, final_norm_g):
    b, s, d = x.shape
    assert d == D_MODEL and s % TM_PROJ == 0 and s % TM_FFN == 0 and s % TS == 0
    assert norm1_g.shape[0] == 1, "single-layer block"
    assert w_in.shape[2] == 2 * QA_W + 2 * KA_W + 2 * QB_W
    kb, ft, wo, w1, w2 = _proj_call(
        x, norm1_g[0].reshape(1, d), w_in[0], w_out[0], w_ff1[0], w_ff2[0])

    bias_a, sink_a = _bias_a_table(sinks_a[0])
    bias_b = _bias_b_table(rel_bias_b[0])
    ga = jnp.broadcast_to(out_norm_a_g[0].astype(F32)[:, None], (QA_W, A_PAIR))
    gb = jnp.broadcast_to(out_norm_b_g[0].astype(F32)[:, None], (QB_W, TQ))
    yt = _attn_call(kb, ft, bias_a, sink_a, bias_b, ga, gb)

    return _ffn_call(yt, x, wo, norm2_g[0].reshape(1, d), w1, w2, final_norm_g.reshape(1, d))
```

```python
import functools

import jax
import jax.numpy as jnp
import numpy as np
from jax import lax
from jax.experimental import pallas as pl
from jax.experimental.pallas import tpu as pltpu

D_MODEL = 1024
CHUNK = 64
HEAD_DIM = 64
A_HEADS = 8
A_KV_HEADS = 2
A_GROUP = A_HEADS // A_KV_HEADS
A_BAND_CHUNKS = 3
B_HEADS = 8
B_BAND_CHUNKS = 9
B_MAX_REL = 128
D_FF = 4 * D_MODEL
EPS = 1e-6
NEG_INF = -1e30
LOG2E = 1.4426950408889634

QA_W = A_HEADS * HEAD_DIM
KA_W = A_KV_HEADS * HEAD_DIM
QB_W = B_HEADS * HEAD_DIM
MIX_W = QA_W + QB_W
FT_QB, FT_VB, FT_KVA, FT_ROWS = QA_W, QA_W + QB_W, QA_W + 2 * QB_W, QA_W + 2 * QB_W + 2 * KA_W

LANES = 128
SUBLANES = 8
TQ = 256
Q_PER_STEP = 2
TS = Q_PER_STEP * TQ
A_PAIR = 2 * CHUNK
A_WIN = 4 * CHUNK
B_BLOCKS = 3
B_WIN = B_BLOCKS * TQ
B_LOOKBACK = B_BLOCKS - 1
ONES_ROWS = 16
PIPE_DEPTH = 7
TM_PROJ = 1024
PROJ_SUB = 512
TM_FFN = 1024
FF_CHUNK = 1024
FFN_TAIL_ROWS = 256
V7X_VMEM_BYTES = 64 * 1024 * 1024
VMEM_LIMIT = V7X_VMEM_BYTES * 7 // 8

F32 = jnp.float32
BF16 = jnp.bfloat16


def _b_live_tiles():
    chunks_per_tile = A_PAIR // CHUNK
    live = []
    for p in range(B_BLOCKS):
        tiles = []
        for rt in range(TQ // A_PAIR):
            for lt in range(TQ // LANES):
                kcs = [p * (TQ // CHUNK) + rt * chunks_per_tile + i for i in range(chunks_per_tile)]
                qcs = [lt * (LANES // CHUNK) + i for i in range(LANES // CHUNK)]
                if any(qc <= kc <= qc + B_BAND_CHUNKS - 1 for kc in kcs for qc in qcs):
                    tiles.append((rt, lt))
        live.append(tuple(tiles))
    return tuple(live)


B_LIVE_TILES = _b_live_tiles()


def _rms(x, g):
    ms = jnp.mean(x * x, axis=-1, keepdims=True)
    return x * lax.rsqrt(ms + EPS) * g


def _rms_rows(xt, g):
    ms = jnp.mean(xt * xt, axis=0, keepdims=True)
    return xt * lax.rsqrt(ms + EPS) * g


def _dot(a, b):
    return jnp.dot(a, b, preferred_element_type=F32)


def _dot_nt(a, b):
    return lax.dot_general(a, b, (((1,), (1,)), ((), ())), preferred_element_type=F32)


def _dot_tn(a, b):
    return lax.dot_general(a, b, (((0,), (0,)), ((), ())), preferred_element_type=F32)


def _const_spec(shape):
    nd = len(shape)
    return pl.BlockSpec(shape, lambda *_: (0,) * nd, pipeline_mode=pl.Buffered(1))


def _proj_kernel(x_ref, g_ref, win_ref, wo_ref, w1_ref, w2_ref,
                 kb_ref, ft_ref, wo_bf_ref, w1_bf_ref, w2_bf_ref,
                 wk_scr, wt_scr):
    @pl.when((pl.program_id(0) == 0) & (pl.program_id(1) == 0))
    def _prepare_weights():
        o_ka, o_qb, o_kb, o_vb = QA_W, QA_W + 2 * KA_W, QA_W + 2 * KA_W + QB_W, QA_W + 2 * KA_W + 2 * QB_W
        q_scale = HEAD_DIM ** -0.5 * LOG2E
        wk_scr[...] = win_ref[:, o_kb:o_vb].astype(BF16)
        row = 0
        for lo, hi, scale in ((0, o_ka, q_scale), (o_qb, o_kb, q_scale), (o_vb, o_vb + QB_W, None),
                              (o_ka, o_qb, None)):
            part = win_ref[:, lo:hi]
            if scale is not None:
                part = part * scale
            wt_scr[row:row + hi - lo, :] = part.T.astype(BF16)
            row += hi - lo

    for i in range(TM_PROJ // PROJ_SUB):
        tok = slice(i * PROJ_SUB, (i + 1) * PROJ_SUB)
        n = _rms(x_ref[0, tok, :], g_ref[...]).astype(BF16)
        kb_ref[0, tok, :] = _dot(n, wk_scr[...]).astype(BF16)
        ft_ref[0, :, tok] = _dot_nt(wt_scr[...], n).astype(BF16)
    wo_bf_ref[...] = wo_ref[...].astype(BF16)
    w1_bf_ref[...] = w1_ref[...].astype(BF16)
    w2_bf_ref[...] = w2_ref[...].astype(BF16)


def _proj_call(x, g, w_in, wo, w1, w2):
    b, s, _ = x.shape
    steps = b * (s // TM_PROJ)

    def tok(width):
        return pl.BlockSpec((1, TM_PROJ, width), lambda bi, i: (bi, i, 0))

    def feat(width):
        return pl.BlockSpec((1, width, TM_PROJ), lambda bi, i: (bi, 0, i))

    def row_share(w):
        rows = w.shape[0] // steps
        assert rows * steps == w.shape[0] and rows % 16 == 0, "weight rows must split into bf16 row tiles"
        return pl.BlockSpec((rows, w.shape[1]), lambda bi, i: (bi * (s // TM_PROJ) + i, 0))

    def bf(w):
        return jax.ShapeDtypeStruct(w.shape, BF16)

    return pl.pallas_call(
        _proj_kernel,
        grid=(b, s // TM_PROJ),
        in_specs=[tok(D_MODEL), _const_spec(g.shape), _const_spec(w_in.shape),
                  row_share(wo), row_share(w1), row_share(w2)],
        out_specs=[tok(QB_W), feat(FT_ROWS), row_share(wo), row_share(w1), row_share(w2)],
        out_shape=[jax.ShapeDtypeStruct((b, s, QB_W), BF16), jax.ShapeDtypeStruct((b, FT_ROWS, s), BF16),
                   bf(wo), bf(w1), bf(w2)],
        scratch_shapes=[pltpu.VMEM((D_MODEL, QB_W), BF16),
                        pltpu.VMEM((w_in.shape[1] - QB_W, D_MODEL), BF16)],
        compiler_params=pltpu.CompilerParams(
            dimension_semantics=("arbitrary", "arbitrary"), vmem_limit_bytes=VMEM_LIMIT),
        name="norm_in_proj",
    )(x, g, w_in, wo, w1, w2)


def _attn_kernel(ft_ref, kbp_ref, kbc_ref, vbp_ref, kvap_ref,
                 bias_a_ref, sink_ref, bias_b_ref, ga_ref, gb_ref,
                 y_ref, ya_scr, yb_scr):
    i = pl.program_id(1)
    zeros_q = jnp.zeros((HEAD_DIM, TQ), BF16)
    ones_k = jnp.ones((ONES_ROWS, TQ), BF16)
    zeros_p = jnp.zeros((HEAD_DIM, A_PAIR), BF16)
    pen_a = jnp.where(i >= 1, 0.0, NEG_INF).astype(F32)
    before_start = jnp.where(i == 0, jnp.ones((TQ, LANES), BF16), jnp.zeros((TQ, LANES), BF16))
    pen_rows = jnp.where(lax.broadcasted_iota(jnp.int32, (LANES, TQ), 0) == 0, NEG_INF, 0.0).astype(BF16)

    def key_block(q, p):
        g = q + p - B_LOOKBACK
        return (True, slice((g + B_LOOKBACK) * TQ, (g + B_LOOKBACK + 1) * TQ)) if g < 0 else \
            (False, slice(g * TQ, (g + 1) * TQ))


    b_parts = {}
    sumsq_b = [jnp.zeros((SUBLANES, TQ), F32) for _ in range(Q_PER_STEP)]
    sumsq_a = [jnp.zeros((SUBLANES, A_PAIR), F32) for _ in range(TS // A_PAIR)]

    def rowgroup_sumsq(t):
        return jnp.sum((t * t).reshape(t.shape[0] // SUBLANES, SUBLANES, t.shape[1]), axis=0)

    def inv_rms(sumsq, width):
        return lax.rsqrt(jnp.sum(sumsq, axis=0, keepdims=True) / width + EPS)

    def b_scores(q, h, p):
        hp, half = divmod(h, 2)
        qt = ft_ref[0, FT_QB + h * HEAD_DIM:FT_QB + (h + 1) * HEAD_DIM, q * TQ:(q + 1) * TQ]
        qm = jnp.concatenate([qt, zeros_q] if half == 0 else [zeros_q, qt], axis=0)
        in_prev, toks = key_block(q, p)
        kp = (kbp_ref if in_prev else kbc_ref)[0, toks, hp * LANES:(hp + 1) * LANES]
        if in_prev:
            kp = jnp.concatenate([kp, before_start], axis=1)
            qm = jnp.concatenate([qm, pen_rows], axis=0)
        raw = _dot(kp, qm)
        tiles = {}
        for rt, lt in B_LIVE_TILES[p]:
            ks = slice(rt * A_PAIR, (rt + 1) * A_PAIR)
            qs = slice(lt * LANES, (lt + 1) * LANES)
            tiles[rt, lt] = raw[ks, qs] + bias_b_ref[h, p * TQ + rt * A_PAIR:p * TQ + (rt + 1) * A_PAIR, qs]
        mcols = []
        for lt in range(TQ // LANES):
            col = functools.reduce(jnp.maximum, [t for (_, l2), t in tiles.items() if l2 == lt])
            mcols.append(jnp.max(col, axis=0, keepdims=True))
        return tiles, mcols

    def b_output(q, h, p, tiles, mcols):
        rows = slice(h * HEAD_DIM, (h + 1) * HEAD_DIM)
        tokq = slice(q * TQ, (q + 1) * TQ)
        dead = jnp.zeros((A_PAIR, LANES), BF16)
        pt = jnp.concatenate([
            jnp.concatenate([jnp.exp2(tiles[rt, lt] - mcols[lt]).astype(BF16) if (rt, lt) in tiles else dead
                             for rt in range(TQ // A_PAIR)], axis=0)
            for lt in range(TQ // LANES)], axis=1)
        mp = jnp.concatenate(mcols, axis=1)
        in_prev, toks = key_block(q, p)
        if in_prev:
            vh = vbp_ref[0, rows, toks]
        else:
            vh = ft_ref[0, FT_VB + h * HEAD_DIM:FT_VB + (h + 1) * HEAD_DIM, toks]
        vt = jnp.concatenate([vh, ones_k], axis=0)
        b_parts.setdefault((q, h), []).append((_dot(vt, pt)[:HEAD_DIM + 8], mp))
        if p == B_BLOCKS - 1:
            parts = b_parts.pop((q, h))
            m = functools.reduce(jnp.maximum, [mq for _, mq in parts])
            ot = sum(op * jnp.exp2(mq - m) for op, mq in parts)
            yt = ot[:HEAD_DIM] * (1.0 / ot[HEAD_DIM:HEAD_DIM + 1])
            sumsq_b[q] = sumsq_b[q] + rowgroup_sumsq(yt)
            yb_scr[rows, tokq] = yt * gb_ref[rows, :]

    def a_windows(r):
        if r == 0:
            kvwin = jnp.concatenate([kvap_ref[0], ft_ref[0, FT_KVA:, :A_PAIR]], axis=1)
        else:
            kvwin = ft_ref[0, FT_KVA:, (r - 1) * A_PAIR:(r + 1) * A_PAIR]
        return kvwin[:KA_W], kvwin[KA_W:]

    def a_scores(r, kvh):
        tok = slice(r * A_PAIR, (r + 1) * A_PAIR)
        kwin, _ = a_windows(r)
        blocks = []
        for g in range(A_GROUP):
            h = kvh * A_GROUP + g
            qt = ft_ref[0, h * HEAD_DIM:(h + 1) * HEAD_DIM, tok]
            blocks.append(jnp.concatenate([qt, zeros_p] if kvh == 0 else [zeros_p, qt], axis=0))
        qst = jnp.concatenate(blocks, axis=1)
        st = _dot_tn(kwin, qst) + bias_a_ref[kvh]
        s0 = st[:A_PAIR]
        s1 = st[A_PAIR:]
        if r == 0:
            s0 = s0 + pen_a
        m = jnp.maximum(jnp.max(jnp.maximum(s0, s1), axis=0, keepdims=True), sink_ref[kvh])
        return (s0, s1), m

    def a_output(r, kvh, st, m):
        tok = slice(r * A_PAIR, (r + 1) * A_PAIR)
        _, vwin = a_windows(r)
        pt = jnp.concatenate([jnp.exp2(st[0] - m), jnp.exp2(st[1] - m)], axis=0).astype(BF16)
        vt = jnp.concatenate([vwin[kvh * HEAD_DIM:(kvh + 1) * HEAD_DIM, :], ones_k], axis=0)
        ot = _dot(vt, pt)
        den = ot[HEAD_DIM:HEAD_DIM + 1] + jnp.exp2(sink_ref[kvh] - m)
        yt = ot[:HEAD_DIM] * (1.0 / den)
        for g in range(A_GROUP):
            rows = slice((kvh * A_GROUP + g) * HEAD_DIM, (kvh * A_GROUP + g + 1) * HEAD_DIM)
            yh = yt[:, g * A_PAIR:(g + 1) * A_PAIR]
            sumsq_a[r] = sumsq_a[r] + rowgroup_sumsq(yh)
            ya_scr[rows, tok] = yh * ga_ref[rows, :]

    def b_finish(q):
        tokq = slice(q * TQ, (q + 1) * TQ)
        y_ref[0, QA_W:, tokq] = (yb_scr[:, tokq] * inv_rms(sumsq_b[q], QB_W)).astype(BF16)

    def a_finish(r):
        tok = slice(r * A_PAIR, (r + 1) * A_PAIR)
        y_ref[0, :QA_W, tok] = (ya_scr[:, tok] * inv_rms(sumsq_a[r], QA_W)).astype(BF16)

    units = []
    for q in range(Q_PER_STEP):
        units += [(b_scores, b_output, (q, h, p),
                   functools.partial(b_finish, q) if (h, p) == (B_HEADS - 1, B_BLOCKS - 1) else None)
                  for h in range(B_HEADS) for p in range(B_BLOCKS)]
        units += [(a_scores, a_output, (r, kvh), functools.partial(a_finish, r) if kvh == A_KV_HEADS - 1 else None)
                  for r in range(q * TQ // A_PAIR, (q + 1) * TQ // A_PAIR) for kvh in range(A_KV_HEADS)]
    pending = []

    def run_output():
        output, args, staged, finish = pending.pop(0)
        output(*args, *staged)
        if finish is not None:
            finish()

    for scores, output, args, finish in units:
        pending.append((output, args, scores(*args), finish))
        if len(pending) > PIPE_DEPTH:
            run_output()
    while pending:
        run_output()


def _attn_call(kb, ft, bias_a, sink_a, bias_b, ga, gb):
    b, s, _ = kb.shape
    assert FT_VB % QB_W == 0 and FT_KVA % (2 * KA_W) == 0
    look = B_LOOKBACK * TQ
    assert TS % look == 0
    pairs_per_step = TS // A_PAIR

    def look_block(i):
        return jnp.maximum(i * (TS // look) - 1, 0)

    in_specs = [
        pl.BlockSpec((1, FT_ROWS, TS), lambda bi, i: (bi, 0, i)),
        pl.BlockSpec((1, look, QB_W), lambda bi, i: (bi, look_block(i), 0)),
        pl.BlockSpec((1, TS, QB_W), lambda bi, i: (bi, i, 0)),
        pl.BlockSpec((1, QB_W, look), lambda bi, i: (bi, FT_VB // QB_W, look_block(i))),
        pl.BlockSpec((1, 2 * KA_W, A_PAIR),
                     lambda bi, i: (bi, FT_KVA // (2 * KA_W), jnp.maximum(pairs_per_step * i - 1, 0))),
        _const_spec(bias_a.shape), _const_spec(sink_a.shape), _const_spec(bias_b.shape),
        _const_spec(ga.shape), _const_spec(gb.shape),
    ]
    return pl.pallas_call(
        _attn_kernel,
        grid=(b, s // TS),
        in_specs=in_specs,
        out_specs=pl.BlockSpec((1, MIX_W, TS), lambda bi, i: (bi, 0, i)),
        out_shape=jax.ShapeDtypeStruct((b, MIX_W, s), BF16),
        scratch_shapes=[pltpu.VMEM((QA_W, TS), F32), pltpu.VMEM((QB_W, TS), F32)],
        compiler_params=pltpu.CompilerParams(
            dimension_semantics=("arbitrary", "arbitrary"), vmem_limit_bytes=VMEM_LIMIT),
        name="attention",
    )(ft, kb, kb, ft, ft, bias_a, sink_a, bias_b, ga, gb)


def _ffn_kernel(yt_ref, x_ref, wo_ref, g2_ref, w1_ref, w2_ref, gf_ref, o_ref, h_scr, n2_scr):
    h = x_ref[0] + _dot_tn(yt_ref[0], wo_ref[...])
    h_scr[...] = h
    n2_scr[...] = _rms(h, g2_ref[...]).astype(BF16)

    def mlp_chunk(rows, c):
        cols = slice(c * FF_CHUNK, (c + 1) * FF_CHUNK)
        u = _dot(n2_scr[rows, :], w1_ref[:, cols])
        u = jnp.square(jnp.maximum(u, 0.0)).astype(BF16)
        return _dot(u, w2_ref[cols, :])

    last = D_FF // FF_CHUNK - 1
    for c in range(last):
        h_scr[...] += mlp_chunk(slice(None), c)
    for blk in range(TM_FFN // FFN_TAIL_ROWS):
        rows = slice(blk * FFN_TAIL_ROWS, (blk + 1) * FFN_TAIL_ROWS)
        o_ref[0, rows, :] = _rms(h_scr[rows, :] + mlp_chunk(rows, last), gf_ref[...])


def _ffn_call(yt, x, wo, g2, w1, w2, gf):
    b, s, _ = x.shape
    return pl.pallas_call(
        _ffn_kernel,
        grid=(b, s // TM_FFN),
        in_specs=[
            pl.BlockSpec((1, MIX_W, TM_FFN), lambda bi, i: (bi, 0, i)),
            pl.BlockSpec((1, TM_FFN, D_MODEL), lambda bi, i: (bi, i, 0)),
            _const_spec(wo.shape), _const_spec(g2.shape),
            _const_spec(w1.shape), _const_spec(w2.shape), _const_spec(gf.shape),
        ],
        out_specs=pl.BlockSpec((1, TM_FFN, D_MODEL), lambda bi, i: (bi, i, 0)),
        out_shape=jax.ShapeDtypeStruct((b, s, D_MODEL), F32),
        scratch_shapes=[pltpu.VMEM((TM_FFN, D_MODEL), F32), pltpu.VMEM((TM_FFN, D_MODEL), BF16)],
        compiler_params=pltpu.CompilerParams(
            dimension_semantics=("arbitrary", "arbitrary"), vmem_limit_bytes=VMEM_LIMIT),
        name="out_proj_mlp",
    )(yt, x, wo, g2, w1, w2, gf)


def _bias_a_table(sinks):
    k = np.arange(A_WIN)[:, None]
    i = np.arange(A_PAIR)[None, :]
    dist = np.abs(A_PAIR + i - k).astype(np.float32)
    qc = i // CHUNK
    kc = k // CHUNK
    allowed = (kc >= qc) & (kc <= qc + A_BAND_CHUNKS - 1)
    slopes = jnp.exp2(-8.0 * (jnp.arange(A_HEADS, dtype=F32) + 1.0) / A_HEADS)
    bias = -slopes[:, None, None] * jnp.asarray(dist)[None] * LOG2E
    bias = jnp.where(jnp.asarray(allowed)[None], bias, NEG_INF)
    bias = bias.reshape(A_KV_HEADS, A_GROUP, A_WIN, A_PAIR).transpose(0, 2, 1, 3)
    bias = bias.reshape(A_KV_HEADS, A_WIN, A_GROUP * A_PAIR)
    sink = jnp.broadcast_to((sinks.astype(F32) * LOG2E).reshape(A_KV_HEADS, 1, A_GROUP, 1),
                            (A_KV_HEADS, 1, A_GROUP, A_PAIR)).reshape(A_KV_HEADS, 1, A_GROUP * A_PAIR)
    return bias, sink


def _bias_b_table(rel_bias):
    rb = rel_bias.astype(F32) * LOG2E
    base = (B_BAND_CHUNKS - 1) * CHUNK

    def tile(delta):
        lo = base + LANES * delta - (LANES - 1)
        if lo >= B_MAX_REL:
            return jnp.broadcast_to(rb[:, -1][:, None, None], (B_HEADS, LANES, LANES))
        off = np.arange(2 * LANES)
        off = np.where(off >= LANES, off - 2 * LANES, off)
        rel = np.clip(base + LANES * delta + off, -B_MAX_REL, B_MAX_REL) + B_MAX_REL
        row = rb[:, jnp.asarray(rel)]
        flat = jnp.tile(row, (1, LANES))[:, :LANES * (2 * LANES - 1)]
        return flat.reshape(B_HEADS, LANES, 2 * LANES - 1)[:, :, :LANES]

    tiles = {delta: tile(delta) for delta in range(-(B_WIN // LANES - 1), TQ // LANES)}
    bias = jnp.concatenate(
        [jnp.concatenate([tiles[t - a] for t in range(TQ // LANES)], axis=2) for a in range(B_WIN // LANES)],
        axis=1)
    k = np.arange(B_WIN)[:, None]
    q = np.arange(TQ)[None, :]
    qc = q // CHUNK
    kc = k // CHUNK
    allowed = (kc >= qc) & (kc <= qc + B_BAND_CHUNKS - 1)
    return jnp.where(jnp.asarray(allowed)[None], bias, NEG_INF)


def kernel(x, norm1_g, w_in, sinks_a, rel_bias_b, out_norm_a_g, out_norm_b_g, w_out, norm2_g,
           w_ff1, w_ff2, final_norm_g):
    b, s, d = x.shape
    assert d == D_MODEL and s % TM_PROJ == 0 and s % TM_FFN == 0 and s % TS == 0
    assert norm1_g.shape[0] == 1, "single-layer block"
    assert w_in.shape[2] == 2 * QA_W + 2 * KA_W + 2 * QB_W
    kb, ft, wo, w1, w2 = _proj_call(
        x, norm1_g[0].reshape(1, d), w_in[0], w_out[0], w_ff1[0], w_ff2[0])

    bias_a, sink_a = _bias_a_table(sinks_a[0])
    bias_b = _bias_b_table(rel_bias_b[0])
    ga = jnp.broadcast_to(out_norm_a_g[0].astype(F32)[:, None], (QA_W, A_PAIR))
    gb = jnp.broadcast_to(out_norm_b_g[0].astype(F32)[:, None], (QB_W, TQ))
    yt = _attn_call(kb, ft, bias_a, sink_a, bias_b, ga, gb)

    return _ffn_call(yt, x, wo, norm2_g[0].reshape(1, d), w1, w2, final_norm_g.reshape(1, d))
```

```python
import functools

import jax
import jax.numpy as jnp
import numpy as np
from jax import lax
from jax.experimental import pallas as pl
from jax.experimental.pallas import tpu as pltpu

D_MODEL = 1024
CHUNK = 64
HEAD_DIM = 64
A_HEADS = 8
A_KV_HEADS = 2
A_GROUP = A_HEADS // A_KV_HEADS
A_BAND_CHUNKS = 3
B_HEADS = 8
B_BAND_CHUNKS = 9
B_MAX_REL = 128
D_FF = 4 * D_MODEL
EPS = 1e-6
NEG_INF = -1e30
LOG2E = 1.4426950408889634

QA_W = A_HEADS * HEAD_DIM
KA_W = A_KV_HEADS * HEAD_DIM
QB_W = B_HEADS * HEAD_DIM
MIX_W = QA_W + QB_W
FT_QB, FT_VB, FT_KVA, FT_ROWS = QA_W, QA_W + QB_W, QA_W + 2 * QB_W, QA_W + 2 * QB_W + 2 * KA_W

LANES = 128
SUBLANES = 8
TQ = 256
Q_PER_STEP = 2
TS = Q_PER_STEP * TQ
A_PAIR = 2 * CHUNK
A_WIN = 4 * CHUNK
B_BLOCKS = 3
B_WIN = B_BLOCKS * TQ
B_LOOKBACK = B_BLOCKS - 1
ONES_ROWS = 16
PIPE_DEPTH = 7
TM_PROJ = 1024
PROJ_SUB = 512
TM_FFN = 1024
FF_CHUNK = 1024
FFN_TAIL_ROWS = 256
V7X_VMEM_BYTES = 64 * 1024 * 1024
VMEM_LIMIT = V7X_VMEM_BYTES * 7 // 8

F32 = jnp.float32
BF16 = jnp.bfloat16


def _b_live_tiles():
    chunks_per_tile = A_PAIR // CHUNK
    live = []
    for p in range(B_BLOCKS):
        tiles = []
        for rt in range(TQ // A_PAIR):
            for lt in range(TQ // LANES):
                kcs = [p * (TQ // CHUNK) + rt * chunks_per_tile + i for i in range(chunks_per_tile)]
                qcs = [lt * (LANES // CHUNK) + i for i in range(LANES // CHUNK)]
                if any(qc <= kc <= qc + B_BAND_CHUNKS - 1 for kc in kcs for qc in qcs):
                    tiles.append((rt, lt))
        live.append(tuple(tiles))
    return tuple(live)


B_LIVE_TILES = _b_live_tiles()


def _rms(x, g):
    ms = jnp.mean(x * x, axis=-1, keepdims=True)
    return x * lax.rsqrt(ms + EPS) * g


def _rms_rows(xt, g):
    ms = jnp.mean(xt * xt, axis=0, keepdims=True)
    return xt * lax.rsqrt(ms + EPS) * g


def _dot(a, b):
    return jnp.dot(a, b, preferred_element_type=F32)


def _dot_nt(a, b):
    return lax.dot_general(a, b, (((1,), (1,)), ((), ())), preferred_element_type=F32)


def _dot_tn(a, b):
    return lax.dot_general(a, b, (((0,), (0,)), ((), ())), preferred_element_type=F32)


def _const_spec(shape):
    nd = len(shape)
    return pl.BlockSpec(shape, lambda *_: (0,) * nd, pipeline_mode=pl.Buffered(1))


def _proj_kernel(x_ref, g_ref, win_ref, wo_ref, w1_ref, w2_ref,
                 kb_ref, ft_ref, wo_bf_ref, w1_bf_ref, w2_bf_ref,
                 wk_scr, wt_scr):
    @pl.when((pl.program_id(0) == 0) & (pl.program_id(1) == 0))
    def _prepare_weights():
        o_ka, o_qb, o_kb, o_vb = QA_W, QA_W + 2 * KA_W, QA_W + 2 * KA_W + QB_W, QA_W + 2 * KA_W + 2 * QB_W
        q_scale = HEAD_DIM ** -0.5 * LOG2E
        wk_scr[...] = win_ref[:, o_kb:o_vb].astype(BF16)
        row = 0
        for lo, hi, scale in ((0, o_ka, q_scale), (o_qb, o_kb, q_scale), (o_vb, o_vb + QB_W, None),
                              (o_ka, o_qb, None)):
            part = win_ref[:, lo:hi]
            if scale is not None:
                part = part * scale
            wt_scr[row:row + hi - lo, :] = part.T.astype(BF16)
            row += hi - lo

    for i in range(TM_PROJ // PROJ_SUB):
        tok = slice(i * PROJ_SUB, (i + 1) * PROJ_SUB)
        n = _rms(x_ref[0, tok, :], g_ref[...]).astype(BF16)
        kb_ref[0, tok, :] = _dot(n, wk_scr[...]).astype(BF16)
        ft_ref[0, :, tok] = _dot_nt(wt_scr[...], n).astype(BF16)
    wo_bf_ref[...] = wo_ref[...].astype(BF16)
    w1_bf_ref[...] = w1_ref[...].astype(BF16)
    w2_bf_ref[...] = w2_ref[...].astype(BF16)


def _proj_call(x, g, w_in, wo, w1, w2):
    b, s, _ = x.shape
    steps = b * (s // TM_PROJ)

    def tok(width):
        return pl.BlockSpec((1, TM_PROJ, width), lambda bi, i: (bi, i, 0))

    def feat(width):
        return pl.BlockSpec((1, width, TM_PROJ), lambda bi, i: (bi, 0, i))

    def row_share(w):
        rows = w.shape[0] // steps
        assert rows * steps == w.shape[0] and rows % 16 == 0, "weight rows must split into bf16 row tiles"
        return pl.BlockSpec((rows, w.shape[1]), lambda bi, i: (bi * (s // TM_PROJ) + i, 0))

    def bf(w):
        return jax.ShapeDtypeStruct(w.shape, BF16)

    return pl.pallas_call(
        _proj_kernel,
        grid=(b, s // TM_PROJ),
        in_specs=[tok(D_MODEL), _const_spec(g.shape), _const_spec(w_in.shape),
                  row_share(wo), row_share(w1), row_share(w2)],
        out_specs=[tok(QB_W), feat(FT_ROWS), row_share(wo), row_share(w1), row_share(w2)],
        out_shape=[jax.ShapeDtypeStruct((b, s, QB_W), BF16), jax.ShapeDtypeStruct((b, FT_ROWS, s), BF16),
                   bf(wo), bf(w1), bf(w2)],
        scratch_shapes=[pltpu.VMEM((D_MODEL, QB_W), BF16),
                        pltpu.VMEM((w_in.shape[1] - QB_W, D_MODEL), BF16)],
        compiler_params=pltpu.CompilerParams(
            dimension_semantics=("arbitrary", "arbitrary"), vmem_limit_bytes=VMEM_LIMIT),
        name="norm_in_proj",
    )(x, g, w_in, wo, w1, w2)


def _attn_kernel(ft_ref, kbp_ref, kbc_ref, vbp_ref, kvap_ref,
                 bias_a_ref, sink_ref, bias_b_ref, ga_ref, gb_ref,
                 y_ref, ya_scr, yb_scr):
    i = pl.program_id(1)
    zeros_q = jnp.zeros((HEAD_DIM, TQ), BF16)
    ones_k = jnp.ones((ONES_ROWS, TQ), BF16)
    zeros_p = jnp.zeros((HEAD_DIM, A_PAIR), BF16)
    pen_a = jnp.where(i >= 1, 0.0, NEG_INF).astype(F32)

    def key_block(q, p):
        g = q + p - B_LOOKBACK
        return (True, slice((g + B_LOOKBACK) * TQ, (g + B_LOOKBACK + 1) * TQ)) if g < 0 else \
            (False, slice(g * TQ, (g + 1) * TQ))


    b_parts = {}
    sumsq_b = [jnp.zeros((SUBLANES, TQ), F32) for _ in range(Q_PER_STEP)]
    sumsq_a = [jnp.zeros((SUBLANES, A_PAIR), F32) for _ in range(TS // A_PAIR)]

    def rowgroup_sumsq(t):
        return jnp.sum((t * t).reshape(t.shape[0] // SUBLANES, SUBLANES, t.shape[1]), axis=0)

    def inv_rms(sumsq, width):
        return lax.rsqrt(jnp.sum(sumsq, axis=0, keepdims=True) / width + EPS)

    def b_scores(q, h, p):
        hp, half = divmod(h, 2)
        qt = ft_ref[0, FT_QB + h * HEAD_DIM:FT_QB + (h + 1) * HEAD_DIM, q * TQ:(q + 1) * TQ]
        qm = jnp.concatenate([qt, zeros_q] if half == 0 else [zeros_q, qt], axis=0)
        in_prev, toks = key_block(q, p)
        kp = (kbp_ref if in_prev else kbc_ref)[0, toks, hp * LANES:(hp + 1) * LANES]
        raw = _dot(kp, qm)
        tiles = {}
        for rt, lt in B_LIVE_TILES[p]:
            ks = slice(rt * A_PAIR, (rt + 1) * A_PAIR)
            qs = slice(lt * LANES, (lt + 1) * LANES)
            tiles[rt, lt] = raw[ks, qs] + bias_b_ref[h, p * TQ + rt * A_PAIR:p * TQ + (rt + 1) * A_PAIR, qs]
        mcols = []
        for lt in range(TQ // LANES):
            col = functools.reduce(jnp.maximum, [t for (_, l2), t in tiles.items() if l2 == lt])
            mcols.append(jnp.max(col, axis=0, keepdims=True))
        return tiles, mcols

    def b_output(q, h, p, tiles, mcols):
        rows = slice(h * HEAD_DIM, (h + 1) * HEAD_DIM)
        tokq = slice(q * TQ, (q + 1) * TQ)
        dead = jnp.zeros((A_PAIR, LANES), BF16)
        pt = jnp.concatenate([
            jnp.concatenate([jnp.exp2(tiles[rt, lt] - mcols[lt]).astype(BF16) if (rt, lt) in tiles else dead
                             for rt in range(TQ // A_PAIR)], axis=0)
            for lt in range(TQ // LANES)], axis=1)
        mp = jnp.concatenate(mcols, axis=1)
        in_prev, toks = key_block(q, p)
        if in_prev:
            mp = jnp.where(i == 0, NEG_INF, mp)
            vh = vbp_ref[0, rows, toks]
        else:
            vh = ft_ref[0, FT_VB + h * HEAD_DIM:FT_VB + (h + 1) * HEAD_DIM, toks]
        vt = jnp.concatenate([vh, ones_k], axis=0)
        b_parts.setdefault((q, h), []).append((_dot(vt, pt)[:HEAD_DIM + 8], mp))
        if p == B_BLOCKS - 1:
            parts = b_parts.pop((q, h))
            m = functools.reduce(jnp.maximum, [mq for _, mq in parts])
            ot = sum(op * jnp.exp2(mq - m) for op, mq in parts)
            yt = ot[:HEAD_DIM] * (1.0 / ot[HEAD_DIM:HEAD_DIM + 1])
            sumsq_b[q] = sumsq_b[q] + rowgroup_sumsq(yt)
            yb_scr[rows, tokq] = yt * gb_ref[rows, :]

    def a_windows(r):
        if r == 0:
            kvwin = jnp.concatenate([kvap_ref[0], ft_ref[0, FT_KVA:, :A_PAIR]], axis=1)
        else:
            kvwin = ft_ref[0, FT_KVA:, (r - 1) * A_PAIR:(r + 1) * A_PAIR]
        return kvwin[:KA_W], kvwin[KA_W:]

    def a_scores(r, kvh):
        tok = slice(r * A_PAIR, (r + 1) * A_PAIR)
        kwin, _ = a_windows(r)
        blocks = []
        for g in range(A_GROUP):
            h = kvh * A_GROUP + g
            qt = ft_ref[0, h * HEAD_DIM:(h + 1) * HEAD_DIM, tok]
            blocks.append(jnp.concatenate([qt, zeros_p] if kvh == 0 else [zeros_p, qt], axis=0))
        qst = jnp.concatenate(blocks, axis=1)
        st = _dot_tn(kwin, qst) + bias_a_ref[kvh]
        s0 = st[:A_PAIR]
        s1 = st[A_PAIR:]
        if r == 0:
            s0 = s0 + pen_a
        m = jnp.maximum(jnp.max(jnp.maximum(s0, s1), axis=0, keepdims=True), sink_ref[kvh])
        return (s0, s1), m

    def a_output(r, kvh, st, m):
        tok = slice(r * A_PAIR, (r + 1) * A_PAIR)
        _, vwin = a_windows(r)
        pt = jnp.concatenate([jnp.exp2(st[0] - m), jnp.exp2(st[1] - m)], axis=0).astype(BF16)
        vt = jnp.concatenate([vwin[kvh * HEAD_DIM:(kvh + 1) * HEAD_DIM, :], ones_k], axis=0)
        ot = _dot(vt, pt)
        den = ot[HEAD_DIM:HEAD_DIM + 1] + jnp.exp2(sink_ref[kvh] - m)
        yt = ot[:HEAD_DIM] * (1.0 / den)
        for g in range(A_GROUP):
            rows = slice((kvh * A_GROUP + g) * HEAD_DIM, (kvh * A_GROUP + g + 1) * HEAD_DIM)
            yh = yt[:, g * A_PAIR:(g + 1) * A_PAIR]
            sumsq_a[r] = sumsq_a[r] + rowgroup_sumsq(yh)
            ya_scr[rows, tok] = yh * ga_ref[rows, :]

    def b_finish(q):
        tokq = slice(q * TQ, (q + 1) * TQ)
        y_ref[0, QA_W:, tokq] = (yb_scr[:, tokq] * inv_rms(sumsq_b[q], QB_W)).astype(BF16)

    def a_finish(r):
        tok = slice(r * A_PAIR, (r + 1) * A_PAIR)
        y_ref[0, :QA_W, tok] = (ya_scr[:, tok] * inv_rms(sumsq_a[r], QA_W)).astype(BF16)

    units = []
    for q in range(Q_PER_STEP):
        units += [(b_scores, b_output, (q, h, p),
                   functools.partial(b_finish, q) if (h, p) == (B_HEADS - 1, B_BLOCKS - 1) else None)
                  for h in range(B_HEADS) for p in range(B_BLOCKS)]
        units += [(a_scores, a_output, (r, kvh), functools.partial(a_finish, r) if kvh == A_KV_HEADS - 1 else None)
                  for r in range(q * TQ // A_PAIR, (q + 1) * TQ // A_PAIR) for kvh in range(A_KV_HEADS)]
    pending = []

    def run_output():
        output, args, staged, finish = pending.pop(0)
        output(*args, *staged)
        if finish is not None:
            finish()

    for scores, output, args, finish in units:
        pending.append((output, args, scores(*args), finish))
        if len(pending) > PIPE_DEPTH:
            run_output()
    while pending:
        run_output()


def _attn_call(kb, ft, bias_a, sink_a, bias_b, ga, gb):
    b, s, _ = kb.shape
    assert FT_VB % QB_W == 0 and FT_KVA % (2 * KA_W) == 0
    look = B_LOOKBACK * TQ
    assert TS % look == 0
    pairs_per_step = TS // A_PAIR

    def look_block(i):
        return jnp.maximum(i * (TS // look) - 1, 0)

    in_specs = [
        pl.BlockSpec((1, FT_ROWS, TS), lambda bi, i: (bi, 0, i)),
        pl.BlockSpec((1, look, QB_W), lambda bi, i: (bi, look_block(i), 0)),
        pl.BlockSpec((1, TS, QB_W), lambda bi, i: (bi, i, 0)),
        pl.BlockSpec((1, QB_W, look), lambda bi, i: (bi, FT_VB // QB_W, look_block(i))),
        pl.BlockSpec((1, 2 * KA_W, A_PAIR),
                     lambda bi, i: (bi, FT_KVA // (2 * KA_W), jnp.maximum(pairs_per_step * i - 1, 0))),
        _const_spec(bias_a.shape), _const_spec(sink_a.shape), _const_spec(bias_b.shape),
        _const_spec(ga.shape), _const_spec(gb.shape),
    ]
    return pl.pallas_call(
        _attn_kernel,
        grid=(b, s // TS),
        in_specs=in_specs,
        out_specs=pl.BlockSpec((1, MIX_W, TS), lambda bi, i: (bi, 0, i)),
        out_shape=jax.ShapeDtypeStruct((b, MIX_W, s), BF16),
        scratch_shapes=[pltpu.VMEM((QA_W, TS), F32), pltpu.VMEM((QB_W, TS), F32)],
        compiler_params=pltpu.CompilerParams(
            dimension_semantics=("arbitrary", "arbitrary"), vmem_limit_bytes=VMEM_LIMIT),
        name="attention",
    )(ft, kb, kb, ft, ft, bias_a, sink_a, bias_b, ga, gb)


def _ffn_kernel(yt_ref, x_ref, wo_ref, g2_ref, w1_ref, w2_ref, gf_ref, o_ref, h_scr, n2_scr):
    h = x_ref[0] + _dot_tn(yt_ref[0], wo_ref[...])
    h_scr[...] = h
    n2_scr[...] = _rms(h, g2_ref[...]).astype(BF16)

    def mlp_chunk(rows, c):
        cols = slice(c * FF_CHUNK, (c + 1) * FF_CHUNK)
        u = _dot(n2_scr[rows, :], w1_ref[:, cols])
        u = jnp.square(jnp.maximum(u, 0.0)).astype(BF16)
        return _dot(u, w2_ref[cols, :])

    last = D_FF // FF_CHUNK - 1
    for c in range(last):
        h_scr[...] += mlp_chunk(slice(None), c)
    for blk in range(TM_FFN // FFN_TAIL_ROWS):
        rows = slice(blk * FFN_TAIL_ROWS, (blk + 1) * FFN_TAIL_ROWS)
        o_ref[0, rows, :] = _rms(h_scr[rows, :] + mlp_chunk(rows, last), gf_ref[...])


def _ffn_call(yt, x, wo, g2, w1, w2, gf):
    b, s, _ = x.shape
    return pl.pallas_call(
        _ffn_kernel,
        grid=(b, s // TM_FFN),
        in_specs=[
            pl.BlockSpec((1, MIX_W, TM_FFN), lambda bi, i: (bi, 0, i)),
            pl.BlockSpec((1, TM_FFN, D_MODEL), lambda bi, i: (bi, i, 0)),
            _const_spec(wo.shape), _const_spec(g2.shape),
            _const_spec(w1.shape), _const_spec(w2.shape), _const_spec(gf.shape),
        ],
        out_specs=pl.BlockSpec((1, TM_FFN, D_MODEL), lambda bi, i: (bi, i, 0)),
        out_shape=jax.ShapeDtypeStruct((b, s, D_MODEL), F32),
        scratch_shapes=[pltpu.VMEM((TM_FFN, D_MODEL), F32), pltpu.VMEM((TM_FFN, D_MODEL), BF16)],
        compiler_params=pltpu.CompilerParams(
            dimension_semantics=("arbitrary", "arbitrary"), vmem_limit_bytes=VMEM_LIMIT),
        name="out_proj_mlp",
    )(yt, x, wo, g2, w1, w2, gf)


def _bias_a_table(sinks):
    k = np.arange(A_WIN)[:, None]
    i = np.arange(A_PAIR)[None, :]
    dist = np.abs(A_PAIR + i - k).astype(np.float32)
    qc = i // CHUNK
    kc = k // CHUNK
    allowed = (kc >= qc) & (kc <= qc + A_BAND_CHUNKS - 1)
    slopes = jnp.exp2(-8.0 * (jnp.arange(A_HEADS, dtype=F32) + 1.0) / A_HEADS)
    bias = -slopes[:, None, None] * jnp.asarray(dist)[None] * LOG2E
    bias = jnp.where(jnp.asarray(allowed)[None], bias, NEG_INF)
    bias = bias.reshape(A_KV_HEADS, A_GROUP, A_WIN, A_PAIR).transpose(0, 2, 1, 3)
    bias = bias.reshape(A_KV_HEADS, A_WIN, A_GROUP * A_PAIR)
    sink = jnp.broadcast_to((sinks.astype(F32) * LOG2E).reshape(A_KV_HEADS, 1, A_GROUP, 1),
                            (A_KV_HEADS, 1, A_GROUP, A_PAIR)).reshape(A_KV_HEADS, 1, A_GROUP * A_PAIR)
    return bias, sink


def _bias_b_table(rel_bias):
    rb = rel_bias.astype(F32) * LOG2E
    base = (B_BAND_CHUNKS - 1) * CHUNK

    def tile(delta):
        lo = base + LANES * delta - (LANES - 1)
        if lo >= B_MAX_REL:
            return jnp.broadcast_to(rb[:, -1][:, None, None], (B_HEADS, LANES, LANES))
        off = np.arange(2 * LANES)
        off = np.where(off >= LANES, off - 2 * LANES, off)
        rel = np.clip(base + LANES * delta + off, -B_MAX_REL, B_MAX_REL) + B_MAX_REL
        row = rb[:, jnp.asarray(rel)]
        flat = jnp.tile(row, (1, LANES))[:, :LANES * (2 * LANES - 1)]
        return flat.reshape(B_HEADS, LANES, 2 * LANES - 1)[:, :, :LANES]

    tiles = {delta: tile(delta) for delta in range(-(B_WIN // LANES - 1), TQ // LANES)}
    bias = jnp.concatenate(
        [jnp.concatenate([tiles[t - a] for t in range(TQ // LANES)], axis=2) for a in range(B_WIN // LANES)],
        axis=1)
    k = np.arange(B_WIN)[:, None]
    q = np.arange(TQ)[None, :]
    qc = q // CHUNK
    kc = k // CHUNK
    allowed = (kc >= qc) & (kc <= qc + B_BAND_CHUNKS - 1)
    return jnp.where(jnp.asarray(allowed)[None], bias, NEG_INF)


def kernel(x, norm1_g, w_in, sinks_a, rel_bias_b, out_norm_a_g, out_norm_b_g, w_out, norm2_g,
           w_ff1, w_ff2, final_norm_g):
    b, s, d = x.shape
    assert d == D_MODEL and s % TM_PROJ == 0 and s % TM_FFN == 0 and s % TS == 0
    assert norm1_g.shape[0] == 1, "single-layer block"
    assert w_in.shape[2] == 2 * QA_W + 2 * KA_W + 2 * QB_W
    kb, ft, wo, w1, w2 = _proj_call(
        x, norm1_g[0].reshape(1, d), w_in[0], w_out[0], w_ff1[0], w_ff2[0])

    bias_a, sink_a = _bias_a_table(sinks_a[0])
    bias_b = _bias_b_table(rel_bias_b[0])
    ga = jnp.broadcast_to(out_norm_a_g[0].astype(F32)[:, None], (QA_W, A_PAIR))
    gb = jnp.broadcast_to(out_norm_b_g[0].astype(F32)[:, None], (QB_W, TQ))
    yt = _attn_call(kb, ft, bias_a, sink_a, bias_b, ga, gb)

    return _ffn_call(yt, x, wo, norm2_g[0].reshape(1, d), w1, w2, final_norm_g.reshape(1, d))
```

```python
import functools

import jax
import jax.numpy as jnp
import numpy as np
from jax import lax
from jax.experimental import pallas as pl
from jax.experimental.pallas import tpu as pltpu

D_MODEL = 1024
CHUNK = 64
HEAD_DIM = 64
A_HEADS = 8
A_KV_HEADS = 2
A_GROUP = A_HEADS // A_KV_HEADS
A_BAND_CHUNKS = 3
B_HEADS = 8
B_BAND_CHUNKS = 9
B_MAX_REL = 128
D_FF = 4 * D_MODEL
EPS = 1e-6
NEG_INF = -1e30
LOG2E = 1.4426950408889634

QA_W = A_HEADS * HEAD_DIM
KA_W = A_KV_HEADS * HEAD_DIM
QB_W = B_HEADS * HEAD_DIM
MIX_W = QA_W + QB_W
FT_QB, FT_VB, FT_KVA, FT_ROWS = QA_W, QA_W + QB_W, QA_W + 2 * QB_W, QA_W + 2 * QB_W + 2 * KA_W

LANES = 128
SUBLANES = 8
TQ = 256
Q_PER_STEP = 2
TS = Q_PER_STEP * TQ
A_PAIR = 2 * CHUNK
A_WIN = 4 * CHUNK
B_BLOCKS = 3
B_WIN = B_BLOCKS * TQ
B_LOOKBACK = B_BLOCKS - 1
ONES_ROWS = 16
PIPE_DEPTH = 7
TM_PROJ = 1024
PROJ_SUB = 512
TM_FFN = 1024
FF_CHUNK = 1024
FFN_TAIL_ROWS = 256
V7X_VMEM_BYTES = 64 * 1024 * 1024
VMEM_LIMIT = V7X_VMEM_BYTES * 7 // 8

F32 = jnp.float32
BF16 = jnp.bfloat16


def _b_live_tiles():
    chunks_per_tile = A_PAIR // CHUNK
    live = []
    for p in range(B_BLOCKS):
        tiles = []
        for rt in range(TQ // A_PAIR):
            for lt in range(TQ // LANES):
                kcs = [p * (TQ // CHUNK) + rt * chunks_per_tile + i for i in range(chunks_per_tile)]
                qcs = [lt * (LANES // CHUNK) + i for i in range(LANES // CHUNK)]
                if any(qc <= kc <= qc + B_BAND_CHUNKS - 1 for kc in kcs for qc in qcs):
                    tiles.append((rt, lt))
        live.append(tuple(tiles))
    return tuple(live)


B_LIVE_TILES = _b_live_tiles()


def _rms(x, g):
    ms = jnp.mean(x * x, axis=-1, keepdims=True)
    return x * lax.rsqrt(ms + EPS) * g


def _rms_rows(xt, g):
    ms = jnp.mean(xt * xt, axis=0, keepdims=True)
    return xt * lax.rsqrt(ms + EPS) * g


def _dot(a, b):
    return jnp.dot(a, b, preferred_element_type=F32)


def _dot_nt(a, b):
    return lax.dot_general(a, b, (((1,), (1,)), ((), ())), preferred_element_type=F32)


def _dot_tn(a, b):
    return lax.dot_general(a, b, (((0,), (0,)), ((), ())), preferred_element_type=F32)


def _const_spec(shape):
    nd = len(shape)
    return pl.BlockSpec(shape, lambda *_: (0,) * nd, pipeline_mode=pl.Buffered(1))


def _proj_kernel(x_ref, g_ref, win_ref, wo_ref, w1_ref, w2_ref,
                 kb_ref, ft_ref, wo_bf_ref, w1_bf_ref, w2_bf_ref,
                 wk_scr, wt_scr):
    @pl.when((pl.program_id(0) == 0) & (pl.program_id(1) == 0))
    def _prepare_weights():
        o_ka, o_qb, o_kb, o_vb = QA_W, QA_W + 2 * KA_W, QA_W + 2 * KA_W + QB_W, QA_W + 2 * KA_W + 2 * QB_W
        q_scale = HEAD_DIM ** -0.5 * LOG2E
        wk_scr[...] = win_ref[:, o_kb:o_vb].astype(BF16)
        row = 0
        for lo, hi, scale in ((0, o_ka, q_scale), (o_qb, o_kb, q_scale), (o_vb, o_vb + QB_W, None),
                              (o_ka, o_qb, None)):
            part = win_ref[:, lo:hi]
            if scale is not None:
                part = part * scale
            wt_scr[:, row:row + hi - lo] = part.astype(BF16)
            row += hi - lo

    for i in range(TM_PROJ // PROJ_SUB):
        tok = slice(i * PROJ_SUB, (i + 1) * PROJ_SUB)
        n = _rms(x_ref[0, tok, :], g_ref[...]).astype(BF16)
        kb_ref[0, tok, :] = _dot(n, wk_scr[...]).astype(BF16)
        ft_ref[0, :, tok] = _dot(n, wt_scr[...]).T.astype(BF16)
    wo_bf_ref[...] = wo_ref[...].astype(BF16)
    w1_bf_ref[...] = w1_ref[...].astype(BF16)
    w2_bf_ref[...] = w2_ref[...].astype(BF16)


def _proj_call(x, g, w_in, wo, w1, w2):
    b, s, _ = x.shape
    steps = b * (s // TM_PROJ)

    def tok(width):
        return pl.BlockSpec((1, TM_PROJ, width), lambda bi, i: (bi, i, 0))

    def feat(width):
        return pl.BlockSpec((1, width, TM_PROJ), lambda bi, i: (bi, 0, i))

    def row_share(w):
        rows = w.shape[0] // steps
        assert rows * steps == w.shape[0] and rows % 16 == 0, "weight rows must split into bf16 row tiles"
        return pl.BlockSpec((rows, w.shape[1]), lambda bi, i: (bi * (s // TM_PROJ) + i, 0))

    def bf(w):
        return jax.ShapeDtypeStruct(w.shape, BF16)

    return pl.pallas_call(
        _proj_kernel,
        grid=(b, s // TM_PROJ),
        in_specs=[tok(D_MODEL), _const_spec(g.shape), _const_spec(w_in.shape),
                  row_share(wo), row_share(w1), row_share(w2)],
        out_specs=[tok(QB_W), feat(FT_ROWS), row_share(wo), row_share(w1), row_share(w2)],
        out_shape=[jax.ShapeDtypeStruct((b, s, QB_W), BF16), jax.ShapeDtypeStruct((b, FT_ROWS, s), BF16),
                   bf(wo), bf(w1), bf(w2)],
        scratch_shapes=[pltpu.VMEM((D_MODEL, QB_W), BF16),
                        pltpu.VMEM((D_MODEL, w_in.shape[1] - QB_W), BF16)],
        compiler_params=pltpu.CompilerParams(
            dimension_semantics=("arbitrary", "arbitrary"), vmem_limit_bytes=VMEM_LIMIT),
        name="norm_in_proj",
    )(x, g, w_in, wo, w1, w2)


def _attn_kernel(ft_ref, kbp_ref, kbc_ref, vbp_ref, kvap_ref,
                 bias_a_ref, sink_ref, bias_b_ref, ga_ref, gb_ref,
                 y_ref, ya_scr, yb_scr):
    i = pl.program_id(1)
    zeros_q = jnp.zeros((HEAD_DIM, TQ), BF16)
    ones_k = jnp.ones((ONES_ROWS, TQ), BF16)
    zeros_p = jnp.zeros((HEAD_DIM, A_PAIR), BF16)
    pen_a = jnp.where(i >= 1, 0.0, NEG_INF).astype(F32)

    def key_block(q, p):
        g = q + p - B_LOOKBACK
        return (True, slice((g + B_LOOKBACK) * TQ, (g + B_LOOKBACK + 1) * TQ)) if g < 0 else \
            (False, slice(g * TQ, (g + 1) * TQ))


    b_parts = {}
    sumsq_b = [jnp.zeros((SUBLANES, TQ), F32) for _ in range(Q_PER_STEP)]
    sumsq_a = [jnp.zeros((SUBLANES, A_PAIR), F32) for _ in range(TS // A_PAIR)]

    def rowgroup_sumsq(t):
        return jnp.sum((t * t).reshape(t.shape[0] // SUBLANES, SUBLANES, t.shape[1]), axis=0)

    def inv_rms(sumsq, width):
        return lax.rsqrt(jnp.sum(sumsq, axis=0, keepdims=True) / width + EPS)

    def b_scores(q, h, p):
        hp, half = divmod(h, 2)
        qt = ft_ref[0, FT_QB + h * HEAD_DIM:FT_QB + (h + 1) * HEAD_DIM, q * TQ:(q + 1) * TQ]
        qm = jnp.concatenate([qt, zeros_q] if half == 0 else [zeros_q, qt], axis=0)
        in_prev, toks = key_block(q, p)
        kp = (kbp_ref if in_prev else kbc_ref)[0, toks, hp * LANES:(hp + 1) * LANES]
        raw = _dot(kp, qm)
        tiles = {}
        for rt, lt in B_LIVE_TILES[p]:
            ks = slice(rt * A_PAIR, (rt + 1) * A_PAIR)
            qs = slice(lt * LANES, (lt + 1) * LANES)
            tiles[rt, lt] = raw[ks, qs] + bias_b_ref[h, p * TQ + rt * A_PAIR:p * TQ + (rt + 1) * A_PAIR, qs]
        mcols = []
        for lt in range(TQ // LANES):
            col = functools.reduce(jnp.maximum, [t for (_, l2), t in tiles.items() if l2 == lt])
            mcols.append(jnp.max(col, axis=0, keepdims=True))
        return tiles, mcols

    def b_output(q, h, p, tiles, mcols):
        rows = slice(h * HEAD_DIM, (h + 1) * HEAD_DIM)
        tokq = slice(q * TQ, (q + 1) * TQ)
        dead = jnp.zeros((A_PAIR, LANES), BF16)
        pt = jnp.concatenate([
            jnp.concatenate([jnp.exp2(tiles[rt, lt] - mcols[lt]).astype(BF16) if (rt, lt) in tiles else dead
                             for rt in range(TQ // A_PAIR)], axis=0)
            for lt in range(TQ // LANES)], axis=1)
        mp = jnp.concatenate(mcols, axis=1)
        in_prev, toks = key_block(q, p)
        if in_prev:
            mp = jnp.where(i == 0, NEG_INF, mp)
            vh = vbp_ref[0, rows, toks]
        else:
            vh = ft_ref[0, FT_VB + h * HEAD_DIM:FT_VB + (h + 1) * HEAD_DIM, toks]
        vt = jnp.concatenate([vh, ones_k], axis=0)
        b_parts.setdefault((q, h), []).append((_dot(vt, pt)[:HEAD_DIM + 8], mp))
        if p == B_BLOCKS - 1:
            parts = b_parts.pop((q, h))
            m = functools.reduce(jnp.maximum, [mq for _, mq in parts])
            ot = sum(op * jnp.exp2(mq - m) for op, mq in parts)
            yt = ot[:HEAD_DIM] * (1.0 / ot[HEAD_DIM:HEAD_DIM + 1])
            sumsq_b[q] = sumsq_b[q] + rowgroup_sumsq(yt)
            yb_scr[rows, tokq] = yt * gb_ref[rows, :]

    def a_windows(r):
        if r == 0:
            kvwin = jnp.concatenate([kvap_ref[0], ft_ref[0, FT_KVA:, :A_PAIR]], axis=1)
        else:
            kvwin = ft_ref[0, FT_KVA:, (r - 1) * A_PAIR:(r + 1) * A_PAIR]
        return kvwin[:KA_W], kvwin[KA_W:]

    def a_scores(r, kvh):
        tok = slice(r * A_PAIR, (r + 1) * A_PAIR)
        kwin, _ = a_windows(r)
        blocks = []
        for g in range(A_GROUP):
            h = kvh * A_GROUP + g
            qt = ft_ref[0, h * HEAD_DIM:(h + 1) * HEAD_DIM, tok]
            blocks.append(jnp.concatenate([qt, zeros_p] if kvh == 0 else [zeros_p, qt], axis=0))
        qst = jnp.concatenate(blocks, axis=1)
        st = _dot_tn(kwin, qst) + bias_a_ref[kvh]
        s0 = st[:A_PAIR]
        s1 = st[A_PAIR:]
        if r == 0:
            s0 = s0 + pen_a
        m = jnp.maximum(jnp.max(jnp.maximum(s0, s1), axis=0, keepdims=True), sink_ref[kvh])
        return (s0, s1), m

    def a_output(r, kvh, st, m):
        tok = slice(r * A_PAIR, (r + 1) * A_PAIR)
        _, vwin = a_windows(r)
        pt = jnp.concatenate([jnp.exp2(st[0] - m), jnp.exp2(st[1] - m)], axis=0).astype(BF16)
        vt = jnp.concatenate([vwin[kvh * HEAD_DIM:(kvh + 1) * HEAD_DIM, :], ones_k], axis=0)
        ot = _dot(vt, pt)
        den = ot[HEAD_DIM:HEAD_DIM + 1] + jnp.exp2(sink_ref[kvh] - m)
        yt = ot[:HEAD_DIM] * (1.0 / den)
        for g in range(A_GROUP):
            rows = slice((kvh * A_GROUP + g) * HEAD_DIM, (kvh * A_GROUP + g + 1) * HEAD_DIM)
            yh = yt[:, g * A_PAIR:(g + 1) * A_PAIR]
            sumsq_a[r] = sumsq_a[r] + rowgroup_sumsq(yh)
            ya_scr[rows, tok] = yh * ga_ref[rows, :]

    def b_finish(q):
        tokq = slice(q * TQ, (q + 1) * TQ)
        y_ref[0, QA_W:, tokq] = (yb_scr[:, tokq] * inv_rms(sumsq_b[q], QB_W)).astype(BF16)

    def a_finish(r):
        tok = slice(r * A_PAIR, (r + 1) * A_PAIR)
        y_ref[0, :QA_W, tok] = (ya_scr[:, tok] * inv_rms(sumsq_a[r], QA_W)).astype(BF16)

    units = []
    for q in range(Q_PER_STEP):
        units += [(b_scores, b_output, (q, h, p),
                   functools.partial(b_finish, q) if (h, p) == (B_HEADS - 1, B_BLOCKS - 1) else None)
                  for h in range(B_HEADS) for p in range(B_BLOCKS)]
        units += [(a_scores, a_output, (r, kvh), functools.partial(a_finish, r) if kvh == A_KV_HEADS - 1 else None)
                  for r in range(q * TQ // A_PAIR, (q + 1) * TQ // A_PAIR) for kvh in range(A_KV_HEADS)]
    pending = []

    def run_output():
        output, args, staged, finish = pending.pop(0)
        output(*args, *staged)
        if finish is not None:
            finish()

    for scores, output, args, finish in units:
        pending.append((output, args, scores(*args), finish))
        if len(pending) > PIPE_DEPTH:
            run_output()
    while pending:
        run_output()


def _attn_call(kb, ft, bias_a, sink_a, bias_b, ga, gb):
    b, s, _ = kb.shape
    assert FT_VB % QB_W == 0 and FT_KVA % (2 * KA_W) == 0
    look = B_LOOKBACK * TQ
    assert TS % look == 0
    pairs_per_step = TS // A_PAIR

    def look_block(i):
        return jnp.maximum(i * (TS // look) - 1, 0)

    in_specs = [
        pl.BlockSpec((1, FT_ROWS, TS), lambda bi, i: (bi, 0, i)),
        pl.BlockSpec((1, look, QB_W), lambda bi, i: (bi, look_block(i), 0)),
        pl.BlockSpec((1, TS, QB_W), lambda bi, i: (bi, i, 0)),
        pl.BlockSpec((1, QB_W, look), lambda bi, i: (bi, FT_VB // QB_W, look_block(i))),
        pl.BlockSpec((1, 2 * KA_W, A_PAIR),
                     lambda bi, i: (bi, FT_KVA // (2 * KA_W), jnp.maximum(pairs_per_step * i - 1, 0))),
        _const_spec(bias_a.shape), _const_spec(sink_a.shape), _const_spec(bias_b.shape),
        _const_spec(ga.shape), _const_spec(gb.shape),
    ]
    return pl.pallas_call(
        _attn_kernel,
        grid=(b, s // TS),
        in_specs=in_specs,
        out_specs=pl.BlockSpec((1, MIX_W, TS), lambda bi, i: (bi, 0, i)),
        out_shape=jax.ShapeDtypeStruct((b, MIX_W, s), BF16),
        scratch_shapes=[pltpu.VMEM((QA_W, TS), F32), pltpu.VMEM((QB_W, TS), F32)],
        compiler_params=pltpu.CompilerParams(
            dimension_semantics=("arbitrary", "arbitrary"), vmem_limit_bytes=VMEM_LIMIT),
        name="attention",
    )(ft, kb, kb, ft, ft, bias_a, sink_a, bias_b, ga, gb)


def _ffn_kernel(yt_ref, x_ref, wo_ref, g2_ref, w1_ref, w2_ref, gf_ref, o_ref, h_scr, n2_scr):
    h = x_ref[0] + _dot_tn(yt_ref[0], wo_ref[...])
    h_scr[...] = h
    n2_scr[...] = _rms(h, g2_ref[...]).astype(BF16)

    def mlp_chunk(rows, c):
        cols = slice(c * FF_CHUNK, (c + 1) * FF_CHUNK)
        u = _dot(n2_scr[rows, :], w1_ref[:, cols])
        u = jnp.square(jnp.maximum(u, 0.0)).astype(BF16)
        return _dot(u, w2_ref[cols, :])

    last = D_FF // FF_CHUNK - 1
    for c in range(last):
        h_scr[...] += mlp_chunk(slice(None), c)
    for blk in range(TM_FFN // FFN_TAIL_ROWS):
        rows = slice(blk * FFN_TAIL_ROWS, (blk + 1) * FFN_TAIL_ROWS)
        o_ref[0, rows, :] = _rms(h_scr[rows, :] + mlp_chunk(rows, last), gf_ref[...])


def _ffn_call(yt, x, wo, g2, w1, w2, gf):
    b, s, _ = x.shape
    return pl.pallas_call(
        _ffn_kernel,
        grid=(b, s // TM_FFN),
        in_specs=[
            pl.BlockSpec((1, MIX_W, TM_FFN), lambda bi, i: (bi, 0, i)),
            pl.BlockSpec((1, TM_FFN, D_MODEL), lambda bi, i: (bi, i, 0)),
            _const_spec(wo.shape), _const_spec(g2.shape),
            _const_spec(w1.shape), _const_spec(w2.shape), _const_spec(gf.shape),
        ],
        out_specs=pl.BlockSpec((1, TM_FFN, D_MODEL), lambda bi, i: (bi, i, 0)),
        out_shape=jax.ShapeDtypeStruct((b, s, D_MODEL), F32),
        scratch_shapes=[pltpu.VMEM((TM_FFN, D_MODEL), F32), pltpu.VMEM((TM_FFN, D_MODEL), BF16)],
        compiler_params=pltpu.CompilerParams(
            dimension_semantics=("arbitrary", "arbitrary"), vmem_limit_bytes=VMEM_LIMIT),
        name="out_proj_mlp",
    )(yt, x, wo, g2, w1, w2, gf)


def _bias_a_table(sinks):
    k = np.arange(A_WIN)[:, None]
    i = np.arange(A_PAIR)[None, :]
    dist = np.abs(A_PAIR + i - k).astype(np.float32)
    qc = i // CHUNK
    kc = k // CHUNK
    allowed = (kc >= qc) & (kc <= qc + A_BAND_CHUNKS - 1)
    slopes = jnp.exp2(-8.0 * (jnp.arange(A_HEADS, dtype=F32) + 1.0) / A_HEADS)
    bias = -slopes[:, None, None] * jnp.asarray(dist)[None] * LOG2E
    bias = jnp.where(jnp.asarray(allowed)[None], bias, NEG_INF)
    bias = bias.reshape(A_KV_HEADS, A_GROUP, A_WIN, A_PAIR).transpose(0, 2, 1, 3)
    bias = bias.reshape(A_KV_HEADS, A_WIN, A_GROUP * A_PAIR)
    sink = jnp.broadcast_to((sinks.astype(F32) * LOG2E).reshape(A_KV_HEADS, 1, A_GROUP, 1),
                            (A_KV_HEADS, 1, A_GROUP, A_PAIR)).reshape(A_KV_HEADS, 1, A_GROUP * A_PAIR)
    return bias, sink


def _bias_b_table(rel_bias):
    rb = rel_bias.astype(F32) * LOG2E
    base = (B_BAND_CHUNKS - 1) * CHUNK

    def tile(delta):
        lo = base + LANES * delta - (LANES - 1)
        if lo >= B_MAX_REL:
            return jnp.broadcast_to(rb[:, -1][:, None, None], (B_HEADS, LANES, LANES))
        off = np.arange(2 * LANES)
        off = np.where(off >= LANES, off - 2 * LANES, off)
        rel = np.clip(base + LANES * delta + off, -B_MAX_REL, B_MAX_REL) + B_MAX_REL
        row = rb[:, jnp.asarray(rel)]
        flat = jnp.tile(row, (1, LANES))[:, :LANES * (2 * LANES - 1)]
        return flat.reshape(B_HEADS, LANES, 2 * LANES - 1)[:, :, :LANES]

    tiles = {delta: tile(delta) for delta in range(-(B_WIN // LANES - 1), TQ // LANES)}
    bias = jnp.concatenate(
        [jnp.concatenate([tiles[t - a] for t in range(TQ // LANES)], axis=2) for a in range(B_WIN // LANES)],
        axis=1)
    k = np.arange(B_WIN)[:, None]
    q = np.arange(TQ)[None, :]
    qc = q // CHUNK
    kc = k // CHUNK
    allowed = (kc >= qc) & (kc <= qc + B_BAND_CHUNKS - 1)
    return jnp.where(jnp.asarray(allowed)[None], bias, NEG_INF)


def kernel(x, norm1_g, w_in, sinks_a, rel_bias_b, out_norm_a_g, out_norm_b_g, w_out, norm2_g,
           w_ff1, w_ff2, final_norm_g):
    b, s, d = x.shape
    assert d == D_MODEL and s % TM_PROJ == 0 and s % TM_FFN == 0 and s % TS == 0
    assert norm1_g.shape[0] == 1, "single-layer block"
    assert w_in.shape[2] == 2 * QA_W + 2 * KA_W + 2 * QB_W
    kb, ft, wo, w1, w2 = _proj_call(
        x, norm1_g[0].reshape(1, d), w_in[0], w_out[0], w_ff1[0], w_ff2[0])

    bias_a, sink_a = _bias_a_table(sinks_a[0])
    bias_b = _bias_b_table(rel_bias_b[0])
    ga = jnp.broadcast_to(out_norm_a_g[0].astype(F32)[:, None], (QA_W, A_PAIR))
    gb = jnp.broadcast_to(out_norm_b_g[0].astype(F32)[:, None], (QB_W, TQ))
    yt = _attn_call(kb, ft, bias_a, sink_a, bias_b, ga, gb)

    return _ffn_call(yt, x, wo, norm2_g[0].reshape(1, d), w1, w2, final_norm_g.reshape(1, d))
```

```python
import functools

import jax
import jax.numpy as jnp
import numpy as np
from jax import lax
from jax.experimental import pallas as pl
from jax.experimental.pallas import tpu as pltpu

D_MODEL = 1024
CHUNK = 64
HEAD_DIM = 64
A_HEADS = 8
A_KV_HEADS = 2
A_GROUP = A_HEADS // A_KV_HEADS
A_BAND_CHUNKS = 3
B_HEADS = 8
B_BAND_CHUNKS = 9
B_MAX_REL = 128
D_FF = 4 * D_MODEL
EPS = 1e-6
NEG_INF = -1e30
LOG2E = 1.4426950408889634

QA_W = A_HEADS * HEAD_DIM
KA_W = A_KV_HEADS * HEAD_DIM
QB_W = B_HEADS * HEAD_DIM
MIX_W = QA_W + QB_W
FT_QB, FT_VB, FT_KVA, FT_ROWS = QA_W, QA_W + QB_W, QA_W + 2 * QB_W, QA_W + 2 * QB_W + 2 * KA_W

LANES = 128
SUBLANES = 8
TQ = 256
Q_PER_STEP = 2
TS = Q_PER_STEP * TQ
A_PAIR = 2 * CHUNK
A_WIN = 4 * CHUNK
B_BLOCKS = 3
B_WIN = B_BLOCKS * TQ
B_LOOKBACK = B_BLOCKS - 1
ONES_ROWS = 16
PIPE_DEPTH = 7
TM_PROJ = 1024
PROJ_SUB = 512
TM_FFN = 1024
FF_CHUNK = 2048
FFN_TAIL_ROWS = 256
V7X_VMEM_BYTES = 64 * 1024 * 1024
VMEM_LIMIT = V7X_VMEM_BYTES * 7 // 8

F32 = jnp.float32
BF16 = jnp.bfloat16


def _b_live_tiles():
    chunks_per_tile = A_PAIR // CHUNK
    live = []
    for p in range(B_BLOCKS):
        tiles = []
        for rt in range(TQ // A_PAIR):
            for lt in range(TQ // LANES):
                kcs = [p * (TQ // CHUNK) + rt * chunks_per_tile + i for i in range(chunks_per_tile)]
                qcs = [lt * (LANES // CHUNK) + i for i in range(LANES // CHUNK)]
                if any(qc <= kc <= qc + B_BAND_CHUNKS - 1 for kc in kcs for qc in qcs):
                    tiles.append((rt, lt))
        live.append(tuple(tiles))
    return tuple(live)


B_LIVE_TILES = _b_live_tiles()


def _rms(x, g):
    ms = jnp.mean(x * x, axis=-1, keepdims=True)
    return x * lax.rsqrt(ms + EPS) * g


def _rms_rows(xt, g):
    ms = jnp.mean(xt * xt, axis=0, keepdims=True)
    return xt * lax.rsqrt(ms + EPS) * g


def _dot(a, b):
    return jnp.dot(a, b, preferred_element_type=F32)


def _dot_nt(a, b):
    return lax.dot_general(a, b, (((1,), (1,)), ((), ())), preferred_element_type=F32)


def _dot_tn(a, b):
    return lax.dot_general(a, b, (((0,), (0,)), ((), ())), preferred_element_type=F32)


def _const_spec(shape):
    nd = len(shape)
    return pl.BlockSpec(shape, lambda *_: (0,) * nd, pipeline_mode=pl.Buffered(1))


def _proj_kernel(x_ref, g_ref, win_ref, wo_ref, w1_ref, w2_ref,
                 kb_ref, ft_ref, wo_bf_ref, w1_bf_ref, w2_bf_ref,
                 wk_scr, wt_scr):
    @pl.when((pl.program_id(0) == 0) & (pl.program_id(1) == 0))
    def _prepare_weights():
        o_ka, o_qb, o_kb, o_vb = QA_W, QA_W + 2 * KA_W, QA_W + 2 * KA_W + QB_W, QA_W + 2 * KA_W + 2 * QB_W
        q_scale = HEAD_DIM ** -0.5 * LOG2E
        wk_scr[...] = win_ref[:, o_kb:o_vb].astype(BF16)
        row = 0
        for lo, hi, scale in ((0, o_ka, q_scale), (o_qb, o_kb, q_scale), (o_vb, o_vb + QB_W, None),
                              (o_ka, o_qb, None)):
            part = win_ref[:, lo:hi]
            if scale is not None:
                part = part * scale
            wt_scr[row:row + hi - lo, :] = part.T.astype(BF16)
            row += hi - lo

    for i in range(TM_PROJ // PROJ_SUB):
        tok = slice(i * PROJ_SUB, (i + 1) * PROJ_SUB)
        n = _rms(x_ref[0, tok, :], g_ref[...]).astype(BF16)
        kb_ref[0, tok, :] = _dot(n, wk_scr[...]).astype(BF16)
        ft_ref[0, :, tok] = _dot_nt(wt_scr[...], n).astype(BF16)
    wo_bf_ref[...] = wo_ref[...].astype(BF16)
    w1_bf_ref[...] = w1_ref[...].astype(BF16)
    w2_bf_ref[...] = w2_ref[...].astype(BF16)


def _proj_call(x, g, w_in, wo, w1, w2):
    b, s, _ = x.shape
    steps = b * (s // TM_PROJ)

    def tok(width):
        return pl.BlockSpec((1, TM_PROJ, width), lambda bi, i: (bi, i, 0))

    def feat(width):
        return pl.BlockSpec((1, width, TM_PROJ), lambda bi, i: (bi, 0, i))

    def row_share(w):
        rows = w.shape[0] // steps
        assert rows * steps == w.shape[0] and rows % 16 == 0, "weight rows must split into bf16 row tiles"
        return pl.BlockSpec((rows, w.shape[1]), lambda bi, i: (bi * (s // TM_PROJ) + i, 0))

    def bf(w):
        return jax.ShapeDtypeStruct(w.shape, BF16)

    return pl.pallas_call(
        _proj_kernel,
        grid=(b, s // TM_PROJ),
        in_specs=[tok(D_MODEL), _const_spec(g.shape), _const_spec(w_in.shape),
                  row_share(wo), row_share(w1), row_share(w2)],
        out_specs=[tok(QB_W), feat(FT_ROWS), row_share(wo), row_share(w1), row_share(w2)],
        out_shape=[jax.ShapeDtypeStruct((b, s, QB_W), BF16), jax.ShapeDtypeStruct((b, FT_ROWS, s), BF16),
                   bf(wo), bf(w1), bf(w2)],
        scratch_shapes=[pltpu.VMEM((D_MODEL, QB_W), BF16),
                        pltpu.VMEM((w_in.shape[1] - QB_W, D_MODEL), BF16)],
        compiler_params=pltpu.CompilerParams(
            dimension_semantics=("arbitrary", "arbitrary"), vmem_limit_bytes=VMEM_LIMIT),
        name="norm_in_proj",
    )(x, g, w_in, wo, w1, w2)


def _attn_kernel(ft_ref, kbp_ref, kbc_ref, vbp_ref, kvap_ref,
                 bias_a_ref, sink_ref, bias_b_ref, ga_ref, gb_ref,
                 y_ref, ya_scr, yb_scr):
    i = pl.program_id(1)
    zeros_q = jnp.zeros((HEAD_DIM, TQ), BF16)
    ones_k = jnp.ones((ONES_ROWS, TQ), BF16)
    zeros_p = jnp.zeros((HEAD_DIM, A_PAIR), BF16)
    pen_a = jnp.where(i >= 1, 0.0, NEG_INF).astype(F32)

    def key_block(q, p):
        g = q + p - B_LOOKBACK
        return (True, slice((g + B_LOOKBACK) * TQ, (g + B_LOOKBACK + 1) * TQ)) if g < 0 else \
            (False, slice(g * TQ, (g + 1) * TQ))


    b_parts = {}
    sumsq_b = [jnp.zeros((SUBLANES, TQ), F32) for _ in range(Q_PER_STEP)]
    sumsq_a = [jnp.zeros((SUBLANES, A_PAIR), F32) for _ in range(TS // A_PAIR)]

    def rowgroup_sumsq(t):
        return jnp.sum((t * t).reshape(t.shape[0] // SUBLANES, SUBLANES, t.shape[1]), axis=0)

    def inv_rms(sumsq, width):
        return lax.rsqrt(jnp.sum(sumsq, axis=0, keepdims=True) / width + EPS)

    def b_scores(q, h, p):
        hp, half = divmod(h, 2)
        qt = ft_ref[0, FT_QB + h * HEAD_DIM:FT_QB + (h + 1) * HEAD_DIM, q * TQ:(q + 1) * TQ]
        qm = jnp.concatenate([qt, zeros_q] if half == 0 else [zeros_q, qt], axis=0)
        in_prev, toks = key_block(q, p)
        kp = (kbp_ref if in_prev else kbc_ref)[0, toks, hp * LANES:(hp + 1) * LANES]
        raw = _dot(kp, qm)
        tiles = {}
        for rt, lt in B_LIVE_TILES[p]:
            ks = slice(rt * A_PAIR, (rt + 1) * A_PAIR)
            qs = slice(lt * LANES, (lt + 1) * LANES)
            tiles[rt, lt] = raw[ks, qs] + bias_b_ref[h, p * TQ + rt * A_PAIR:p * TQ + (rt + 1) * A_PAIR, qs]
        mcols = []
        for lt in range(TQ // LANES):
            col = functools.reduce(jnp.maximum, [t for (_, l2), t in tiles.items() if l2 == lt])
            mcols.append(jnp.max(col, axis=0, keepdims=True))
        return tiles, mcols

    def b_output(q, h, p, tiles, mcols):
        rows = slice(h * HEAD_DIM, (h + 1) * HEAD_DIM)
        tokq = slice(q * TQ, (q + 1) * TQ)
        dead = jnp.zeros((A_PAIR, LANES), BF16)
        pt = jnp.concatenate([
            jnp.concatenate([jnp.exp2(tiles[rt, lt] - mcols[lt]).astype(BF16) if (rt, lt) in tiles else dead
                             for rt in range(TQ // A_PAIR)], axis=0)
            for lt in range(TQ // LANES)], axis=1)
        mp = jnp.concatenate(mcols, axis=1)
        in_prev, toks = key_block(q, p)
        if in_prev:
            mp = jnp.where(i == 0, NEG_INF, mp)
            vh = vbp_ref[0, rows, toks]
        else:
            vh = ft_ref[0, FT_VB + h * HEAD_DIM:FT_VB + (h + 1) * HEAD_DIM, toks]
        vt = jnp.concatenate([vh, ones_k], axis=0)
        b_parts.setdefault((q, h), []).append((_dot(vt, pt)[:HEAD_DIM + 8], mp))
        if p == B_BLOCKS - 1:
            parts = b_parts.pop((q, h))
            m = functools.reduce(jnp.maximum, [mq for _, mq in parts])
            ot = sum(op * jnp.exp2(mq - m) for op, mq in parts)
            yt = ot[:HEAD_DIM] * (1.0 / ot[HEAD_DIM:HEAD_DIM + 1])
            sumsq_b[q] = sumsq_b[q] + rowgroup_sumsq(yt)
            yb_scr[rows, tokq] = yt * gb_ref[rows, :]

    def a_windows(r):
        if r == 0:
            kvwin = jnp.concatenate([kvap_ref[0], ft_ref[0, FT_KVA:, :A_PAIR]], axis=1)
        else:
            kvwin = ft_ref[0, FT_KVA:, (r - 1) * A_PAIR:(r + 1) * A_PAIR]
        return kvwin[:KA_W], kvwin[KA_W:]

    def a_scores(r, kvh):
        tok = slice(r * A_PAIR, (r + 1) * A_PAIR)
        kwin, _ = a_windows(r)
        blocks = []
        for g in range(A_GROUP):
            h = kvh * A_GROUP + g
            qt = ft_ref[0, h * HEAD_DIM:(h + 1) * HEAD_DIM, tok]
            blocks.append(jnp.concatenate([qt, zeros_p] if kvh == 0 else [zeros_p, qt], axis=0))
        qst = jnp.concatenate(blocks, axis=1)
        st = _dot_tn(kwin, qst) + bias_a_ref[kvh]
        s0 = st[:A_PAIR]
        s1 = st[A_PAIR:]
        if r == 0:
            s0 = s0 + pen_a
        m = jnp.maximum(jnp.max(jnp.maximum(s0, s1), axis=0, keepdims=True), sink_ref[kvh])
        return (s0, s1), m

    def a_output(r, kvh, st, m):
        tok = slice(r * A_PAIR, (r + 1) * A_PAIR)
        _, vwin = a_windows(r)
        pt = jnp.concatenate([jnp.exp2(st[0] - m), jnp.exp2(st[1] - m)], axis=0).astype(BF16)
        vt = jnp.concatenate([vwin[kvh * HEAD_DIM:(kvh + 1) * HEAD_DIM, :], ones_k], axis=0)
        ot = _dot(vt, pt)
        den = ot[HEAD_DIM:HEAD_DIM + 1] + jnp.exp2(sink_ref[kvh] - m)
        yt = ot[:HEAD_DIM] * (1.0 / den)
        for g in range(A_GROUP):
            rows = slice((kvh * A_GROUP + g) * HEAD_DIM, (kvh * A_GROUP + g + 1) * HEAD_DIM)
            yh = yt[:, g * A_PAIR:(g + 1) * A_PAIR]
            sumsq_a[r] = sumsq_a[r] + rowgroup_sumsq(yh)
            ya_scr[rows, tok] = yh * ga_ref[rows, :]

    def b_finish(q):
        tokq = slice(q * TQ, (q + 1) * TQ)
        y_ref[0, QA_W:, tokq] = (yb_scr[:, tokq] * inv_rms(sumsq_b[q], QB_W)).astype(BF16)

    def a_finish(r):
        tok = slice(r * A_PAIR, (r + 1) * A_PAIR)
        y_ref[0, :QA_W, tok] = (ya_scr[:, tok] * inv_rms(sumsq_a[r], QA_W)).astype(BF16)

    units = []
    for q in range(Q_PER_STEP):
        units += [(b_scores, b_output, (q, h, p),
                   functools.partial(b_finish, q) if (h, p) == (B_HEADS - 1, B_BLOCKS - 1) else None)
                  for h in range(B_HEADS) for p in range(B_BLOCKS)]
        units += [(a_scores, a_output, (r, kvh), functools.partial(a_finish, r) if kvh == A_KV_HEADS - 1 else None)
                  for r in range(q * TQ // A_PAIR, (q + 1) * TQ // A_PAIR) for kvh in range(A_KV_HEADS)]
    pending = []

    def run_output():
        output, args, staged, finish = pending.pop(0)
        output(*args, *staged)
        if finish is not None:
            finish()

    for scores, output, args, finish in units:
        pending.append((output, args, scores(*args), finish))
        if len(pending) > PIPE_DEPTH:
            run_output()
    while pending:
        run_output()


def _attn_call(kb, ft, bias_a, sink_a, bias_b, ga, gb):
    b, s, _ = kb.shape
    assert FT_VB % QB_W == 0 and FT_KVA % (2 * KA_W) == 0
    look = B_LOOKBACK * TQ
    assert TS % look == 0
    pairs_per_step = TS // A_PAIR

    def look_block(i):
        return jnp.maximum(i * (TS // look) - 1, 0)

    in_specs = [
        pl.BlockSpec((1, FT_ROWS, TS), lambda bi, i: (bi, 0, i)),
        pl.BlockSpec((1, look, QB_W), lambda bi, i: (bi, look_block(i), 0)),
        pl.BlockSpec((1, TS, QB_W), lambda bi, i: (bi, i, 0)),
        pl.BlockSpec((1, QB_W, look), lambda bi, i: (bi, FT_VB // QB_W, look_block(i))),
        pl.BlockSpec((1, 2 * KA_W, A_PAIR),
                     lambda bi, i: (bi, FT_KVA // (2 * KA_W), jnp.maximum(pairs_per_step * i - 1, 0))),
        _const_spec(bias_a.shape), _const_spec(sink_a.shape), _const_spec(bias_b.shape),
        _const_spec(ga.shape), _const_spec(gb.shape),
    ]
    return pl.pallas_call(
        _attn_kernel,
        grid=(b, s // TS),
        in_specs=in_specs,
        out_specs=pl.BlockSpec((1, MIX_W, TS), lambda bi, i: (bi, 0, i)),
        out_shape=jax.ShapeDtypeStruct((b, MIX_W, s), BF16),
        scratch_shapes=[pltpu.VMEM((QA_W, TS), F32), pltpu.VMEM((QB_W, TS), F32)],
        compiler_params=pltpu.CompilerParams(
            dimension_semantics=("arbitrary", "arbitrary"), vmem_limit_bytes=VMEM_LIMIT),
        name="attention",
    )(ft, kb, kb, ft, ft, bias_a, sink_a, bias_b, ga, gb)


def _ffn_kernel(yt_ref, x_ref, wo_ref, g2_ref, w1_ref, w2_ref, gf_ref, o_ref, h_scr, n2_scr):
    h = x_ref[0] + _dot_tn(yt_ref[0], wo_ref[...])
    h_scr[...] = h
    n2_scr[...] = _rms(h, g2_ref[...]).astype(BF16)

    def mlp_chunk(rows, c):
        cols = slice(c * FF_CHUNK, (c + 1) * FF_CHUNK)
        u = _dot(n2_scr[rows, :], w1_ref[:, cols])
        u = jnp.square(jnp.maximum(u, 0.0)).astype(BF16)
        return _dot(u, w2_ref[cols, :])

    last = D_FF // FF_CHUNK - 1
    for c in range(last):
        h_scr[...] += mlp_chunk(slice(None), c)
    for blk in range(TM_FFN // FFN_TAIL_ROWS):
        rows = slice(blk * FFN_TAIL_ROWS, (blk + 1) * FFN_TAIL_ROWS)
        o_ref[0, rows, :] = _rms(h_scr[rows, :] + mlp_chunk(rows, last), gf_ref[...])


def _ffn_call(yt, x, wo, g2, w1, w2, gf):
    b, s, _ = x.shape
    return pl.pallas_call(
        _ffn_kernel,
        grid=(b, s // TM_FFN),
        in_specs=[
            pl.BlockSpec((1, MIX_W, TM_FFN), lambda bi, i: (bi, 0, i)),
            pl.BlockSpec((1, TM_FFN, D_MODEL), lambda bi, i: (bi, i, 0)),
            _const_spec(wo.shape), _const_spec(g2.shape),
            _const_spec(w1.shape), _const_spec(w2.shape), _const_spec(gf.shape),
        ],
        out_specs=pl.BlockSpec((1, TM_FFN, D_MODEL), lambda bi, i: (bi, i, 0)),
        out_shape=jax.ShapeDtypeStruct((b, s, D_MODEL), F32),
        scratch_shapes=[pltpu.VMEM((TM_FFN, D_MODEL), F32), pltpu.VMEM((TM_FFN, D_MODEL), BF16)],
        compiler_params=pltpu.CompilerParams(
            dimension_semantics=("arbitrary", "arbitrary"), vmem_limit_bytes=VMEM_LIMIT),
        name="out_proj_mlp",
    )(yt, x, wo, g2, w1, w2, gf)


def _bias_a_table(sinks):
    k = np.arange(A_WIN)[:, None]
    i = np.arange(A_PAIR)[None, :]
    dist = np.abs(A_PAIR + i - k).astype(np.float32)
    qc = i // CHUNK
    kc = k // CHUNK
    allowed = (kc >= qc) & (kc <= qc + A_BAND_CHUNKS - 1)
    slopes = jnp.exp2(-8.0 * (jnp.arange(A_HEADS, dtype=F32) + 1.0) / A_HEADS)
    bias = -slopes[:, None, None] * jnp.asarray(dist)[None] * LOG2E
    bias = jnp.where(jnp.asarray(allowed)[None], bias, NEG_INF)
    bias = bias.reshape(A_KV_HEADS, A_GROUP, A_WIN, A_PAIR).transpose(0, 2, 1, 3)
    bias = bias.reshape(A_KV_HEADS, A_WIN, A_GROUP * A_PAIR)
    sink = jnp.broadcast_to((sinks.astype(F32) * LOG2E).reshape(A_KV_HEADS, 1, A_GROUP, 1),
                            (A_KV_HEADS, 1, A_GROUP, A_PAIR)).reshape(A_KV_HEADS, 1, A_GROUP * A_PAIR)
    return bias, sink


def _bias_b_table(rel_bias):
    rb = rel_bias.astype(F32) * LOG2E
    base = (B_BAND_CHUNKS - 1) * CHUNK

    def tile(delta):
        lo = base + LANES * delta - (LANES - 1)
        if lo >= B_MAX_REL:
            return jnp.broadcast_to(rb[:, -1][:, None, None], (B_HEADS, LANES, LANES))
        off = np.arange(2 * LANES)
        off = np.where(off >= LANES, off - 2 * LANES, off)
        rel = np.clip(base + LANES * delta + off, -B_MAX_REL, B_MAX_REL) + B_MAX_REL
        row = rb[:, jnp.asarray(rel)]
        flat = jnp.tile(row, (1, LANES))[:, :LANES * (2 * LANES - 1)]
        return flat.reshape(B_HEADS, LANES, 2 * LANES - 1)[:, :, :LANES]

    tiles = {delta: tile(delta) for delta in range(-(B_WIN // LANES - 1), TQ // LANES)}
    bias = jnp.concatenate(
        [jnp.concatenate([tiles[t - a] for t in range(TQ // LANES)], axis=2) for a in range(B_WIN // LANES)],
        axis=1)
    k = np.arange(B_WIN)[:, None]
    q = np.arange(TQ)[None, :]
    qc = q // CHUNK
    kc = k // CHUNK
    allowed = (kc >= qc) & (kc <= qc + B_BAND_CHUNKS - 1)
    return jnp.where(jnp.asarray(allowed)[None], bias, NEG_INF)


def kernel(x, norm1_g, w_in, sinks_a, rel_bias_b, out_norm_a_g, out_norm_b_g, w_out, norm2_g,
           w_ff1, w_ff2, final_norm_g):
    b, s, d = x.shape
    assert d == D_MODEL and s % TM_PROJ == 0 and s % TM_FFN == 0 and s % TS == 0
    assert norm1_g.shape[0] == 1, "single-layer block"
    assert w_in.shape[2] == 2 * QA_W + 2 * KA_W + 2 * QB_W
    kb, ft, wo, w1, w2 = _proj_call(
        x, norm1_g[0].reshape(1, d), w_in[0], w_out[0], w_ff1[0], w_ff2[0])

    bias_a, sink_a = _bias_a_table(sinks_a[0])
    bias_b = _bias_b_table(rel_bias_b[0])
    ga = jnp.broadcast_to(out_norm_a_g[0].astype(F32)[:, None], (QA_W, A_PAIR))
    gb = jnp.broadcast_to(out_norm_b_g[0].astype(F32)[:, None], (QB_W, TQ))
    yt = _attn_call(kb, ft, bias_a, sink_a, bias_b, ga, gb)

    return _ffn_call(yt, x, wo, norm2_g[0].reshape(1, d), w1, w2, final_norm_g.reshape(1, d))
```

```python
import functools

import jax
import jax.numpy as jnp
import numpy as np
from jax import lax
from jax.experimental import pallas as pl
from jax.experimental.pallas import tpu as pltpu

D_MODEL = 1024
CHUNK = 64
HEAD_DIM = 64
A_HEADS = 8
A_KV_HEADS = 2
A_GROUP = A_HEADS // A_KV_HEADS
A_BAND_CHUNKS = 3
B_HEADS = 8
B_BAND_CHUNKS = 9
B_MAX_REL = 128
D_FF = 4 * D_MODEL
EPS = 1e-6
NEG_INF = -1e30
LOG2E = 1.4426950408889634

QA_W = A_HEADS * HEAD_DIM
KA_W = A_KV_HEADS * HEAD_DIM
QB_W = B_HEADS * HEAD_DIM
MIX_W = QA_W + QB_W
FT_QB, FT_VB, FT_KVA, FT_ROWS = QA_W, QA_W + QB_W, QA_W + 2 * QB_W, QA_W + 2 * QB_W + 2 * KA_W

LANES = 128
SUBLANES = 8
TQ = 256
Q_PER_STEP = 2
TS = Q_PER_STEP * TQ
A_PAIR = 2 * CHUNK
A_WIN = 4 * CHUNK
B_BLOCKS = 3
B_WIN = B_BLOCKS * TQ
B_LOOKBACK = B_BLOCKS - 1
ONES_ROWS = 16
PIPE_DEPTH = 7
TM_PROJ = 1024
PROJ_SUB = 512
TM_FFN = 1024
FF_CHUNK = 1024
FFN_TAIL_ROWS = 256
V7X_VMEM_BYTES = 64 * 1024 * 1024
VMEM_LIMIT = V7X_VMEM_BYTES * 7 // 8

F32 = jnp.float32
BF16 = jnp.bfloat16


def _b_live_tiles():
    chunks_per_tile = A_PAIR // CHUNK
    live = []
    for p in range(B_BLOCKS):
        tiles = []
        for rt in range(TQ // A_PAIR):
            for lt in range(TQ // LANES):
                kcs = [p * (TQ // CHUNK) + rt * chunks_per_tile + i for i in range(chunks_per_tile)]
                qcs = [lt * (LANES // CHUNK) + i for i in range(LANES // CHUNK)]
                if any(qc <= kc <= qc + B_BAND_CHUNKS - 1 for kc in kcs for qc in qcs):
                    tiles.append((rt, lt))
        live.append(tuple(tiles))
    return tuple(live)


B_LIVE_TILES = _b_live_tiles()


def _rms(x, g):
    ms = jnp.mean(x * x, axis=-1, keepdims=True)
    return x * lax.rsqrt(ms + EPS) * g


def _rms_rows(xt, g):
    ms = jnp.mean(xt * xt, axis=0, keepdims=True)
    return xt * lax.rsqrt(ms + EPS) * g


def _dot(a, b):
    return jnp.dot(a, b, preferred_element_type=F32)


def _dot_nt(a, b):
    return lax.dot_general(a, b, (((1,), (1,)), ((), ())), preferred_element_type=F32)


def _dot_tn(a, b):
    return lax.dot_general(a, b, (((0,), (0,)), ((), ())), preferred_element_type=F32)


def _const_spec(shape):
    nd = len(shape)
    return pl.BlockSpec(shape, lambda *_: (0,) * nd, pipeline_mode=pl.Buffered(1))


def _proj_kernel(x_ref, g_ref, win_ref, wo_ref, w1_ref, w2_ref,
                 kb_ref, ft_ref, wo_bf_ref, w1_bf_ref, w2_bf_ref,
                 wk_scr, wt_scr):
    @pl.when((pl.program_id(0) == 0) & (pl.program_id(1) == 0))
    def _prepare_weights():
        o_ka, o_qb, o_kb, o_vb = QA_W, QA_W + 2 * KA_W, QA_W + 2 * KA_W + QB_W, QA_W + 2 * KA_W + 2 * QB_W
        q_scale = HEAD_DIM ** -0.5 * LOG2E
        wk_scr[...] = win_ref[:, o_kb:o_vb].astype(BF16)
        row = 0
        for lo, hi, scale in ((0, o_ka, q_scale), (o_qb, o_kb, q_scale), (o_vb, o_vb + QB_W, None),
                              (o_ka, o_qb, None)):
            part = win_ref[:, lo:hi]
            if scale is not None:
                part = part * scale
            wt_scr[row:row + hi - lo, :] = part.T.astype(BF16)
            row += hi - lo

    for i in range(TM_PROJ // PROJ_SUB):
        tok = slice(i * PROJ_SUB, (i + 1) * PROJ_SUB)
        n = _rms(x_ref[0, tok, :], g_ref[...]).astype(BF16)
        kb_ref[0, tok, :] = _dot(n, wk_scr[...]).astype(BF16)
        ft_ref[0, :, tok] = _dot_nt(wt_scr[...], n).astype(BF16)
    wo_bf_ref[...] = wo_ref[...].astype(BF16)
    w1_bf_ref[...] = w1_ref[...].astype(BF16)
    w2_bf_ref[...] = w2_ref[...].astype(BF16)


def _proj_call(x, g, w_in, wo, w1, w2):
    b, s, _ = x.shape
    steps = b * (s // TM_PROJ)

    def tok(width):
        return pl.BlockSpec((1, TM_PROJ, width), lambda bi, i: (bi, i, 0))

    def feat(width):
        return pl.BlockSpec((1, width, TM_PROJ), lambda bi, i: (bi, 0, i))

    def row_share(w):
        rows = w.shape[0] // steps
        assert rows * steps == w.shape[0] and rows % 16 == 0, "weight rows must split into bf16 row tiles"
        return pl.BlockSpec((rows, w.shape[1]), lambda bi, i: (bi * (s // TM_PROJ) + i, 0))

    def bf(w):
        return jax.ShapeDtypeStruct(w.shape, BF16)

    return pl.pallas_call(
        _proj_kernel,
        grid=(b, s // TM_PROJ),
        in_specs=[tok(D_MODEL), _const_spec(g.shape), _const_spec(w_in.shape),
                  row_share(wo), row_share(w1), row_share(w2)],
        out_specs=[tok(QB_W), feat(FT_ROWS), row_share(wo), row_share(w1), row_share(w2)],
        out_shape=[jax.ShapeDtypeStruct((b, s, QB_W), BF16), jax.ShapeDtypeStruct((b, FT_ROWS, s), BF16),
                   bf(wo), bf(w1), bf(w2)],
        scratch_shapes=[pltpu.VMEM((D_MODEL, QB_W), BF16),
                        pltpu.VMEM((w_in.shape[1] - QB_W, D_MODEL), BF16)],
        compiler_params=pltpu.CompilerParams(
            dimension_semantics=("arbitrary", "arbitrary"), vmem_limit_bytes=VMEM_LIMIT),
        name="norm_in_proj",
    )(x, g, w_in, wo, w1, w2)


def _attn_kernel(ft_ref, kbp_ref, kbc_ref, vbp_ref, kvap_ref,
                 bias_a_ref, sink_ref, bias_b_ref, ga_ref, gb_ref, x_ref, wo_ref,
                 h_ref, ya_scr, yb_scr, y_ref):
    i = pl.program_id(1)
    zeros_q = jnp.zeros((HEAD_DIM, TQ), BF16)
    ones_k = jnp.ones((ONES_ROWS, TQ), BF16)
    zeros_p = jnp.zeros((HEAD_DIM, A_PAIR), BF16)
    pen_a = jnp.where(i >= 1, 0.0, NEG_INF).astype(F32)

    def key_block(q, p):
        g = q + p - B_LOOKBACK
        return (True, slice((g + B_LOOKBACK) * TQ, (g + B_LOOKBACK + 1) * TQ)) if g < 0 else \
            (False, slice(g * TQ, (g + 1) * TQ))


    b_parts = {}
    sumsq_b = [jnp.zeros((SUBLANES, TQ), F32) for _ in range(Q_PER_STEP)]
    sumsq_a = [jnp.zeros((SUBLANES, A_PAIR), F32) for _ in range(TS // A_PAIR)]

    def rowgroup_sumsq(t):
        return jnp.sum((t * t).reshape(t.shape[0] // SUBLANES, SUBLANES, t.shape[1]), axis=0)

    def inv_rms(sumsq, width):
        return lax.rsqrt(jnp.sum(sumsq, axis=0, keepdims=True) / width + EPS)

    def b_scores(q, h, p):
        hp, half = divmod(h, 2)
        qt = ft_ref[0, FT_QB + h * HEAD_DIM:FT_QB + (h + 1) * HEAD_DIM, q * TQ:(q + 1) * TQ]
        qm = jnp.concatenate([qt, zeros_q] if half == 0 else [zeros_q, qt], axis=0)
        in_prev, toks = key_block(q, p)
        kp = (kbp_ref if in_prev else kbc_ref)[0, toks, hp * LANES:(hp + 1) * LANES]
        raw = _dot(kp, qm)
        tiles = {}
        for rt, lt in B_LIVE_TILES[p]:
            ks = slice(rt * A_PAIR, (rt + 1) * A_PAIR)
            qs = slice(lt * LANES, (lt + 1) * LANES)
            tiles[rt, lt] = raw[ks, qs] + bias_b_ref[h, p * TQ + rt * A_PAIR:p * TQ + (rt + 1) * A_PAIR, qs]
        mcols = []
        for lt in range(TQ // LANES):
            col = functools.reduce(jnp.maximum, [t for (_, l2), t in tiles.items() if l2 == lt])
            mcols.append(jnp.max(col, axis=0, keepdims=True))
        return tiles, mcols

    def b_output(q, h, p, tiles, mcols):
        rows = slice(h * HEAD_DIM, (h + 1) * HEAD_DIM)
        tokq = slice(q * TQ, (q + 1) * TQ)
        dead = jnp.zeros((A_PAIR, LANES), BF16)
        pt = jnp.concatenate([
            jnp.concatenate([jnp.exp2(tiles[rt, lt] - mcols[lt]).astype(BF16) if (rt, lt) in tiles else dead
                             for rt in range(TQ // A_PAIR)], axis=0)
            for lt in range(TQ // LANES)], axis=1)
        mp = jnp.concatenate(mcols, axis=1)
        in_prev, toks = key_block(q, p)
        if in_prev:
            mp = jnp.where(i == 0, NEG_INF, mp)
            vh = vbp_ref[0, rows, toks]
        else:
            vh = ft_ref[0, FT_VB + h * HEAD_DIM:FT_VB + (h + 1) * HEAD_DIM, toks]
        vt = jnp.concatenate([vh, ones_k], axis=0)
        b_parts.setdefault((q, h), []).append((_dot(vt, pt)[:HEAD_DIM + 8], mp))
        if p == B_BLOCKS - 1:
            parts = b_parts.pop((q, h))
            m = functools.reduce(jnp.maximum, [mq for _, mq in parts])
            ot = sum(op * jnp.exp2(mq - m) for op, mq in parts)
            yt = ot[:HEAD_DIM] * (1.0 / ot[HEAD_DIM:HEAD_DIM + 1])
            sumsq_b[q] = sumsq_b[q] + rowgroup_sumsq(yt)
            yb_scr[rows, tokq] = yt * gb_ref[rows, :]

    def a_windows(r):
        if r == 0:
            kvwin = jnp.concatenate([kvap_ref[0], ft_ref[0, FT_KVA:, :A_PAIR]], axis=1)
        else:
            kvwin = ft_ref[0, FT_KVA:, (r - 1) * A_PAIR:(r + 1) * A_PAIR]
        return kvwin[:KA_W], kvwin[KA_W:]

    def a_scores(r, kvh):
        tok = slice(r * A_PAIR, (r + 1) * A_PAIR)
        kwin, _ = a_windows(r)
        blocks = []
        for g in range(A_GROUP):
            h = kvh * A_GROUP + g
            qt = ft_ref[0, h * HEAD_DIM:(h + 1) * HEAD_DIM, tok]
            blocks.append(jnp.concatenate([qt, zeros_p] if kvh == 0 else [zeros_p, qt], axis=0))
        qst = jnp.concatenate(blocks, axis=1)
        st = _dot_tn(kwin, qst) + bias_a_ref[kvh]
        s0 = st[:A_PAIR]
        s1 = st[A_PAIR:]
        if r == 0:
            s0 = s0 + pen_a
        m = jnp.maximum(jnp.max(jnp.maximum(s0, s1), axis=0, keepdims=True), sink_ref[kvh])
        return (s0, s1), m

    def a_output(r, kvh, st, m):
        tok = slice(r * A_PAIR, (r + 1) * A_PAIR)
        _, vwin = a_windows(r)
        pt = jnp.concatenate([jnp.exp2(st[0] - m), jnp.exp2(st[1] - m)], axis=0).astype(BF16)
        vt = jnp.concatenate([vwin[kvh * HEAD_DIM:(kvh + 1) * HEAD_DIM, :], ones_k], axis=0)
        ot = _dot(vt, pt)
        den = ot[HEAD_DIM:HEAD_DIM + 1] + jnp.exp2(sink_ref[kvh] - m)
        yt = ot[:HEAD_DIM] * (1.0 / den)
        for g in range(A_GROUP):
            rows = slice((kvh * A_GROUP + g) * HEAD_DIM, (kvh * A_GROUP + g + 1) * HEAD_DIM)
            yh = yt[:, g * A_PAIR:(g + 1) * A_PAIR]
            sumsq_a[r] = sumsq_a[r] + rowgroup_sumsq(yh)
            ya_scr[rows, tok] = yh * ga_ref[rows, :]

    def b_finish(q):
        tokq = slice(q * TQ, (q + 1) * TQ)
        y_ref[0, QA_W:, tokq] = (yb_scr[:, tokq] * inv_rms(sumsq_b[q], QB_W)).astype(BF16)

    def a_finish(r):
        tok = slice(r * A_PAIR, (r + 1) * A_PAIR)
        y_ref[0, :QA_W, tok] = (ya_scr[:, tok] * inv_rms(sumsq_a[r], QA_W)).astype(BF16)
        if (r + 1) % (TQ // A_PAIR) == 0:
            tokq = slice((r + 1) * A_PAIR - TQ, (r + 1) * A_PAIR)
            h_ref[0, tokq, :] = x_ref[0, tokq, :] + _dot_tn(y_ref[0, :, tokq], wo_ref[...])

    units = []
    for q in range(Q_PER_STEP):
        units += [(b_scores, b_output, (q, h, p),
                   functools.partial(b_finish, q) if (h, p) == (B_HEADS - 1, B_BLOCKS - 1) else None)
                  for h in range(B_HEADS) for p in range(B_BLOCKS)]
        units += [(a_scores, a_output, (r, kvh), functools.partial(a_finish, r) if kvh == A_KV_HEADS - 1 else None)
                  for r in range(q * TQ // A_PAIR, (q + 1) * TQ // A_PAIR) for kvh in range(A_KV_HEADS)]
    pending = []

    def run_output():
        output, args, staged, finish = pending.pop(0)
        output(*args, *staged)
        if finish is not None:
            finish()

    for scores, output, args, finish in units:
        pending.append((output, args, scores(*args), finish))
        if len(pending) > PIPE_DEPTH:
            run_output()
    while pending:
        run_output()


def _attn_call(kb, ft, bias_a, sink_a, bias_b, ga, gb, x, wo):
    b, s, _ = kb.shape
    assert FT_VB % QB_W == 0 and FT_KVA % (2 * KA_W) == 0
    look = B_LOOKBACK * TQ
    assert TS % look == 0
    pairs_per_step = TS // A_PAIR

    def look_block(i):
        return jnp.maximum(i * (TS // look) - 1, 0)

    in_specs = [
        pl.BlockSpec((1, FT_ROWS, TS), lambda bi, i: (bi, 0, i)),
        pl.BlockSpec((1, look, QB_W), lambda bi, i: (bi, look_block(i), 0)),
        pl.BlockSpec((1, TS, QB_W), lambda bi, i: (bi, i, 0)),
        pl.BlockSpec((1, QB_W, look), lambda bi, i: (bi, FT_VB // QB_W, look_block(i))),
        pl.BlockSpec((1, 2 * KA_W, A_PAIR),
                     lambda bi, i: (bi, FT_KVA // (2 * KA_W), jnp.maximum(pairs_per_step * i - 1, 0))),
        _const_spec(bias_a.shape), _const_spec(sink_a.shape), _const_spec(bias_b.shape),
        _const_spec(ga.shape), _const_spec(gb.shape),
        pl.BlockSpec((1, TS, D_MODEL), lambda bi, i: (bi, i, 0)), _const_spec(wo.shape),
    ]
    return pl.pallas_call(
        _attn_kernel,
        grid=(b, s // TS),
        in_specs=in_specs,
        out_specs=pl.BlockSpec((1, TS, D_MODEL), lambda bi, i: (bi, i, 0)),
        out_shape=jax.ShapeDtypeStruct((b, s, D_MODEL), F32),
        scratch_shapes=[pltpu.VMEM((QA_W, TS), F32), pltpu.VMEM((QB_W, TS), F32),
                        pltpu.VMEM((1, MIX_W, TS), BF16)],
        compiler_params=pltpu.CompilerParams(
            dimension_semantics=("arbitrary", "arbitrary"), vmem_limit_bytes=VMEM_LIMIT),
        name="attention",
    )(ft, kb, kb, ft, ft, bias_a, sink_a, bias_b, ga, gb, x, wo)


def _ffn_kernel(h_ref, g2_ref, w1_ref, w2_ref, gf_ref, o_ref, h_scr, n2_scr):
    h = h_ref[0]
    h_scr[...] = h
    n2_scr[...] = _rms(h, g2_ref[...]).astype(BF16)

    def mlp_chunk(rows, c):
        cols = slice(c * FF_CHUNK, (c + 1) * FF_CHUNK)
        u = _dot(n2_scr[rows, :], w1_ref[:, cols])
        u = jnp.square(jnp.maximum(u, 0.0)).astype(BF16)
        return _dot(u, w2_ref[cols, :])

    last = D_FF // FF_CHUNK - 1
    for c in range(last):
        h_scr[...] += mlp_chunk(slice(None), c)
    for blk in range(TM_FFN // FFN_TAIL_ROWS):
        rows = slice(blk * FFN_TAIL_ROWS, (blk + 1) * FFN_TAIL_ROWS)
        o_ref[0, rows, :] = _rms(h_scr[rows, :] + mlp_chunk(rows, last), gf_ref[...])


def _ffn_call(h, g2, w1, w2, gf):
    b, s, _ = h.shape
    return pl.pallas_call(
        _ffn_kernel,
        grid=(b, s // TM_FFN),
        in_specs=[
            pl.BlockSpec((1, TM_FFN, D_MODEL), lambda bi, i: (bi, i, 0)),
            _const_spec(g2.shape), _const_spec(w1.shape), _const_spec(w2.shape), _const_spec(gf.shape),
        ],
        out_specs=pl.BlockSpec((1, TM_FFN, D_MODEL), lambda bi, i: (bi, i, 0)),
        out_shape=jax.ShapeDtypeStruct((b, s, D_MODEL), F32),
        scratch_shapes=[pltpu.VMEM((TM_FFN, D_MODEL), F32), pltpu.VMEM((TM_FFN, D_MODEL), BF16)],
        compiler_params=pltpu.CompilerParams(
            dimension_semantics=("arbitrary", "arbitrary"), vmem_limit_bytes=VMEM_LIMIT),
        name="out_proj_mlp",
    )(h, g2, w1, w2, gf)


def _bias_a_table(sinks):
    k = np.arange(A_WIN)[:, None]
    i = np.arange(A_PAIR)[None, :]
    dist = np.abs(A_PAIR + i - k).astype(np.float32)
    qc = i // CHUNK
    kc = k // CHUNK
    allowed = (kc >= qc) & (kc <= qc + A_BAND_CHUNKS - 1)
    slopes = jnp.exp2(-8.0 * (jnp.arange(A_HEADS, dtype=F32) + 1.0) / A_HEADS)
    bias = -slopes[:, None, None] * jnp.asarray(dist)[None] * LOG2E
    bias = jnp.where(jnp.asarray(allowed)[None], bias, NEG_INF)
    bias = bias.reshape(A_KV_HEADS, A_GROUP, A_WIN, A_PAIR).transpose(0, 2, 1, 3)
    bias = bias.reshape(A_KV_HEADS, A_WIN, A_GROUP * A_PAIR)
    sink = jnp.broadcast_to((sinks.astype(F32) * LOG2E).reshape(A_KV_HEADS, 1, A_GROUP, 1),
                            (A_KV_HEADS, 1, A_GROUP, A_PAIR)).reshape(A_KV_HEADS, 1, A_GROUP * A_PAIR)
    return bias, sink


def _bias_b_table(rel_bias):
    rb = rel_bias.astype(F32) * LOG2E
    base = (B_BAND_CHUNKS - 1) * CHUNK

    def tile(delta):
        lo = base + LANES * delta - (LANES - 1)
        if lo >= B_MAX_REL:
            return jnp.broadcast_to(rb[:, -1][:, None, None], (B_HEADS, LANES, LANES))
        off = np.arange(2 * LANES)
        off = np.where(off >= LANES, off - 2 * LANES, off)
        rel = np.clip(base + LANES * delta + off, -B_MAX_REL, B_MAX_REL) + B_MAX_REL
        row = rb[:, jnp.asarray(rel)]
        flat = jnp.tile(row, (1, LANES))[:, :LANES * (2 * LANES - 1)]
        return flat.reshape(B_HEADS, LANES, 2 * LANES - 1)[:, :, :LANES]

    tiles = {delta: tile(delta) for delta in range(-(B_WIN // LANES - 1), TQ // LANES)}
    bias = jnp.concatenate(
        [jnp.concatenate([tiles[t - a] for t in range(TQ // LANES)], axis=2) for a in range(B_WIN // LANES)],
        axis=1)
    k = np.arange(B_WIN)[:, None]
    q = np.arange(TQ)[None, :]
    qc = q // CHUNK
    kc = k // CHUNK
    allowed = (kc >= qc) & (kc <= qc + B_BAND_CHUNKS - 1)
    return jnp.where(jnp.asarray(allowed)[None], bias, NEG_INF)


def kernel(x, norm1_g, w_in, sinks_a, rel_bias_b, out_norm_a_g, out_norm_b_g, w_out, norm2_g,
           w_ff1, w_ff2, final_norm_g):
    b, s, d = x.shape
    assert d == D_MODEL and s % TM_PROJ == 0 and s % TM_FFN == 0 and s % TS == 0
    assert norm1_g.shape[0] == 1, "single-layer block"
    assert w_in.shape[2] == 2 * QA_W + 2 * KA_W + 2 * QB_W
    kb, ft, wo, w1, w2 = _proj_call(
        x, norm1_g[0].reshape(1, d), w_in[0], w_out[0], w_ff1[0], w_ff2[0])

    bias_a, sink_a = _bias_a_table(sinks_a[0])
    bias_b = _bias_b_table(rel_bias_b[0])
    ga = jnp.broadcast_to(out_norm_a_g[0].astype(F32)[:, None], (QA_W, A_PAIR))
    gb = jnp.broadcast_to(out_norm_b_g[0].astype(F32)[:, None], (QB_W, TQ))
    h = _attn_call(kb, ft, bias_a, sink_a, bias_b, ga, gb, x, wo)

    return _ffn_call(h, norm2_g[0].reshape(1, d), w1, w2, final_norm_g.reshape(1, d))
```

```python
import functools

import jax
import jax.numpy as jnp
import numpy as np
from jax import lax
from jax.experimental import pallas as pl
from jax.experimental.pallas import tpu as pltpu

D_MODEL = 1024
CHUNK = 64
HEAD_DIM = 64
A_HEADS = 8
A_KV_HEADS = 2
A_GROUP = A_HEADS // A_KV_HEADS
A_BAND_CHUNKS = 3
B_HEADS = 8
B_BAND_CHUNKS = 9
B_MAX_REL = 128
D_FF = 4 * D_MODEL
EPS = 1e-6
NEG_INF = -1e30
LOG2E = 1.4426950408889634

QA_W = A_HEADS * HEAD_DIM
KA_W = A_KV_HEADS * HEAD_DIM
QB_W = B_HEADS * HEAD_DIM
MIX_W = QA_W + QB_W
FT_QB, FT_VB, FT_KVA, FT_ROWS = QA_W, QA_W + QB_W, QA_W + 2 * QB_W, QA_W + 2 * QB_W + 2 * KA_W

LANES = 128
SUBLANES = 8
TQ = 256
Q_PER_STEP = 2
TS = Q_PER_STEP * TQ
A_PAIR = 2 * CHUNK
A_WIN = 4 * CHUNK
B_BLOCKS = 3
B_WIN = B_BLOCKS * TQ
B_LOOKBACK = B_BLOCKS - 1
ONES_ROWS = 16
PIPE_DEPTH = 7
TM_PROJ = 1024
PROJ_SUB = 512
TM_FFN = 1024
FF_CHUNK = 1024
FFN_TAIL_ROWS = 256
V7X_VMEM_BYTES = 64 * 1024 * 1024
VMEM_LIMIT = V7X_VMEM_BYTES * 7 // 8

F32 = jnp.float32
BF16 = jnp.bfloat16


def _b_live_tiles():
    chunks_per_tile = A_PAIR // CHUNK
    live = []
    for p in range(B_BLOCKS):
        tiles = []
        for rt in range(TQ // A_PAIR):
            for lt in range(TQ // LANES):
                kcs = [p * (TQ // CHUNK) + rt * chunks_per_tile + i for i in range(chunks_per_tile)]
                qcs = [lt * (LANES // CHUNK) + i for i in range(LANES // CHUNK)]
                if any(qc <= kc <= qc + B_BAND_CHUNKS - 1 for kc in kcs for qc in qcs):
                    tiles.append((rt, lt))
        live.append(tuple(tiles))
    return tuple(live)


B_LIVE_TILES = _b_live_tiles()


def _rms(x, g):
    ms = jnp.mean(x * x, axis=-1, keepdims=True)
    return x * lax.rsqrt(ms + EPS) * g


def _rms_rows(xt, g):
    ms = jnp.mean(xt * xt, axis=0, keepdims=True)
    return xt * lax.rsqrt(ms + EPS) * g


def _dot(a, b):
    return jnp.dot(a, b, preferred_element_type=F32)


def _dot_nt(a, b):
    return lax.dot_general(a, b, (((1,), (1,)), ((), ())), preferred_element_type=F32)


def _dot_tn(a, b):
    return lax.dot_general(a, b, (((0,), (0,)), ((), ())), preferred_element_type=F32)


def _const_spec(shape):
    nd = len(shape)
    return pl.BlockSpec(shape, lambda *_: (0,) * nd, pipeline_mode=pl.Buffered(1))


def _proj_kernel(x_ref, g_ref, win_ref, wo_ref, w1_ref, w2_ref,
                 kb_ref, ft_ref, wo_bf_ref, w1_bf_ref, w2_bf_ref,
                 wk_scr, wt_scr):
    @pl.when((pl.program_id(0) == 0) & (pl.program_id(1) == 0))
    def _prepare_weights():
        o_ka, o_qb, o_kb, o_vb = QA_W, QA_W + 2 * KA_W, QA_W + 2 * KA_W + QB_W, QA_W + 2 * KA_W + 2 * QB_W
        q_scale = HEAD_DIM ** -0.5 * LOG2E
        wk_scr[...] = win_ref[:, o_kb:o_vb].astype(BF16)
        row = 0
        for lo, hi, scale in ((0, o_ka, q_scale), (o_qb, o_kb, q_scale), (o_vb, o_vb + QB_W, None),
                              (o_ka, o_qb, None)):
            part = win_ref[:, lo:hi]
            if scale is not None:
                part = part * scale
            wt_scr[row:row + hi - lo, :] = part.T.astype(BF16)
            row += hi - lo

    for i in range(TM_PROJ // PROJ_SUB):
        tok = slice(i * PROJ_SUB, (i + 1) * PROJ_SUB)
        n = _rms(x_ref[0, tok, :], g_ref[...]).astype(BF16)
        kb_ref[0, tok, :] = _dot(n, wk_scr[...]).astype(BF16)
        ft_ref[0, :, tok] = _dot_nt(wt_scr[...], n).astype(BF16)
    wo_bf_ref[...] = wo_ref[...].astype(BF16)
    w1_bf_ref[...] = w1_ref[...].astype(BF16)
    w2_bf_ref[...] = w2_ref[...].astype(BF16)


def _proj_call(x, g, w_in, wo, w1, w2):
    b, s, _ = x.shape
    steps = b * (s // TM_PROJ)

    def tok(width):
        return pl.BlockSpec((1, TM_PROJ, width), lambda bi, i: (bi, i, 0))

    def feat(width):
        return pl.BlockSpec((1, width, TM_PROJ), lambda bi, i: (bi, 0, i))

    def row_share(w):
        rows = w.shape[0] // steps
        assert rows * steps == w.shape[0] and rows % 16 == 0, "weight rows must split into bf16 row tiles"
        return pl.BlockSpec((rows, w.shape[1]), lambda bi, i: (bi * (s // TM_PROJ) + i, 0))

    def bf(w):
        return jax.ShapeDtypeStruct(w.shape, BF16)

    return pl.pallas_call(
        _proj_kernel,
        grid=(b, s // TM_PROJ),
        in_specs=[tok(D_MODEL), _const_spec(g.shape), _const_spec(w_in.shape),
                  row_share(wo), row_share(w1), row_share(w2)],
        out_specs=[tok(QB_W), feat(FT_ROWS), row_share(wo), row_share(w1), row_share(w2)],
        out_shape=[jax.ShapeDtypeStruct((b, s, QB_W), BF16), jax.ShapeDtypeStruct((b, FT_ROWS, s), BF16),
                   bf(wo), bf(w1), bf(w2)],
        scratch_shapes=[pltpu.VMEM((D_MODEL, QB_W), BF16),
                        pltpu.VMEM((w_in.shape[1] - QB_W, D_MODEL), BF16)],
        compiler_params=pltpu.CompilerParams(
            dimension_semantics=("arbitrary", "arbitrary"), vmem_limit_bytes=VMEM_LIMIT),
        name="norm_in_proj",
    )(x, g, w_in, wo, w1, w2)


def _attn_kernel(ft_ref, kbp_ref, kbc_ref, vbp_ref, kvap_ref,
                 bias_a_ref, sink_ref, bias_b_ref, ga_ref, gb_ref,
                 y_ref, ya_scr, yb_scr):
    i = pl.program_id(1)
    zeros_q = jnp.zeros((HEAD_DIM, TQ), BF16)
    ones_k = jnp.ones((ONES_ROWS, TQ), BF16)
    zeros_p = jnp.zeros((HEAD_DIM, A_PAIR), BF16)
    pen_a = jnp.where(i >= 1, 0.0, NEG_INF).astype(F32)

    def key_block(q, p):
        g = q + p - B_LOOKBACK
        return (True, slice((g + B_LOOKBACK) * TQ, (g + B_LOOKBACK + 1) * TQ)) if g < 0 else \
            (False, slice(g * TQ, (g + 1) * TQ))


    b_parts = {}
    sumsq_b = [jnp.zeros((SUBLANES, TQ), F32) for _ in range(Q_PER_STEP)]
    sumsq_a = [jnp.zeros((SUBLANES, A_PAIR), F32) for _ in range(TS // A_PAIR)]

    def rowgroup_sumsq(t):
        return jnp.sum((t * t).reshape(t.shape[0] // SUBLANES, SUBLANES, t.shape[1]), axis=0)

    def inv_rms(sumsq, width):
        return lax.rsqrt(jnp.sum(sumsq, axis=0, keepdims=True) / width + EPS)

    def b_scores(q, h):
        hp, half = divmod(h, 2)
        qt = ft_ref[0, FT_QB + h * HEAD_DIM:FT_QB + (h + 1) * HEAD_DIM, q * TQ:(q + 1) * TQ]
        qm = jnp.concatenate([qt, zeros_q] if half == 0 else [zeros_q, qt], axis=0)
        kps = []
        for p in range(B_BLOCKS):
            in_prev, toks = key_block(q, p)
            kps.append((kbp_ref if in_prev else kbc_ref)[0, toks, hp * LANES:(hp + 1) * LANES])
        raw = _dot(jnp.concatenate(kps, axis=0), qm)
        return [b_tiles(h, p, raw[p * TQ:(p + 1) * TQ]) for p in range(B_BLOCKS)]

    def b_tiles(h, p, raw):
        tiles = {}
        for rt, lt in B_LIVE_TILES[p]:
            ks = slice(rt * A_PAIR, (rt + 1) * A_PAIR)
            qs = slice(lt * LANES, (lt + 1) * LANES)
            tiles[rt, lt] = raw[ks, qs] + bias_b_ref[h, p * TQ + rt * A_PAIR:p * TQ + (rt + 1) * A_PAIR, qs]
        mcols = []
        for lt in range(TQ // LANES):
            col = functools.reduce(jnp.maximum, [t for (_, l2), t in tiles.items() if l2 == lt])
            mcols.append(jnp.max(col, axis=0, keepdims=True))
        return tiles, mcols

    def b_output(q, h, p, tiles, mcols):
        rows = slice(h * HEAD_DIM, (h + 1) * HEAD_DIM)
        tokq = slice(q * TQ, (q + 1) * TQ)
        dead = jnp.zeros((A_PAIR, LANES), BF16)
        pt = jnp.concatenate([
            jnp.concatenate([jnp.exp2(tiles[rt, lt] - mcols[lt]).astype(BF16) if (rt, lt) in tiles else dead
                             for rt in range(TQ // A_PAIR)], axis=0)
            for lt in range(TQ // LANES)], axis=1)
        mp = jnp.concatenate(mcols, axis=1)
        in_prev, toks = key_block(q, p)
        if in_prev:
            mp = jnp.where(i == 0, NEG_INF, mp)
            vh = vbp_ref[0, rows, toks]
        else:
            vh = ft_ref[0, FT_VB + h * HEAD_DIM:FT_VB + (h + 1) * HEAD_DIM, toks]
        vt = jnp.concatenate([vh, ones_k], axis=0)
        b_parts.setdefault((q, h), []).append((_dot(vt, pt)[:HEAD_DIM + 8], mp))
        if p == B_BLOCKS - 1:
            parts = b_parts.pop((q, h))
            m = functools.reduce(jnp.maximum, [mq for _, mq in parts])
            ot = sum(op * jnp.exp2(mq - m) for op, mq in parts)
            yt = ot[:HEAD_DIM] * (1.0 / ot[HEAD_DIM:HEAD_DIM + 1])
            sumsq_b[q] = sumsq_b[q] + rowgroup_sumsq(yt)
            yb_scr[rows, tokq] = yt * gb_ref[rows, :]

    def a_windows(r):
        if r == 0:
            kvwin = jnp.concatenate([kvap_ref[0], ft_ref[0, FT_KVA:, :A_PAIR]], axis=1)
        else:
            kvwin = ft_ref[0, FT_KVA:, (r - 1) * A_PAIR:(r + 1) * A_PAIR]
        return kvwin[:KA_W], kvwin[KA_W:]

    def a_scores(r, kvh):
        tok = slice(r * A_PAIR, (r + 1) * A_PAIR)
        kwin, _ = a_windows(r)
        blocks = []
        for g in range(A_GROUP):
            h = kvh * A_GROUP + g
            qt = ft_ref[0, h * HEAD_DIM:(h + 1) * HEAD_DIM, tok]
            blocks.append(jnp.concatenate([qt, zeros_p] if kvh == 0 else [zeros_p, qt], axis=0))
        qst = jnp.concatenate(blocks, axis=1)
        st = _dot_tn(kwin, qst) + bias_a_ref[kvh]
        s0 = st[:A_PAIR]
        s1 = st[A_PAIR:]
        if r == 0:
            s0 = s0 + pen_a
        m = jnp.maximum(jnp.max(jnp.maximum(s0, s1), axis=0, keepdims=True), sink_ref[kvh])
        return (s0, s1), m

    def a_output(r, kvh, st, m):
        tok = slice(r * A_PAIR, (r + 1) * A_PAIR)
        _, vwin = a_windows(r)
        pt = jnp.concatenate([jnp.exp2(st[0] - m), jnp.exp2(st[1] - m)], axis=0).astype(BF16)
        vt = jnp.concatenate([vwin[kvh * HEAD_DIM:(kvh + 1) * HEAD_DIM, :], ones_k], axis=0)
        ot = _dot(vt, pt)
        den = ot[HEAD_DIM:HEAD_DIM + 1] + jnp.exp2(sink_ref[kvh] - m)
        yt = ot[:HEAD_DIM] * (1.0 / den)
        for g in range(A_GROUP):
            rows = slice((kvh * A_GROUP + g) * HEAD_DIM, (kvh * A_GROUP + g + 1) * HEAD_DIM)
            yh = yt[:, g * A_PAIR:(g + 1) * A_PAIR]
            sumsq_a[r] = sumsq_a[r] + rowgroup_sumsq(yh)
            ya_scr[rows, tok] = yh * ga_ref[rows, :]

    def b_finish(q):
        tokq = slice(q * TQ, (q + 1) * TQ)
        y_ref[0, QA_W:, tokq] = (yb_scr[:, tokq] * inv_rms(sumsq_b[q], QB_W)).astype(BF16)

    def a_finish(r):
        tok = slice(r * A_PAIR, (r + 1) * A_PAIR)
        y_ref[0, :QA_W, tok] = (ya_scr[:, tok] * inv_rms(sumsq_a[r], QA_W)).astype(BF16)

    groups = []
    for q in range(Q_PER_STEP):
        groups += [(b_scores, (q, h),
                    [(b_output, (q, h, p),
                      functools.partial(b_finish, q) if (h, p) == (B_HEADS - 1, B_BLOCKS - 1) else None)
                     for p in range(B_BLOCKS)])
                   for h in range(B_HEADS)]
        groups += [(lambda *a: [a_scores(*a)], (r, kvh),
                    [(a_output, (r, kvh), functools.partial(a_finish, r) if kvh == A_KV_HEADS - 1 else None)])
                   for r in range(q * TQ // A_PAIR, (q + 1) * TQ // A_PAIR) for kvh in range(A_KV_HEADS)]
    pending = []

    def run_output():
        output, args, staged, finish = pending.pop(0)
        output(*args, *staged)
        if finish is not None:
            finish()

    for scores, sargs, outputs in groups:
        for (output, oargs, finish), staged in zip(outputs, scores(*sargs)):
            pending.append((output, oargs, staged, finish))
        while len(pending) > PIPE_DEPTH:
            run_output()
    while pending:
        run_output()


def _attn_call(kb, ft, bias_a, sink_a, bias_b, ga, gb):
    b, s, _ = kb.shape
    assert FT_VB % QB_W == 0 and FT_KVA % (2 * KA_W) == 0
    look = B_LOOKBACK * TQ
    assert TS % look == 0
    pairs_per_step = TS // A_PAIR

    def look_block(i):
        return jnp.maximum(i * (TS // look) - 1, 0)

    in_specs = [
        pl.BlockSpec((1, FT_ROWS, TS), lambda bi, i: (bi, 0, i)),
        pl.BlockSpec((1, look, QB_W), lambda bi, i: (bi, look_block(i), 0)),
        pl.BlockSpec((1, TS, QB_W), lambda bi, i: (bi, i, 0)),
        pl.BlockSpec((1, QB_W, look), lambda bi, i: (bi, FT_VB // QB_W, look_block(i))),
        pl.BlockSpec((1, 2 * KA_W, A_PAIR),
                     lambda bi, i: (bi, FT_KVA // (2 * KA_W), jnp.maximum(pairs_per_step * i - 1, 0))),
        _const_spec(bias_a.shape), _const_spec(sink_a.shape), _const_spec(bias_b.shape),
        _const_spec(ga.shape), _const_spec(gb.shape),
    ]
    return pl.pallas_call(
        _attn_kernel,
        grid=(b, s // TS),
        in_specs=in_specs,
        out_specs=pl.BlockSpec((1, MIX_W, TS), lambda bi, i: (bi, 0, i)),
        out_shape=jax.ShapeDtypeStruct((b, MIX_W, s), BF16),
        scratch_shapes=[pltpu.VMEM((QA_W, TS), F32), pltpu.VMEM((QB_W, TS), F32)],
        compiler_params=pltpu.CompilerParams(
            dimension_semantics=("arbitrary", "arbitrary"), vmem_limit_bytes=VMEM_LIMIT),
        name="attention",
    )(ft, kb, kb, ft, ft, bias_a, sink_a, bias_b, ga, gb)


def _ffn_kernel(yt_ref, x_ref, wo_ref, g2_ref, w1_ref, w2_ref, gf_ref, o_ref, h_scr, n2_scr):
    h = x_ref[0] + _dot_tn(yt_ref[0], wo_ref[...])
    h_scr[...] = h
    n2_scr[...] = _rms(h, g2_ref[...]).astype(BF16)

    def mlp_chunk(rows, c):
        cols = slice(c * FF_CHUNK, (c + 1) * FF_CHUNK)
        u = _dot(n2_scr[rows, :], w1_ref[:, cols])
        u = jnp.square(jnp.maximum(u, 0.0)).astype(BF16)
        return _dot(u, w2_ref[cols, :])

    last = D_FF // FF_CHUNK - 1
    for c in range(last):
        h_scr[...] += mlp_chunk(slice(None), c)
    for blk in range(TM_FFN // FFN_TAIL_ROWS):
        rows = slice(blk * FFN_TAIL_ROWS, (blk + 1) * FFN_TAIL_ROWS)
        o_ref[0, rows, :] = _rms(h_scr[rows, :] + mlp_chunk(rows, last), gf_ref[...])


def _ffn_call(yt, x, wo, g2, w1, w2, gf):
    b, s, _ = x.shape
    return pl.pallas_call(
        _ffn_kernel,
        grid=(b, s // TM_FFN),
        in_specs=[
            pl.BlockSpec((1, MIX_W, TM_FFN), lambda bi, i: (bi, 0, i)),
            pl.BlockSpec((1, TM_FFN, D_MODEL), lambda bi, i: (bi, i, 0)),
            _const_spec(wo.shape), _const_spec(g2.shape),
            _const_spec(w1.shape), _const_spec(w2.shape), _const_spec(gf.shape),
        ],
        out_specs=pl.BlockSpec((1, TM_FFN, D_MODEL), lambda bi, i: (bi, i, 0)),
        out_shape=jax.ShapeDtypeStruct((b, s, D_MODEL), F32),
        scratch_shapes=[pltpu.VMEM((TM_FFN, D_MODEL), F32), pltpu.VMEM((TM_FFN, D_MODEL), BF16)],
        compiler_params=pltpu.CompilerParams(
            dimension_semantics=("arbitrary", "arbitrary"), vmem_limit_bytes=VMEM_LIMIT),
        name="out_proj_mlp",
    )(yt, x, wo, g2, w1, w2, gf)


def _bias_a_table(sinks):
    k = np.arange(A_WIN)[:, None]
    i = np.arange(A_PAIR)[None, :]
    dist = np.abs(A_PAIR + i - k).astype(np.float32)
    qc = i // CHUNK
    kc = k // CHUNK
    allowed = (kc >= qc) & (kc <= qc + A_BAND_CHUNKS - 1)
    slopes = jnp.exp2(-8.0 * (jnp.arange(A_HEADS, dtype=F32) + 1.0) / A_HEADS)
    bias = -slopes[:, None, None] * jnp.asarray(dist)[None] * LOG2E
    bias = jnp.where(jnp.asarray(allowed)[None], bias, NEG_INF)
    bias = bias.reshape(A_KV_HEADS, A_GROUP, A_WIN, A_PAIR).transpose(0, 2, 1, 3)
    bias = bias.reshape(A_KV_HEADS, A_WIN, A_GROUP * A_PAIR)
    sink = jnp.broadcast_to((sinks.astype(F32) * LOG2E).reshape(A_KV_HEADS, 1, A_GROUP, 1),
                            (A_KV_HEADS, 1, A_GROUP, A_PAIR)).reshape(A_KV_HEADS, 1, A_GROUP * A_PAIR)
    return bias, sink


def _bias_b_table(rel_bias):
    rb = rel_bias.astype(F32) * LOG2E
    base = (B_BAND_CHUNKS - 1) * CHUNK

    def tile(delta):
        lo = base + LANES * delta - (LANES - 1)
        if lo >= B_MAX_REL:
            return jnp.broadcast_to(rb[:, -1][:, None, None], (B_HEADS, LANES, LANES))
        off = np.arange(2 * LANES)
        off = np.where(off >= LANES, off - 2 * LANES, off)
        rel = np.clip(base + LANES * delta + off, -B_MAX_REL, B_MAX_REL) + B_MAX_REL
        row = rb[:, jnp.asarray(rel)]
        flat = jnp.tile(row, (1, LANES))[:, :LANES * (2 * LANES - 1)]
        return flat.reshape(B_HEADS, LANES, 2 * LANES - 1)[:, :, :LANES]

    tiles = {delta: tile(delta) for delta in range(-(B_WIN // LANES - 1), TQ // LANES)}
    bias = jnp.concatenate(
        [jnp.concatenate([tiles[t - a] for t in range(TQ // LANES)], axis=2) for a in range(B_WIN // LANES)],
        axis=1)
    k = np.arange(B_WIN)[:, None]
    q = np.arange(TQ)[None, :]
    qc = q // CHUNK
    kc = k // CHUNK
    allowed = (kc >= qc) & (kc <= qc + B_BAND_CHUNKS - 1)
    return jnp.where(jnp.asarray(allowed)[None], bias, NEG_INF)


def kernel(x, norm1_g, w_in, sinks_a, rel_bias_b, out_norm_a_g, out_norm_b_g, w_out, norm2_g,
           w_ff1, w_ff2, final_norm_g):
    b, s, d = x.shape
    assert d == D_MODEL and s % TM_PROJ == 0 and s % TM_FFN == 0 and s % TS == 0
    assert norm1_g.shape[0] == 1, "single-layer block"
    assert w_in.shape[2] == 2 * QA_W + 2 * KA_W + 2 * QB_W
    kb, ft, wo, w1, w2 = _proj_call(
        x, norm1_g[0].reshape(1, d), w_in[0], w_out[0], w_ff1[0], w_ff2[0])

    bias_a, sink_a = _bias_a_table(sinks_a[0])
    bias_b = _bias_b_table(rel_bias_b[0])
    ga = jnp.broadcast_to(out_norm_a_g[0].astype(F32)[:, None], (QA_W, A_PAIR))
    gb = jnp.broadcast_to(out_norm_b_g[0].astype(F32)[:, None], (QB_W, TQ))
    yt = _attn_call(kb, ft, bias_a, sink_a, bias_b, ga, gb)

    return _ffn_call(yt, x, wo, norm2_g[0].reshape(1, d), w1, w2, final_norm_g.reshape(1, d))
```

```python
import functools

import jax
import jax.numpy as jnp
import numpy as np
from jax import lax
from jax.experimental import pallas as pl
from jax.experimental.pallas import tpu as pltpu

D_MODEL = 1024
CHUNK = 64
HEAD_DIM = 64
A_HEADS = 8
A_KV_HEADS = 2
A_GROUP = A_HEADS // A_KV_HEADS
A_BAND_CHUNKS = 3
B_HEADS = 8
B_BAND_CHUNKS = 9
B_MAX_REL = 128
D_FF = 4 * D_MODEL
EPS = 1e-6
NEG_INF = -1e30
LOG2E = 1.4426950408889634

QA_W = A_HEADS * HEAD_DIM
KA_W = A_KV_HEADS * HEAD_DIM
QB_W = B_HEADS * HEAD_DIM
MIX_W = QA_W + QB_W
FT_QB, FT_VB, FT_KVA, FT_ROWS = QA_W, QA_W + QB_W, QA_W + 2 * QB_W, QA_W + 2 * QB_W + 2 * KA_W

LANES = 128
SUBLANES = 8
TQ = 256
Q_PER_STEP = 2
TS = Q_PER_STEP * TQ
A_PAIR = 2 * CHUNK
A_WIN = 4 * CHUNK
B_BLOCKS = 3
B_WIN = B_BLOCKS * TQ
B_LOOKBACK = B_BLOCKS - 1
ONES_ROWS = 16
PIPE_DEPTH = 7
TM_PROJ = 1024
PROJ_SUB = 512
TM_FFN = 1024
FF_CHUNK = 1024
FFN_TAIL_ROWS = 256
V7X_VMEM_BYTES = 64 * 1024 * 1024
VMEM_LIMIT = V7X_VMEM_BYTES * 7 // 8

F32 = jnp.float32
BF16 = jnp.bfloat16


def _b_live_tiles():
    chunks_per_tile = A_PAIR // CHUNK
    live = []
    for p in range(B_BLOCKS):
        tiles = []
        for rt in range(TQ // A_PAIR):
            for lt in range(TQ // LANES):
                kcs = [p * (TQ // CHUNK) + rt * chunks_per_tile + i for i in range(chunks_per_tile)]
                qcs = [lt * (LANES // CHUNK) + i for i in range(LANES // CHUNK)]
                if any(qc <= kc <= qc + B_BAND_CHUNKS - 1 for kc in kcs for qc in qcs):
                    tiles.append((rt, lt))
        live.append(tuple(tiles))
    return tuple(live)


B_LIVE_TILES = _b_live_tiles()


def _rms(x, g):
    ms = jnp.mean(x * x, axis=-1, keepdims=True)
    return x * lax.rsqrt(ms + EPS) * g


def _rms_rows(xt, g):
    ms = jnp.mean(xt * xt, axis=0, keepdims=True)
    return xt * lax.rsqrt(ms + EPS) * g


def _dot(a, b):
    return jnp.dot(a, b, preferred_element_type=F32)


def _dot_nt(a, b):
    return lax.dot_general(a, b, (((1,), (1,)), ((), ())), preferred_element_type=F32)


def _dot_tn(a, b):
    return lax.dot_general(a, b, (((0,), (0,)), ((), ())), preferred_element_type=F32)


def _const_spec(shape):
    nd = len(shape)
    return pl.BlockSpec(shape, lambda *_: (0,) * nd, pipeline_mode=pl.Buffered(1))


def _proj_kernel(x_ref, g_ref, win_ref, wo_ref, w1_ref, w2_ref,
                 kb_ref, ft_ref, wo_bf_ref, w1_bf_ref, w2_bf_ref,
                 wk_scr, wt_scr):
    @pl.when((pl.program_id(0) == 0) & (pl.program_id(1) == 0))
    def _prepare_weights():
        o_ka, o_qb, o_kb, o_vb = QA_W, QA_W + 2 * KA_W, QA_W + 2 * KA_W + QB_W, QA_W + 2 * KA_W + 2 * QB_W
        q_scale = HEAD_DIM ** -0.5 * LOG2E
        wk_scr[...] = win_ref[:, o_kb:o_vb].astype(BF16)
        row = 0
        for lo, hi, scale in ((0, o_ka, q_scale), (o_qb, o_kb, q_scale), (o_vb, o_vb + QB_W, None),
                              (o_ka, o_qb, None)):
            part = win_ref[:, lo:hi]
            if scale is not None:
                part = part * scale
            wt_scr[row:row + hi - lo, :] = part.T.astype(BF16)
            row += hi - lo

    for i in range(TM_PROJ // PROJ_SUB):
        tok = slice(i * PROJ_SUB, (i + 1) * PROJ_SUB)
        n = _rms(x_ref[0, tok, :], g_ref[...]).astype(BF16)
        kb_ref[0, tok, :] = _dot(n, wk_scr[...]).astype(BF16)
        ft_ref[0, :, tok] = _dot_nt(wt_scr[...], n).astype(BF16)
    wo_bf_ref[...] = wo_ref[...].astype(BF16)
    w1_bf_ref[...] = w1_ref[...].astype(BF16)
    w2_bf_ref[...] = w2_ref[...].astype(BF16)


def _proj_call(x, g, w_in, wo, w1, w2):
    b, s, _ = x.shape
    steps = b * (s // TM_PROJ)

    def tok(width):
        return pl.BlockSpec((1, TM_PROJ, width), lambda bi, i: (bi, i, 0))

    def feat(width):
        return pl.BlockSpec((1, width, TM_PROJ), lambda bi, i: (bi, 0, i))

    def row_share(w):
        rows = w.shape[0] // steps
        assert rows * steps == w.shape[0] and rows % 16 == 0, "weight rows must split into bf16 row tiles"
        return pl.BlockSpec((rows, w.shape[1]), lambda bi, i: (bi * (s // TM_PROJ) + i, 0))

    def bf(w):
        return jax.ShapeDtypeStruct(w.shape, BF16)

    return pl.pallas_call(
        _proj_kernel,
        grid=(b, s // TM_PROJ),
        in_specs=[tok(D_MODEL), _const_spec(g.shape), _const_spec(w_in.shape),
                  row_share(wo), row_share(w1), row_share(w2)],
        out_specs=[tok(QB_W), feat(FT_ROWS), row_share(wo), row_share(w1), row_share(w2)],
        out_shape=[jax.ShapeDtypeStruct((b, s, QB_W), BF16), jax.ShapeDtypeStruct((b, FT_ROWS, s), BF16),
                   bf(wo), bf(w1), bf(w2)],
        scratch_shapes=[pltpu.VMEM((D_MODEL, QB_W), BF16),
                        pltpu.VMEM((w_in.shape[1] - QB_W, D_MODEL), BF16)],
        compiler_params=pltpu.CompilerParams(
            dimension_semantics=("arbitrary", "arbitrary"), vmem_limit_bytes=VMEM_LIMIT),
        name="norm_in_proj",
    )(x, g, w_in, wo, w1, w2)


def _attn_kernel(ft_ref, kbp_ref, kbc_ref, vbp_ref, kvap_ref,
                 bias_a_ref, sink_ref, bias_b_ref, ga_ref, gb_ref,
                 y_ref, ya_scr, yb_scr):
    i = pl.program_id(1)
    zeros_q = jnp.zeros((HEAD_DIM, TQ), BF16)
    ones_k = jnp.ones((ONES_ROWS, TQ), BF16)
    zeros_p = jnp.zeros((HEAD_DIM, A_PAIR), BF16)
    pen_a = jnp.where(i >= 1, 0.0, NEG_INF).astype(F32)

    def key_block(q, p):
        g = q + p - B_LOOKBACK
        return (True, slice((g + B_LOOKBACK) * TQ, (g + B_LOOKBACK + 1) * TQ)) if g < 0 else \
            (False, slice(g * TQ, (g + 1) * TQ))


    b_parts = {}
    sumsq_b = [jnp.zeros((SUBLANES, TQ), F32) for _ in range(Q_PER_STEP)]
    sumsq_a = [jnp.zeros((SUBLANES, A_PAIR), F32) for _ in range(TS // A_PAIR)]

    def rowgroup_sumsq(t):
        return jnp.sum((t * t).reshape(t.shape[0] // SUBLANES, SUBLANES, t.shape[1]), axis=0)

    def inv_rms(sumsq, width):
        return lax.rsqrt(jnp.sum(sumsq, axis=0, keepdims=True) / width + EPS)

    def b_scores(q, h, p):
        hp, half = divmod(h, 2)
        qt = ft_ref[0, FT_QB + h * HEAD_DIM:FT_QB + (h + 1) * HEAD_DIM, q * TQ:(q + 1) * TQ]
        qm = jnp.concatenate([qt, zeros_q] if half == 0 else [zeros_q, qt], axis=0)
        in_prev, toks = key_block(q, p)
        kp = (kbp_ref if in_prev else kbc_ref)[0, toks, hp * LANES:(hp + 1) * LANES]
        raw = _dot(kp, qm)
        tiles = {}
        for rt, lt in B_LIVE_TILES[p]:
            ks = slice(rt * A_PAIR, (rt + 1) * A_PAIR)
            qs = slice(lt * LANES, (lt + 1) * LANES)
            tiles[rt, lt] = raw[ks, qs] + bias_b_ref[h, p * TQ + rt * A_PAIR:p * TQ + (rt + 1) * A_PAIR, qs]
        mcols = []
        for lt in range(TQ // LANES):
            col = functools.reduce(jnp.maximum, [t for (_, l2), t in tiles.items() if l2 == lt])
            mcols.append(jnp.max(col, axis=0, keepdims=True))
        return tiles, mcols

    def b_output(q, h, p, tiles, mcols):
        rows = slice(h * HEAD_DIM, (h + 1) * HEAD_DIM)
        tokq = slice(q * TQ, (q + 1) * TQ)
        dead = jnp.zeros((A_PAIR, LANES), BF16)
        pt = jnp.concatenate([
            jnp.concatenate([jnp.exp2(tiles[rt, lt] - mcols[lt]).astype(BF16) if (rt, lt) in tiles else dead
                             for rt in range(TQ // A_PAIR)], axis=0)
            for lt in range(TQ // LANES)], axis=1)
        mp = jnp.concatenate(mcols, axis=1)
        in_prev, toks = key_block(q, p)
        if in_prev:
            mp = jnp.where(i == 0, NEG_INF, mp)
            vh = vbp_ref[0, rows, toks]
        else:
            vh = ft_ref[0, FT_VB + h * HEAD_DIM:FT_VB + (h + 1) * HEAD_DIM, toks]
        vt = jnp.concatenate([vh, ones_k], axis=0)
        b_parts.setdefault((q, h), []).append((_dot(vt, pt)[:HEAD_DIM + 8], mp))
        if p == B_BLOCKS - 1:
            parts = b_parts.pop((q, h))
            m = functools.reduce(jnp.maximum, [mq for _, mq in parts])
            ot = sum(op * jnp.exp2(mq - m) for op, mq in parts)
            yt = ot[:HEAD_DIM] * (1.0 / ot[HEAD_DIM:HEAD_DIM + 1])
            sumsq_b[q] = sumsq_b[q] + rowgroup_sumsq(yt)
            yb_scr[rows, tokq] = yt * gb_ref[rows, :]

    def a_windows(r):
        if r == 0:
            kvwin = jnp.concatenate([kvap_ref[0], ft_ref[0, FT_KVA:, :A_PAIR]], axis=1)
        else:
            kvwin = ft_ref[0, FT_KVA:, (r - 1) * A_PAIR:(r + 1) * A_PAIR]
        return kvwin[:KA_W], kvwin[KA_W:]

    def a_scores(r, kvh):
        tok = slice(r * A_PAIR, (r + 1) * A_PAIR)
        kwin, _ = a_windows(r)
        blocks = []
        for g in range(A_GROUP):
            h = kvh * A_GROUP + g
            qt = ft_ref[0, h * HEAD_DIM:(h + 1) * HEAD_DIM, tok]
            blocks.append(jnp.concatenate([qt, zeros_p] if kvh == 0 else [zeros_p, qt], axis=0))
        qst = jnp.concatenate(blocks, axis=1)
        st = _dot_tn(kwin, qst) + bias_a_ref[kvh]
        s0 = st[:A_PAIR]
        s1 = st[A_PAIR:]
        if r == 0:
            s0 = s0 + pen_a
        m = jnp.maximum(jnp.max(jnp.maximum(s0, s1), axis=0, keepdims=True), sink_ref[kvh])
        return (s0, s1), m

    def a_output(r, kvh, st, m):
        tok = slice(r * A_PAIR, (r + 1) * A_PAIR)
        _, vwin = a_windows(r)
        pt = jnp.concatenate([jnp.exp2(st[0] - m), jnp.exp2(st[1] - m)], axis=0).astype(BF16)
        vt = jnp.concatenate([vwin[kvh * HEAD_DIM:(kvh + 1) * HEAD_DIM, :], ones_k], axis=0)
        ot = _dot(vt, pt)
        den = ot[HEAD_DIM:HEAD_DIM + 1] + jnp.exp2(sink_ref[kvh] - m)
        yt = ot[:HEAD_DIM] * (1.0 / den)
        for g in range(A_GROUP):
            rows = slice((kvh * A_GROUP + g) * HEAD_DIM, (kvh * A_GROUP + g + 1) * HEAD_DIM)
            yh = yt[:, g * A_PAIR:(g + 1) * A_PAIR]
            sumsq_a[r] = sumsq_a[r] + rowgroup_sumsq(yh)
            ya_scr[rows, tok] = yh * ga_ref[rows, :]

    def b_finish(q):
        tokq = slice(q * TQ, (q + 1) * TQ)
        y_ref[0, QA_W:, tokq] = (yb_scr[:, tokq] * inv_rms(sumsq_b[q], QB_W)).astype(BF16)

    def a_finish(r):
        tok = slice(r * A_PAIR, (r + 1) * A_PAIR)
        y_ref[0, :QA_W, tok] = (ya_scr[:, tok] * inv_rms(sumsq_a[r], QA_W)).astype(BF16)

    units = []
    for q in range(Q_PER_STEP):
        units += [(b_scores, b_output, (q, h, p),
                   functools.partial(b_finish, q) if (h, p) == (B_HEADS - 1, B_BLOCKS - 1) else None)
                  for h in range(B_HEADS) for p in range(B_BLOCKS)]
        units += [(a_scores, a_output, (r, kvh), functools.partial(a_finish, r) if kvh == A_KV_HEADS - 1 else None)
                  for r in range(q * TQ // A_PAIR, (q + 1) * TQ // A_PAIR) for kvh in range(A_KV_HEADS)]
    pending = []

    def run_output():
        output, args, staged, finish = pending.pop(0)
        output(*args, *staged)
        if finish is not None:
            finish()

    for scores, output, args, finish in units:
        pending.append((output, args, scores(*args), finish))
        if len(pending) > PIPE_DEPTH:
            run_output()
    while pending:
        run_output()


def _attn_call(kb, ft, bias_a, sink_a, bias_b, ga, gb):
    b, s, _ = kb.shape
    assert FT_VB % QB_W == 0 and FT_KVA % (2 * KA_W) == 0
    look = B_LOOKBACK * TQ
    assert TS % look == 0
    pairs_per_step = TS // A_PAIR

    def look_block(i):
        return jnp.maximum(i * (TS // look) - 1, 0)

    in_specs = [
        pl.BlockSpec((1, FT_ROWS, TS), lambda bi, i: (bi, 0, i)),
        pl.BlockSpec((1, look, QB_W), lambda bi, i: (bi, look_block(i), 0)),
        pl.BlockSpec((1, TS, QB_W), lambda bi, i: (bi, i, 0)),
        pl.BlockSpec((1, QB_W, look), lambda bi, i: (bi, FT_VB // QB_W, look_block(i))),
        pl.BlockSpec((1, 2 * KA_W, A_PAIR),
                     lambda bi, i: (bi, FT_KVA // (2 * KA_W), jnp.maximum(pairs_per_step * i - 1, 0))),
        _const_spec(bias_a.shape), _const_spec(sink_a.shape), _const_spec(bias_b.shape),
        _const_spec(ga.shape), _const_spec(gb.shape),
    ]
    return pl.pallas_call(
        _attn_kernel,
        grid=(b, s // TS),
        in_specs=in_specs,
        out_specs=pl.BlockSpec((1, MIX_W, TS), lambda bi, i: (bi, 0, i)),
        out_shape=jax.ShapeDtypeStruct((b, MIX_W, s), BF16),
        scratch_shapes=[pltpu.VMEM((QA_W, TS), F32), pltpu.VMEM((QB_W, TS), F32)],
        compiler_params=pltpu.CompilerParams(
            dimension_semantics=("parallel", "parallel"), vmem_limit_bytes=VMEM_LIMIT),
        name="attention",
    )(ft, kb, kb, ft, ft, bias_a, sink_a, bias_b, ga, gb)


def _ffn_kernel(yt_ref, x_ref, wo_ref, g2_ref, w1_ref, w2_ref, gf_ref, o_ref, h_scr, n2_scr):
    h = x_ref[0] + _dot_tn(yt_ref[0], wo_ref[...])
    h_scr[...] = h
    n2_scr[...] = _rms(h, g2_ref[...]).astype(BF16)

    def mlp_chunk(rows, c):
        cols = slice(c * FF_CHUNK, (c + 1) * FF_CHUNK)
        u = _dot(n2_scr[rows, :], w1_ref[:, cols])
        u = jnp.square(jnp.maximum(u, 0.0)).astype(BF16)
        return _dot(u, w2_ref[cols, :])

    last = D_FF // FF_CHUNK - 1
    for c in range(last):
        h_scr[...] += mlp_chunk(slice(None), c)
    for blk in range(TM_FFN // FFN_TAIL_ROWS):
        rows = slice(blk * FFN_TAIL_ROWS, (blk + 1) * FFN_TAIL_ROWS)
        o_ref[0, rows, :] = _rms(h_scr[rows, :] + mlp_chunk(rows, last), gf_ref[...])


def _ffn_call(yt, x, wo, g2, w1, w2, gf):
    b, s, _ = x.shape
    return pl.pallas_call(
        _ffn_kernel,
        grid=(b, s // TM_FFN),
        in_specs=[
            pl.BlockSpec((1, MIX_W, TM_FFN), lambda bi, i: (bi, 0, i)),
            pl.BlockSpec((1, TM_FFN, D_MODEL), lambda bi, i: (bi, i, 0)),
            _const_spec(wo.shape), _const_spec(g2.shape),
            _const_spec(w1.shape), _const_spec(w2.shape), _const_spec(gf.shape),
        ],
        out_specs=pl.BlockSpec((1, TM_FFN, D_MODEL), lambda bi, i: (bi, i, 0)),
        out_shape=jax.ShapeDtypeStruct((b, s, D_MODEL), F32),
        scratch_shapes=[pltpu.VMEM((TM_FFN, D_MODEL), F32), pltpu.VMEM((TM_FFN, D_MODEL), BF16)],
        compiler_params=pltpu.CompilerParams(
            dimension_semantics=("parallel", "parallel"), vmem_limit_bytes=VMEM_LIMIT),
        name="out_proj_mlp",
    )(yt, x, wo, g2, w1, w2, gf)


def _bias_a_table(sinks):
    k = np.arange(A_WIN)[:, None]
    i = np.arange(A_PAIR)[None, :]
    dist = np.abs(A_PAIR + i - k).astype(np.float32)
    qc = i // CHUNK
    kc = k // CHUNK
    allowed = (kc >= qc) & (kc <= qc + A_BAND_CHUNKS - 1)
    slopes = jnp.exp2(-8.0 * (jnp.arange(A_HEADS, dtype=F32) + 1.0) / A_HEADS)
    bias = -slopes[:, None, None] * jnp.asarray(dist)[None] * LOG2E
    bias = jnp.where(jnp.asarray(allowed)[None], bias, NEG_INF)
    bias = bias.reshape(A_KV_HEADS, A_GROUP, A_WIN, A_PAIR).transpose(0, 2, 1, 3)
    bias = bias.reshape(A_KV_HEADS, A_WIN, A_GROUP * A_PAIR)
    sink = jnp.broadcast_to((sinks.astype(F32) * LOG2E).reshape(A_KV_HEADS, 1, A_GROUP, 1),
                            (A_KV_HEADS, 1, A_GROUP, A_PAIR)).reshape(A_KV_HEADS, 1, A_GROUP * A_PAIR)
    return bias, sink


def _bias_b_table(rel_bias):
    rb = rel_bias.astype(F32) * LOG2E
    base = (B_BAND_CHUNKS - 1) * CHUNK

    def tile(delta):
        lo = base + LANES * delta - (LANES - 1)
        if lo >= B_MAX_REL:
            return jnp.broadcast_to(rb[:, -1][:, None, None], (B_HEADS, LANES, LANES))
        off = np.arange(2 * LANES)
        off = np.where(off >= LANES, off - 2 * LANES, off)
        rel = np.clip(base + LANES * delta + off, -B_MAX_REL, B_MAX_REL) + B_MAX_REL
        row = rb[:, jnp.asarray(rel)]
        flat = jnp.tile(row, (1, LANES))[:, :LANES * (2 * LANES - 1)]
        return flat.reshape(B_HEADS, LANES, 2 * LANES - 1)[:, :, :LANES]

    tiles = {delta: tile(delta) for delta in range(-(B_WIN // LANES - 1), TQ // LANES)}
    bias = jnp.concatenate(
        [jnp.concatenate([tiles[t - a] for t in range(TQ // LANES)], axis=2) for a in range(B_WIN // LANES)],
        axis=1)
    k = np.arange(B_WIN)[:, None]
    q = np.arange(TQ)[None, :]
    qc = q // CHUNK
    kc = k // CHUNK
    allowed = (kc >= qc) & (kc <= qc + B_BAND_CHUNKS - 1)
    return jnp.where(jnp.asarray(allowed)[None], bias, NEG_INF)


def kernel(x, norm1_g, w_in, sinks_a, rel_bias_b, out_norm_a_g, out_norm_b_g, w_out, norm2_g,
           w_ff1, w_ff2, final_norm_g):
    b, s, d = x.shape
    assert d == D_MODEL and s % TM_PROJ == 0 and s % TM_FFN == 0 and s % TS == 0
    assert norm1_g.shape[0] == 1, "single-layer block"
    assert w_in.shape[2] == 2 * QA_W + 2 * KA_W + 2 * QB_W
    kb, ft, wo, w1, w2 = _proj_call(
        x, norm1_g[0].reshape(1, d), w_in[0], w_out[0], w_ff1[0], w_ff2[0])

    bias_a, sink_a = _bias_a_table(sinks_a[0])
    bias_b = _bias_b_table(rel_bias_b[0])
    ga = jnp.broadcast_to(out_norm_a_g[0].astype(F32)[:, None], (QA_W, A_PAIR))
    gb = jnp.broadcast_to(out_norm_b_g[0].astype(F32)[:, None], (QB_W, TQ))
    yt = _attn_call(kb, ft, bias_a, sink_a, bias_b, ga, gb)

    return _ffn_call(yt, x, wo, norm2_g[0].reshape(1, d), w1, w2, final_norm_g.reshape(1, d))
```
